```python
import math
import jax, jax.numpy as jnp
from jax import lax
import numpy as np

D_MODEL = 2048
BATCH = 4
SEQ = 4096
DEPTH = 2

CHUNK = 64
QB = 128
HEAD_DIM = 64
A_HEADS = 16
A_KV_HEADS = 4
WINDOW = 128
WINDOW_CHUNKS = WINDOW // CHUNK
B_HEADS = 16
KV_RANK = 256
IDX_HEADS = 8
IDX_DIM = 64
TOPK_MAX = 256
C_HEADS = 16
C_HEAD_DIM = 128
NUM_BUCKETS = 32
MAX_DISTANCE = 128
D_FF = 5632
CONV_W = 3
EPS = 1e-6

N_EVEN = (DEPTH + 1) // 2
N_ODD = DEPTH // 2
EVEN_SPLITS = (A_HEADS * HEAD_DIM, A_KV_HEADS * HEAD_DIM, A_KV_HEADS * HEAD_DIM,
               B_HEADS * HEAD_DIM, KV_RANK, IDX_HEADS * IDX_DIM, IDX_DIM, IDX_HEADS)
EVEN_IN = sum(EVEN_SPLITS)
MIX_WIDTH_EVEN = (A_HEADS + B_HEADS) * HEAD_DIM
MIX_WIDTH_ODD = C_HEADS * C_HEAD_DIM

kernel_name = 'hybrid_chunk_causal_swa_dsa_stickbreak_convffn'


def rms_norm(x, g):
    xf = x.astype(jnp.float32)
    y = xf * lax.rsqrt(jnp.mean(xf * xf, axis=-1, keepdims=True) + EPS)
    return (y * g.astype(jnp.float32)).astype(x.dtype)


def split_cols(a, sizes):
    offs = np.cumsum(sizes)[:-1].tolist()
    return jnp.split(a, offs, axis=-1)


def t5_bucket(rel):
    half = NUM_BUCKETS // 2
    max_exact = half // 2
    n = jnp.abs(rel)
    nf = jnp.maximum(n, 1).astype(jnp.float32)
    large = max_exact + (jnp.log(nf / max_exact) / math.log(MAX_DISTANCE / max_exact)
                         * (half - max_exact)).astype(jnp.int32)
    large = jnp.minimum(large, half - 1)
    return jnp.where(rel > 0, half, 0) + jnp.where(n < max_exact, n, large)


def to_blocks(a):
    return a.reshape(a.shape[0], a.shape[1] // QB, QB, *a.shape[2:]).swapaxes(0, 1)


def from_blocks(o):
    return o.swapaxes(0, 1).reshape(o.shape[1], o.shape[0] * QB, -1)


def swa_sink_attention(q, k, v, sinks, bias_table):
    Bn, S, H, Dh = q.shape
    KV = k.shape[2]
    G = H // KV
    nblk = S // QB

    def band(a):
        ap = jnp.pad(a, ((0, 0), (QB, 0), (0, 0), (0, 0))).reshape(Bn, nblk + 1, QB, KV, Dh)
        return jnp.concatenate([ap[:, :-1], ap[:, 1:]], axis=2)

    kb, vb = band(k), band(v)
    qg = q.reshape(Bn, nblk, QB, KV, G, Dh)
    s = jnp.einsum('bnqkgd,bnskd->bnkgqs', qg, kb).astype(jnp.float32) * Dh ** -0.5
    q_off = jnp.arange(QB)
    k_off = jnp.arange(2 * QB) - QB
    rel = k_off[None, :] - q_off[:, None]
    bias = bias_table.astype(jnp.float32)[t5_bucket(rel)]
    bias = jnp.transpose(bias, (2, 0, 1)).reshape(KV, G, QB, 2 * QB)
    start = jnp.arange(nblk)[:, None] * QB
    qpos = start + q_off
    kpos = start + k_off
    qc = qpos // CHUNK
    kc = jnp.floor_divide(kpos, CHUNK)
    allowed = ((kpos >= 0)[:, None, :]
               & (kc[:, None, :] <= qc[:, :, None])
               & (kc[:, None, :] >= qc[:, :, None] - WINDOW_CHUNKS))
    s = jnp.where(allowed[None, :, None, None], s + bias, -jnp.inf)
    sink = sinks.astype(jnp.float32).reshape(KV, G)[:, :, None, None]
    m = jnp.maximum(jnp.max(s, axis=-1, keepdims=True), sink)
    p = jnp.exp(s - m)
    p = p / (jnp.sum(p, axis=-1, keepdims=True) + jnp.exp(sink - m))
    o = jnp.einsum('bnkgqs,bnskd->bnqkgd', p.astype(v.dtype), vb)
    return o.reshape(Bn, S, H * Dh)


def dsa_attention(q, c_kv, q_idx, k_idx, w_idx, w_uk, w_uv, bias_table):
    Bn, S, H, Dh = q.shape
    topk = min(TOPK_MAX, S // 4)
    kchunk = jnp.arange(S) // CHUNK
    bias_table = bias_table.astype(jnp.float32)

    def block(args):
        qb, qib, wb, n = args
        qpos = n * QB + jnp.arange(QB)
        qchunk = qpos // CHUNK
        r = jax.nn.relu(jnp.einsum('bqhd,bsd->bqhs', qib, k_idx).astype(jnp.float32) * IDX_DIM ** -0.5)
        score = jnp.einsum('bqhs,bqh->bqs', r, wb.astype(jnp.float32) * IDX_HEADS ** -0.5)
        score = jnp.where((kchunk[None, :] <= qchunk[:, None])[None], score, -jnp.inf)
        _, idx = lax.top_k(score, topk)
        valid = (idx // CHUNK) <= qchunk[None, :, None]
        c_sel = jax.vmap(lambda cb, ib: cb[ib])(c_kv, idx)
        q_lat = jnp.einsum('bqhd,hdr->bqhr', qb, w_uk)
        s = jnp.einsum('bqhr,bqkr->bqhk', q_lat, c_sel).astype(jnp.float32) * Dh ** -0.5
        bias = bias_table[t5_bucket(idx - qpos[None, :, None])]
        s = jnp.where(valid[:, :, None, :], s + jnp.swapaxes(bias, 2, 3), -jnp.inf)
        p = jax.nn.softmax(s, axis=-1).astype(c_sel.dtype)
        o_lat = jnp.einsum('bqhk,bqkr->bqhr', p, c_sel)
        return jnp.einsum('bqhr,hrd->bqhd', o_lat, w_uv)

    out = lax.map(block, (to_blocks(q), to_blocks(q_idx), to_blocks(w_idx), jnp.arange(S // QB)))
    return from_blocks(out)


def stick_breaking_attention(q, k, v):
    Bn, S, H, Dh = q.shape
    kpos = jnp.arange(S)

    def block(args):
        qb, n = args
        qpos = n * QB + jnp.arange(QB)
        z = jnp.einsum('bqhd,bshd->bhqs', qb, k).astype(jnp.float32) * Dh ** -0.5
        earlier = kpos[None, :] < qpos[:, None]
        log_keep = jnp.where(earlier, jax.nn.log_sigmoid(-z), 0.0)
        between = lax.cumsum(log_keep, axis=3, reverse=True) - log_keep
        a = jnp.where(earlier, jnp.exp(jax.nn.log_sigmoid(z) + between), 0.0)
        return jnp.einsum('bhqs,bshd->bqhd', a.astype(v.dtype), v)

    return from_blocks(lax.map(block, (to_blocks(q), jnp.arange(S // QB))))


def even_mixer(h, w_in, kv_norm_g, w_uk, w_uv, sinks, w_out, rel_bias):
    Bn, S, _ = h.shape
    qa, ka, va, qb, c_lat, qi, ki, wi = split_cols(h @ w_in, EVEN_SPLITS)
    o_a = swa_sink_attention(qa.reshape(Bn, S, A_HEADS, HEAD_DIM),
                             ka.reshape(Bn, S, A_KV_HEADS, HEAD_DIM),
                             va.reshape(Bn, S, A_KV_HEADS, HEAD_DIM),
                             sinks, rel_bias[:, :A_HEADS])
    o_b = dsa_attention(qb.reshape(Bn, S, B_HEADS, HEAD_DIM), rms_norm(c_lat, kv_norm_g),
                        qi.reshape(Bn, S, IDX_HEADS, IDX_DIM), ki, wi, w_uk, w_uv,
                        rel_bias[:, A_HEADS:])
    return jnp.concatenate([o_a, o_b], axis=-1) @ w_out


def odd_mixer(h, w_in, w_out):
    Bn, S, _ = h.shape
    q, k, v = jnp.split(h @ w_in, 3, axis=-1)
    shp = (Bn, S, C_HEADS, C_HEAD_DIM)
    return stick_breaking_attention(q.reshape(shp), k.reshape(shp), v.reshape(shp)) @ w_out


def conv_ffn(h, w_up, conv_w, conv_b, w_down):
    S = h.shape[1]
    u = h @ w_up
    up = jnp.pad(u, ((0, 0), (CONV_W - 1, 0), (0, 0)))
    u = sum(conv_w[j] * up[:, j:j + S] for j in range(CONV_W)) + conv_b
    gate, val = jnp.split(u, 2, axis=-1)
    return (jax.nn.silu(gate) * val) @ w_down


def setup_inputs(seed: int = 0) -> dict:
    key = jax.random.key(seed)
    ks = jax.random.split(key, 21)
    D = D_MODEL

    def nrm(k, shape, scale):
        return jax.random.normal(k, shape, jnp.float32) * scale

    return {
        'x': nrm(ks[0], (BATCH, SEQ, D), 1.0),
        'c': nrm(ks[1], (BATCH, D), 1.0),
        'rel_bias': nrm(ks[2], (NUM_BUCKETS, A_HEADS + B_HEADS), 0.5),
        'ada_w': nrm(ks[3], (DEPTH, D, 6 * D), D ** -0.5),
        'ada_b': nrm(ks[4], (DEPTH, 6 * D), 0.02),
        'norm_mix_g': 1.0 + nrm(ks[5], (DEPTH, D), 0.02),
        'norm_ffn_g': 1.0 + nrm(ks[6], (DEPTH, D), 0.02),
        'ev_w_in': nrm(ks[7], (N_EVEN, D, EVEN_IN), D ** -0.5),
        'ev_kv_norm_g': 1.0 + nrm(ks[8], (N_EVEN, KV_RANK), 0.02),
        'ev_w_uk': nrm(ks[9], (N_EVEN, B_HEADS, HEAD_DIM, KV_RANK), KV_RANK ** -0.5),
        'ev_w_uv': nrm(ks[10], (N_EVEN, B_HEADS, KV_RANK, HEAD_DIM), KV_RANK ** -0.5),
        'ev_sinks': nrm(ks[11], (N_EVEN, A_HEADS), 0.5),
        'ev_w_out': nrm(ks[12], (N_EVEN, MIX_WIDTH_EVEN, D), MIX_WIDTH_EVEN ** -0.5),
        'od_w_in': nrm(ks[13], (N_ODD, D, 3 * MIX_WIDTH_ODD), D ** -0.5),
        'od_w_out': nrm(ks[14], (N_ODD, MIX_WIDTH_ODD, D), MIX_WIDTH_ODD ** -0.5),
        'ffn_w_up': nrm(ks[15], (DEPTH, D, 2 * D_FF), D ** -0.5),
        'ffn_conv_w': nrm(ks[16], (DEPTH, CONV_W, 2 * D_FF), CONV_W ** -0.5),
        'ffn_conv_b': nrm(ks[17], (DEPTH, 2 * D_FF), 0.02),
        'ffn_w_down': nrm(ks[18], (DEPTH, D_FF, D), D_FF ** -0.5),
        'final_g': 1.0 + nrm(ks[19], (D,), 0.02),
    }


def reference(x, c, rel_bias, ada_w, ada_b, norm_mix_g, norm_ffn_g, ev_w_in, ev_kv_norm_g,
              ev_w_uk, ev_w_uv, ev_sinks, ev_w_out, od_w_in, od_w_out, ffn_w_up, ffn_conv_w,
              ffn_conv_b, ffn_w_down, final_g):
    c_act = jax.nn.silu(c)
    for i in range(DEPTH):
        mod = c_act @ ada_w[i] + ada_b[i]
        sh1, sc1, g1, sh2, sc2, g2 = [m[:, None, :] for m in jnp.split(mod, 6, axis=-1)]
        h = rms_norm(x, norm_mix_g[i]) * (1.0 + sc1) + sh1
        if i % 2 == 0:
            j = i // 2
            y = even_mixer(h, ev_w_in[j], ev_kv_norm_g[j], ev_w_uk[j], ev_w_uv[j],
                           ev_sinks[j], ev_w_out[j], rel_bias)
        else:
            j = i // 2
            y = odd_mixer(h, od_w_in[j], od_w_out[j])
        x = x + g1 * y
        h = rms_norm(x, norm_ffn_g[i]) * (1.0 + sc2) + sh2
        x = x + g2 * conv_ffn(h, ffn_w_up[i], ffn_conv_w[i], ffn_conv_b[i], ffn_w_down[i])
    return rms_norm(x, final_g)
```

```python
import functools
import math

import jax
import jax.numpy as jnp
import numpy as np
from jax import lax
from jax.experimental import pallas as pl
from jax.experimental.pallas import tpu as pltpu

F32 = jnp.float32
BF16 = jnp.bfloat16
I32 = jnp.int32

CHUNK = 64
QB = 128
HEAD_DIM = 64
A_HEADS = 16
A_KV_HEADS = 4
A_GROUP = A_HEADS // A_KV_HEADS
WINDOW_CHUNKS = 2
B_HEADS = 16
KV_RANK = 256
IDX_HEADS = 8
IDX_DIM = 64
TOPK_MAX = 256
C_HEADS = 16
C_HEAD_DIM = 128
NUM_BUCKETS = 32
MAX_DISTANCE = 128
CONV_W = 3
EPS = 1e-6

LANES = 128
SUBLANES = 8
MXU_DIM = 256
VMEM_LIMIT = 56 * 1024 * 1024

NEG_BIG = -1e30
INT_MIN = -(2 ** 31)
SB_DONE = -100.0

EV_QA = 0
EV_QB = EV_QA + A_HEADS * HEAD_DIM
EV_QI = EV_QB + B_HEADS * HEAD_DIM
EV_KA = EV_QI + IDX_HEADS * LANES
EV_VA = EV_KA + A_KV_HEADS * HEAD_DIM
EV_CL = EV_VA + A_KV_HEADS * HEAD_DIM
EV_KW = EV_CL + KV_RANK
EV_WIDTH = 4096


def _cparams(sem):
    return pltpu.CompilerParams(dimension_semantics=sem, vmem_limit_bytes=VMEM_LIMIT)


def _norm_mod(x, g, sc, sh):
    ms = jnp.mean(x * x, axis=-1, keepdims=True)
    y = x * lax.rsqrt(ms + EPS)
    return (y * g) * (1.0 + sc) + sh


def _silu(x):
    return x / (1.0 + jnp.exp(-x))


def _ada_kernel(c_ref, w_ref, b_ref, o_ref):
    a = _silu(c_ref[...]).astype(BF16)
    o_ref[0] = jnp.dot(a, w_ref[0].astype(BF16), preferred_element_type=F32) + b_ref[0]


def _ada_mod(c, ada_w, ada_b):
    depth, d, n = ada_w.shape
    bn = c.shape[0]
    rows = -(-bn // SUBLANES) * SUBLANES
    c_pad = jnp.pad(c, ((0, rows - bn), (0, 0)))
    tn = 1024
    out = pl.pallas_call(
        _ada_kernel,
        grid=(depth, n // tn),
        in_specs=[
            pl.BlockSpec((rows, d), lambda l, j: (0, 0)),
            pl.BlockSpec((1, d, tn), lambda l, j: (l, 0, j)),
            pl.BlockSpec((1, 1, tn), lambda l, j: (l, 0, j)),
        ],
        out_specs=pl.BlockSpec((1, rows, tn), lambda l, j: (l, 0, j)),
        out_shape=jax.ShapeDtypeStruct((depth, rows, n), F32),
        compiler_params=_cparams(("arbitrary", "arbitrary")),
        name="ada_mod",
    )(c_pad, ada_w, ada_b.reshape(depth, 1, n))
    return out[:, :bn]


def _in_proj_kernel(x_ref, g_ref, sc_ref, sh_ref, w_ref, kvg_ref, o_ref, h_scr, *, kv_tile, kv_off):
    j = pl.program_id(1)

    @pl.when(j == 0)
    def _():
        h_scr[...] = _norm_mod(x_ref[...], g_ref[...], sc_ref[0], sh_ref[0]).astype(BF16)

    acc = jnp.dot(h_scr[...], w_ref[...], preferred_element_type=F32)
    if kv_tile is None:
        o_ref[...] = acc.astype(o_ref.dtype)
    else:
        @pl.when(j != kv_tile)
        def _():
            o_ref[...] = acc.astype(o_ref.dtype)

        @pl.when(j == kv_tile)
        def _():
            lat = acc[:, kv_off:kv_off + KV_RANK]
            ms = jnp.mean(lat * lat, axis=-1, keepdims=True)
            lat = lat * lax.rsqrt(ms + EPS) * kvg_ref[...]
            o_ref[...] = acc.astype(o_ref.dtype)
            o_ref[:, kv_off:kv_off + KV_RANK] = lat.astype(o_ref.dtype)


def _in_proj(x2, g, sc, sh, w, kv_g, seq, *, tm, tn, kv_col=None):
    n_rows, d = x2.shape
    n_out = w.shape[1]
    tiles_per_batch = seq // tm
    if kv_col is None:
        kv_tile, kv_off = None, 0
    else:
        kv_tile, kv_off = kv_col // tn, kv_col % tn
    kern = functools.partial(_in_proj_kernel, kv_tile=kv_tile, kv_off=kv_off)
    return pl.pallas_call(
        kern,
        grid=(n_rows // tm, n_out // tn),
        in_specs=[
            pl.BlockSpec((tm, d), lambda i, j: (i, 0)),
            pl.BlockSpec((1, d), lambda i, j: (0, 0)),
            pl.BlockSpec((1, 1, d), lambda i, j: (i // tiles_per_batch, 0, 0)),
            pl.BlockSpec((1, 1, d), lambda i, j: (i // tiles_per_batch, 0, 0)),
            pl.BlockSpec((d, tn), lambda i, j: (0, j)),
            pl.BlockSpec((1, KV_RANK), lambda i, j: (0, 0)),
        ],
        out_specs=pl.BlockSpec((tm, tn), lambda i, j: (i, j)),
        out_shape=jax.ShapeDtypeStruct((n_rows, n_out), BF16),
        scratch_shapes=[pltpu.VMEM((tm, d), BF16)],
        compiler_params=_cparams(("arbitrary", "arbitrary")),
        name="in_proj",
    )(x2, g, sc, sh, w, kv_g)


def _out_proj_kernel(*refs, n_act):
    acts = refs[:n_act]
    ws = refs[n_act:2 * n_act]
    x_ref, gate_ref, o_ref = refs[2 * n_act:]
    y = jnp.dot(acts[0][...], ws[0][...], preferred_element_type=F32)
    for a_ref, w_ref in zip(acts[1:], ws[1:]):
        y = y + jnp.dot(a_ref[...], w_ref[...], preferred_element_type=F32)
    o_ref[...] = x_ref[...] + gate_ref[0] * y


def _out_proj(acts, ws, x2, gate, seq, *, tm):
    n_rows, d = x2.shape
    tiles_per_batch = seq // tm
    n_act = len(acts)
    in_specs = ([pl.BlockSpec((tm, a.shape[1]), lambda i: (i, 0)) for a in acts]
                + [pl.BlockSpec(w.shape, lambda i: (0, 0)) for w in ws]
                + [pl.BlockSpec((tm, d), lambda i: (i, 0)),
                   pl.BlockSpec((1, 1, d), lambda i: (i // tiles_per_batch, 0, 0))])
    return pl.pallas_call(
        functools.partial(_out_proj_kernel, n_act=n_act),
        grid=(n_rows // tm,),
        in_specs=in_specs,
        out_specs=pl.BlockSpec((tm, d), lambda i: (i, 0)),
        out_shape=jax.ShapeDtypeStruct((n_rows, d), F32),
        compiler_params=_cparams(("arbitrary",)),
        name="out_proj",
    )(*acts, *ws, x2, gate)


def _ffn_kernel(x_ref, g_ref, sc_ref, sh_ref, gate_ref, wg_ref, wv_ref, cg_ref, cv_ref, wd_ref,
                fg_ref, o_ref, h_scr, ug_scr, uv_scr, carry_g, carry_v, *, tiles_per_batch, final):
    i = pl.program_id(0)
    f = pl.program_id(1)
    nf = pl.num_programs(1)
    tm = x_ref.shape[0]

    @pl.when(f == 0)
    def _():
        h_scr[...] = _norm_mod(x_ref[...], g_ref[...], sc_ref[0], sh_ref[0]).astype(BF16)

    batch_start = (i % tiles_per_batch) == 0

    def conv_branch(w_ref, c_ref, u_scr, carry):
        u = jnp.dot(h_scr[...], w_ref[...], preferred_element_type=F32)
        @pl.when(batch_start)
        def _():
            u_scr[0:SUBLANES, :] = jnp.zeros((SUBLANES, u_scr.shape[1]), F32)

        @pl.when(jnp.logical_not(batch_start))
        def _():
            u_scr[0:SUBLANES, :] = carry[f]

        u_scr[SUBLANES:, :] = u
        carry[f] = u[tm - SUBLANES:, :]
        cw = c_ref[...]
        y = cw[2:3] * u + cw[3:4]
        y = y + cw[1:2] * u_scr[SUBLANES - 1:SUBLANES - 1 + tm, :]
        y = y + cw[0:1] * u_scr[SUBLANES - 2:SUBLANES - 2 + tm, :]
        return y

    yg = conv_branch(wg_ref, cg_ref, ug_scr, carry_g)
    yv = conv_branch(wv_ref, cv_ref, uv_scr, carry_v)
    a = (_silu(yg) * yv).astype(BF16)
    contrib = jnp.dot(a, wd_ref[...], preferred_element_type=F32)

    @pl.when(f == 0)
    def _():
        o_ref[...] = contrib

    @pl.when(f != 0)
    def _():
        o_ref[...] += contrib

    @pl.when(f == nf - 1)
    def _():
        xo = x_ref[...] + gate_ref[0] * o_ref[...]
        if final:
            ms = jnp.mean(xo * xo, axis=-1, keepdims=True)
            xo = xo * lax.rsqrt(ms + EPS) * fg_ref[...]
        o_ref[...] = xo


def _conv_ffn(x2, g, sc, sh, gate, w_up, conv4, w_down, final_g, seq, *, tm, tf, final):
    n_rows, d = x2.shape
    d_ff = w_down.shape[0]
    nf = d_ff // tf
    tiles_per_batch = seq // tm
    kern = functools.partial(_ffn_kernel, tiles_per_batch=tiles_per_batch, final=final)
    return pl.pallas_call(
        kern,
        grid=(n_rows // tm, nf),
        in_specs=[
            pl.BlockSpec((tm, d), lambda i, f: (i, 0)),
            pl.BlockSpec((1, d), lambda i, f: (0, 0)),
            pl.BlockSpec((1, 1, d), lambda i, f: (i // tiles_per_batch, 0, 0)),
            pl.BlockSpec((1, 1, d), lambda i, f: (i // tiles_per_batch, 0, 0)),
            pl.BlockSpec((1, 1, d), lambda i, f: (i // tiles_per_batch, 0, 0)),
            pl.BlockSpec((d, tf), lambda i, f: (0, f)),
            pl.BlockSpec((d, tf), lambda i, f: (0, f + nf)),
            pl.BlockSpec((CONV_W + 1, tf), lambda i, f: (0, f)),
            pl.BlockSpec((CONV_W + 1, tf), lambda i, f: (0, f + nf)),
            pl.BlockSpec((tf, d), lambda i, f: (f, 0)),
            pl.BlockSpec((1, d), lambda i, f: (0, 0)),
        ],
        out_specs=pl.BlockSpec((tm, d), lambda i, f: (i, 0)),
        out_shape=jax.ShapeDtypeStruct((n_rows, d), F32),
        scratch_shapes=[
            pltpu.VMEM((tm, d), BF16),
            pltpu.VMEM((tm + SUBLANES, tf), F32),
            pltpu.VMEM((tm + SUBLANES, tf), F32),
            pltpu.VMEM((nf, SUBLANES, tf), F32),
            pltpu.VMEM((nf, SUBLANES, tf), F32),
        ],
        compiler_params=_cparams(("arbitrary", "arbitrary")),
        name="conv_ffn",
    )(x2, g, sc, sh, gate, w_up, w_up, conv4, conv4, w_down, final_g)


def _swa_kernel(q_ref, kp_ref, ko_ref, vp_ref, vo_ref, bias_ref, sink_ref, o_ref):
    n = pl.program_id(1)
    row = lax.broadcasted_iota(I32, (QB, 2 * QB), 0)
    col = lax.broadcasted_iota(I32, (QB, 2 * QB), 1)
    qc = row // CHUNK
    kc = col // CHUNK - QB // CHUNK
    allowed = (kc <= qc) & (kc >= qc - WINDOW_CHUNKS) & ((col >= QB) | (n > 0))
    allowed_g = jnp.concatenate([allowed] * A_GROUP, axis=0)
    k_all = jnp.concatenate([kp_ref[0], ko_ref[0]], axis=0)
    v_all = jnp.concatenate([vp_ref[0], vo_ref[0]], axis=0)
    q_all = q_ref[0] * (HEAD_DIM ** -0.5)
    outs = []
    for g in range(A_KV_HEADS):
        heads = range(g * A_GROUP, (g + 1) * A_GROUP)
        qg = jnp.concatenate([q_all[:, h * HEAD_DIM:(h + 1) * HEAD_DIM] for h in heads], axis=0)
        kg = k_all[:, g * HEAD_DIM:(g + 1) * HEAD_DIM]
        vg = v_all[:, g * HEAD_DIM:(g + 1) * HEAD_DIM]
        s = lax.dot_general(qg, kg, (((1,), (1,)), ((), ())), preferred_element_type=F32)
        bias = jnp.concatenate([bias_ref[h] for h in heads], axis=0)
        s = jnp.where(allowed_g, s + bias, NEG_BIG)
        sink = jnp.concatenate([sink_ref[h] for h in heads], axis=0)[:, 0:1]
        m = jnp.maximum(jnp.max(s, axis=-1, keepdims=True), sink)
        p = jnp.exp(s - m)
        denom = jnp.sum(p, axis=-1, keepdims=True) + jnp.exp(sink - m)
        o = jnp.dot(p.astype(BF16), vg, preferred_element_type=F32) / denom
        outs.extend(o[i * QB:(i + 1) * QB] for i in range(A_GROUP))
    o_ref[0] = jnp.concatenate(outs, axis=-1).astype(o_ref.dtype)


def _swa(proj3, bias_a, sinks):
    bn, seq, _ = proj3.shape
    nblk = seq // QB
    kvw = A_KV_HEADS * HEAD_DIM
    qw = A_HEADS * HEAD_DIM
    sink_b = jnp.broadcast_to(sinks.astype(F32)[:, None, None], (A_HEADS, QB, LANES))
    prev = lambda c: (lambda b, n: (b, jnp.maximum(n - 1, 0), c))
    own = lambda c: (lambda b, n: (b, n, c))
    return pl.pallas_call(
        _swa_kernel,
        grid=(bn, nblk),
        in_specs=[
            pl.BlockSpec((1, QB, qw), own(EV_QA // qw)),
            pl.BlockSpec((1, QB, kvw), prev(EV_KA // kvw)),
            pl.BlockSpec((1, QB, kvw), own(EV_KA // kvw)),
            pl.BlockSpec((1, QB, kvw), prev(EV_VA // kvw)),
            pl.BlockSpec((1, QB, kvw), own(EV_VA // kvw)),
            pl.BlockSpec((A_HEADS, QB, 2 * QB), lambda b, n: (0, 0, 0)),
            pl.BlockSpec((A_HEADS, QB, LANES), lambda b, n: (0, 0, 0)),
        ],
        out_specs=pl.BlockSpec((1, QB, qw), lambda b, n: (b, n, 0)),
        out_shape=jax.ShapeDtypeStruct((bn, seq, qw), BF16),
        compiler_params=_cparams(("arbitrary", "arbitrary")),
        name="swa",
    )(proj3, proj3, proj3, proj3, proj3, bias_a, sink_b)


KCH = 2 * QB


def _dsa_kernel(qb_ref, qi_ref, kq_ref, ckv_ref, kk_ref, wuk_ref, wuv_ref, bias_ref, o_ref,
                qst_scr, wb_scr, key_scr, qlat_scr, m_scr, l_scr, acc_scr, cut_scr, *, topk):
    n = pl.program_id(1)
    nch = n // 2 + 1
    spad = key_scr.shape[1]

    def chunk_start(e):
        return pl.multiple_of((n - 2 * e) * QB, QB)

    idx_scale = IDX_DIM ** -0.5 * IDX_HEADS ** -0.5
    for h in range(IDX_HEADS):
        qst_scr[h * QB:(h + 1) * QB, :] = qi_ref[0, :, h * LANES:(h + 1) * LANES]
        w = kq_ref[0, :, IDX_DIM + h:IDX_DIM + h + 1].astype(F32) * idx_scale
        wb_scr[h] = jnp.broadcast_to(w, (QB, KCH))
    q_all = qb_ref[0] * (HEAD_DIM ** -0.5)
    for h in range(B_HEADS):
        ql = jnp.dot(q_all[:, h * HEAD_DIM:(h + 1) * HEAD_DIM], wuk_ref[h],
                     preferred_element_type=F32)
        qlat_scr[h * QB:(h + 1) * QB, :] = ql.astype(BF16)

    row = lax.broadcasted_iota(I32, (QB, KCH), 0)
    lane = lax.broadcasted_iota(I32, (QB, KCH), 1)
    limit = (n * QB + CHUNK) + (row // CHUNK) * CHUNK

    def idx_body(e, _):
        ks = chunk_start(e)
        kj = kk_ref[0, pl.ds(ks, KCH), :]
        r = lax.dot_general(qst_scr[...], kj, (((1,), (1,)), ((), ())), preferred_element_type=F32)
        sc = jnp.maximum(r[0:QB], 0.0) * wb_scr[0]
        for h in range(1, IDX_HEADS):
            sc = sc + jnp.maximum(r[h * QB:(h + 1) * QB], 0.0) * wb_scr[h]
        bits = lax.bitcast_convert_type(sc, I32)
        key = bits ^ ((bits >> 31) & 0x7FFFFFFF)
        key = jnp.where(bits == INT_MIN, 0, key)
        kpos = ks - QB + lane
        key = jnp.where((kpos >= 0) & (kpos < limit), key, INT_MIN)
        key_scr[:, pl.ds(ks, KCH)] = key
        return 0

    lax.fori_loop(0, nch, idx_body, 0)

    def count(pred):
        def body(e, acc):
            k = key_scr[:, pl.ds(chunk_start(e), KCH)]
            return acc + jnp.where(pred(k, e), 1, 0)
        c = lax.fori_loop(0, nch, body, jnp.zeros((QB, KCH), I32))
        return jnp.sum(c, axis=1, keepdims=True)

    def bit_body(b, t):
        cand = t ^ jnp.left_shift(jnp.int32(1), 31 - b)
        tot = count(lambda k, e: k >= cand)
        return jnp.where(tot >= topk, cand, t)

    thr = lax.fori_loop(0, 32, bit_body, jnp.full((QB, 1), INT_MIN, I32))
    thr = jnp.maximum(thr, INT_MIN + 1)

    n_gt = count(lambda k, e: k > thr)
    n_eq = count(lambda k, e: k == thr)
    need = topk - n_gt
    cut_scr[...] = jnp.full((QB, LANES), 2 * spad, I32)
    excess = jnp.max(jnp.where(n_eq > need, 1, 0))

    @pl.when(excess > 0)
    def _():
        nbits = int(spad).bit_length()

        def cut_body(b, c):
            cand = c | jnp.left_shift(jnp.int32(1), nbits - 1 - b)
            tot = count(lambda k, e: (k == thr) & (chunk_start(e) + lane < cand))
            return jnp.where(tot <= need, cand, c)

        c = lax.fori_loop(0, nbits, cut_body, jnp.zeros((QB, 1), I32))
        cut_scr[...] = jnp.broadcast_to(jnp.where(n_eq > need, c, 2 * spad), (QB, LANES))

    cut = cut_scr[:, 0:1]

    m_scr[...] = jnp.full(m_scr.shape, NEG_BIG, F32)
    l_scr[...] = jnp.zeros(l_scr.shape, F32)
    acc_scr[...] = jnp.zeros(acc_scr.shape, F32)

    def attend(e, near):
        ks = chunk_start(e)
        kv = ckv_ref[0, pl.ds(ks, KCH), :]
        k = key_scr[:, pl.ds(ks, KCH)]
        sel = (k > thr) | ((k == thr) & (ks + lane < cut))
        addmask = jnp.where(sel, 0.0, NEG_BIG)
        for h in range(B_HEADS):
            s = lax.dot_general(qlat_scr[h * QB:(h + 1) * QB, :], kv, (((1,), (1,)), ((), ())),
                                preferred_element_type=F32)
            s = s + addmask
            if near:
                s = s + bias_ref[h]
            m_prev = m_scr[h][:, 0:1]
            l_prev = l_scr[h][:, 0:1]
            m_new = jnp.maximum(m_prev, jnp.max(s, axis=-1, keepdims=True))
            alpha = jnp.exp(m_prev - m_new)
            p = jnp.exp(s - m_new)
            l_new = alpha * l_prev + jnp.sum(p, axis=-1, keepdims=True)
            acc_scr[h] = alpha * acc_scr[h] + jnp.dot(p.astype(BF16), kv, preferred_element_type=F32)
            m_scr[h] = jnp.broadcast_to(m_new, (QB, LANES))
            l_scr[h] = jnp.broadcast_to(l_new, (QB, LANES))

    def far_body(e, _):
        attend(e, False)
        return 0

    lax.fori_loop(1, nch, far_body, 0)
    attend(0, True)

    outs = []
    for h in range(B_HEADS):
        o_lat = (acc_scr[h] / l_scr[h][:, 0:1]).astype(BF16)
        outs.append(jnp.dot(o_lat, wuv_ref[h], preferred_element_type=F32))
    o_ref[0] = jnp.concatenate(outs, axis=-1).astype(o_ref.dtype)


def _dsa(proj3, ckv_pad, kk_pad, w_uk, w_uv, bias_b):
    bn, seq, _ = proj3.shape
    nblk = seq // QB
    spad = seq + QB
    qw = B_HEADS * HEAD_DIM
    iw = IDX_HEADS * LANES
    return pl.pallas_call(
        functools.partial(_dsa_kernel, topk=min(TOPK_MAX, seq // 4)),
        grid=(bn, nblk),
        in_specs=[
            pl.BlockSpec((1, QB, qw), lambda b, n: (b, n, EV_QB // qw)),
            pl.BlockSpec((1, QB, iw), lambda b, n: (b, n, EV_QI // iw)),
            pl.BlockSpec((1, QB, LANES), lambda b, n: (b, n, EV_KW // LANES)),
            pl.BlockSpec((1, spad, KV_RANK), lambda b, n: (b, 0, 0)),
            pl.BlockSpec((1, spad, LANES), lambda b, n: (b, 0, 0)),
            pl.BlockSpec(w_uk.shape, lambda b, n: (0, 0, 0)),
            pl.BlockSpec(w_uv.shape, lambda b, n: (0, 0, 0)),
            pl.BlockSpec((B_HEADS, QB, KCH), lambda b, n: (0, 0, 0)),
        ],
        out_specs=pl.BlockSpec((1, QB, qw), lambda b, n: (b, n, 0)),
        out_shape=jax.ShapeDtypeStruct((bn, seq, qw), BF16),
        scratch_shapes=[
            pltpu.VMEM((IDX_HEADS * QB, LANES), BF16),
            pltpu.VMEM((IDX_HEADS, QB, KCH), F32),
            pltpu.VMEM((QB, spad), I32),
            pltpu.VMEM((B_HEADS * QB, KV_RANK), BF16),
            pltpu.VMEM((B_HEADS, QB, LANES), F32),
            pltpu.VMEM((B_HEADS, QB, LANES), F32),
            pltpu.VMEM((B_HEADS, QB, KV_RANK), F32),
            pltpu.VMEM((QB, LANES), I32),
        ],
        compiler_params=_cparams(("arbitrary", "arbitrary")),
        name="dsa",
    )(proj3, proj3, proj3, ckv_pad, kk_pad, w_uk, w_uv, bias_b)


SB_HG = 4


def _sb_kernel(q_ref, k_ref, v_ref, o_ref):
    n = pl.program_id(2)
    scale = C_HEAD_DIM ** -0.5
    row = lax.broadcasted_iota(I32, (QB, QB), 0)
    col = lax.broadcasted_iota(I32, (QB, QB), 1)
    suffix = jnp.where(row > col, 1.0, 0.0).astype(BF16)
    qs = [q_ref[0, :, h * C_HEAD_DIM:(h + 1) * C_HEAD_DIM] for h in range(SB_HG)]

    def cond(state):
        j, done = state[0], state[1]
        return (j >= 0) & (done == 0)

    def body(state):
        j = state[0]
        ks = pl.multiple_of(j * QB, QB)
        earlier = (ks + col) < (n * QB + row)
        new = []
        worst = jnp.float32(-jnp.inf)
        for h in range(SB_HG):
            carry, acc = state[2 + 2 * h], state[3 + 2 * h]
            kj = k_ref[0, pl.ds(ks, QB), h * C_HEAD_DIM:(h + 1) * C_HEAD_DIM]
            vj = v_ref[0, pl.ds(ks, QB), h * C_HEAD_DIM:(h + 1) * C_HEAD_DIM]
            z = lax.dot_general(qs[h], kj, (((1,), (1,)), ((), ())),
                                preferred_element_type=F32) * scale
            sp = jnp.maximum(z, 0.0) + jnp.log1p(jnp.exp(-jnp.abs(z)))
            lk = jnp.where(earlier, -sp, 0.0)
            hi = lk.astype(BF16)
            lo = (lk - hi.astype(F32)).astype(BF16)
            inner = (jnp.dot(hi, suffix, preferred_element_type=F32)
                     + jnp.dot(lo, suffix, preferred_element_type=F32))
            a = jnp.where(earlier, jnp.exp((z - sp) + (carry + inner)), 0.0)
            acc = acc + jnp.dot(a.astype(BF16), vj, preferred_element_type=F32)
            carry = carry + jnp.sum(lk, axis=-1, keepdims=True)
            worst = jnp.maximum(worst, jnp.max(carry))
            new.extend([carry, acc])
        done = (worst < SB_DONE).astype(I32)
        return (j - 1, done, *new)

    init = [n, jnp.int32(0)]
    for _ in range(SB_HG):
        init.extend([jnp.zeros((QB, 1), F32), jnp.zeros((QB, C_HEAD_DIM), F32)])
    final = lax.while_loop(cond, body, tuple(init))
    o_ref[0] = jnp.concatenate([final[3 + 2 * h] for h in range(SB_HG)], axis=-1).astype(o_ref.dtype)


def _sb(qkv3):
    bn, seq, _ = qkv3.shape
    nblk = seq // QB
    gw = SB_HG * C_HEAD_DIM
    ngrp = C_HEADS // SB_HG
    return pl.pallas_call(
        _sb_kernel,
        grid=(bn, ngrp, nblk),
        in_specs=[
            pl.BlockSpec((1, QB, gw), lambda b, g, n: (b, n, g)),
            pl.BlockSpec((1, seq, gw), lambda b, g, n: (b, 0, ngrp + g)),
            pl.BlockSpec((1, seq, gw), lambda b, g, n: (b, 0, 2 * ngrp + g)),
        ],
        out_specs=pl.BlockSpec((1, QB, gw), lambda b, g, n: (b, n, g)),
        out_shape=jax.ShapeDtypeStruct((bn, seq, C_HEADS * C_HEAD_DIM), BF16),
        compiler_params=_cparams(("arbitrary", "arbitrary", "arbitrary")),
        name="stick_breaking",
    )(qkv3, qkv3, qkv3)


def _t5_bucket(rel):
    half = NUM_BUCKETS // 2
    max_exact = half // 2
    n = jnp.abs(rel)
    nf = jnp.maximum(n, 1).astype(F32)
    large = max_exact + (jnp.log(nf / max_exact) / math.log(MAX_DISTANCE / max_exact)
                         * (half - max_exact)).astype(I32)
    large = jnp.minimum(large, half - 1)
    return jnp.where(rel > 0, half, 0) + jnp.where(n < max_exact, n, large)


def _band_bias(rel_bias):
    rel = (jnp.arange(2 * QB) - QB)[None, :] - jnp.arange(QB)[:, None]
    band = rel_bias.astype(F32)[_t5_bucket(rel)]
    far = rel_bias.astype(F32)[_t5_bucket(jnp.int32(-(QB + 1)))]
    return jnp.transpose(band, (2, 0, 1)), far


def _even_w_in(w_in):
    d = w_in.shape[0]
    sizes = (A_HEADS * HEAD_DIM, A_KV_HEADS * HEAD_DIM, A_KV_HEADS * HEAD_DIM,
             B_HEADS * HEAD_DIM, KV_RANK, IDX_HEADS * IDX_DIM, IDX_DIM, IDX_HEADS)
    offs = np.cumsum((0,) + sizes)
    qa, ka, va, qb, cl, qi, ki, wi = [w_in[:, offs[i]:offs[i + 1]] for i in range(8)]
    qi = jnp.pad(qi.reshape(d, IDX_HEADS, IDX_DIM), ((0, 0), (0, 0), (0, LANES - IDX_DIM)))
    kw = jnp.pad(jnp.concatenate([ki, wi], axis=1), ((0, 0), (0, LANES - IDX_DIM - IDX_HEADS)))
    w = jnp.concatenate([qa, qb, qi.reshape(d, IDX_HEADS * LANES), ka, va, cl, kw], axis=1)
    return jnp.pad(w, ((0, 0), (0, EV_WIDTH - w.shape[1]))).astype(BF16)


def kernel(x, c, rel_bias, ada_w, ada_b, norm_mix_g, norm_ffn_g, ev_w_in, ev_kv_norm_g, ev_w_uk,
           ev_w_uv, ev_sinks, ev_w_out, od_w_in, od_w_out, ffn_w_up, ffn_conv_w, ffn_conv_b,
           ffn_w_down, final_g):
    bn, seq, d = x.shape
    depth = ada_w.shape[0]
    x2 = x.reshape(bn * seq, d)

    mod = _ada_mod(c, ada_w, ada_b)
    band, far = _band_bias(rel_bias)
    bias_a = band[:A_HEADS]
    bias_b = band[A_HEADS:] - far[A_HEADS:, None, None]

    for i in range(depth):
        sh1, sc1, g1, sh2, sc2, g2 = [m.reshape(bn, 1, d) for m in jnp.split(mod[i], 6, axis=-1)]
        g_mix = norm_mix_g[i].reshape(1, d)
        j = i // 2
        if i % 2 == 0:
            proj = _in_proj(x2, g_mix, sc1, sh1, _even_w_in(ev_w_in[j]),
                            ev_kv_norm_g[j].reshape(1, KV_RANK), seq, tm=1024, tn=512, kv_col=EV_CL)
            proj3 = proj.reshape(bn, seq, EV_WIDTH)
            lead = ((0, 0), (QB, 0), (0, 0))
            ckv_pad = jnp.pad(proj3[:, :, EV_CL:EV_CL + KV_RANK], lead)
            kk_pad = jnp.pad(proj3[:, :, EV_KW:EV_KW + LANES], lead)
            o_a = _swa(proj3, bias_a, ev_sinks[j])
            o_b = _dsa(proj3, ckv_pad, kk_pad, ev_w_uk[j].astype(BF16), ev_w_uv[j].astype(BF16), bias_b)
            w_out = ev_w_out[j].astype(BF16)
            wa = w_out[:A_HEADS * HEAD_DIM]
            wb = w_out[A_HEADS * HEAD_DIM:]
            x2 = _out_proj([o_a.reshape(bn * seq, -1), o_b.reshape(bn * seq, -1)], [wa, wb],
                           x2, g1, seq, tm=512)
        else:
            qkv = _in_proj(x2, g_mix, sc1, sh1, od_w_in[j].astype(BF16),
                           jnp.ones((1, KV_RANK), F32), seq, tm=1024, tn=1024)
            o_c = _sb(qkv.reshape(bn, seq, -1))
            x2 = _out_proj([o_c.reshape(bn * seq, -1)], [od_w_out[j].astype(BF16)], x2, g1, seq, tm=512)
        conv4 = jnp.concatenate([ffn_conv_w[i], ffn_conv_b[i][None]], axis=0)
        x2 = _conv_ffn(x2, norm_ffn_g[i].reshape(1, d), sc2, sh2, g2, ffn_w_up[i].astype(BF16), conv4,
                       ffn_w_down[i].astype(BF16), final_g.reshape(1, d), seq,
                       tm=512, tf=512, final=(i == depth - 1))
    return x2.reshape(bn, seq, d)
```

```python
import functools
import math

import jax
import jax.numpy as jnp
import numpy as np
from jax import lax
from jax.experimental import pallas as pl
from jax.experimental.pallas import tpu as pltpu

F32 = jnp.float32
BF16 = jnp.bfloat16
I32 = jnp.int32

CHUNK = 64
QB = 128
HEAD_DIM = 64
A_HEADS = 16
A_KV_HEADS = 4
A_GROUP = A_HEADS // A_KV_HEADS
WINDOW_CHUNKS = 2
B_HEADS = 16
KV_RANK = 256
IDX_HEADS = 8
IDX_DIM = 64
TOPK_MAX = 256
C_HEADS = 16
C_HEAD_DIM = 128
NUM_BUCKETS = 32
MAX_DISTANCE = 128
CONV_W = 3
EPS = 1e-6

LANES = 128
SUBLANES = 8
MXU_DIM = 256
VMEM_LIMIT = 56 * 1024 * 1024

NEG_BIG = -1e30
LOG2E = math.log2(math.e)
INT_MIN = -(2 ** 31)
SB_DONE = -88.0

EV_QA = 0
EV_QB = EV_QA + A_HEADS * HEAD_DIM
EV_QI = EV_QB + B_HEADS * HEAD_DIM
EV_KA = EV_QI + IDX_HEADS * LANES
EV_VA = EV_KA + A_KV_HEADS * HEAD_DIM
EV_CL = EV_VA + A_KV_HEADS * HEAD_DIM
EV_KW = EV_CL + KV_RANK
EV_WIDTH = 4096


def _cparams(sem):
    return pltpu.CompilerParams(dimension_semantics=sem, vmem_limit_bytes=VMEM_LIMIT)


def _norm_mod(x, g, sc, sh):
    ms = jnp.mean(x * x, axis=-1, keepdims=True)
    y = x * lax.rsqrt(ms + EPS)
    return (y * g) * (1.0 + sc) + sh


def _silu(x):
    return x / (1.0 + jnp.exp(-x))


def _ada_kernel(c_ref, w_ref, b_ref, o_ref):
    a = _silu(c_ref[...]).astype(BF16)
    o_ref[0] = jnp.dot(a, w_ref[0].astype(BF16), preferred_element_type=F32) + b_ref[0]


def _ada_mod(c, ada_w, ada_b):
    depth, d, n = ada_w.shape
    bn = c.shape[0]
    rows = -(-bn // SUBLANES) * SUBLANES
    c_pad = jnp.pad(c, ((0, rows - bn), (0, 0)))
    tn = 1024
    out = pl.pallas_call(
        _ada_kernel,
        grid=(depth, n // tn),
        in_specs=[
            pl.BlockSpec((rows, d), lambda l, j: (0, 0)),
            pl.BlockSpec((1, d, tn), lambda l, j: (l, 0, j)),
            pl.BlockSpec((1, 1, tn), lambda l, j: (l, 0, j)),
        ],
        out_specs=pl.BlockSpec((1, rows, tn), lambda l, j: (l, 0, j)),
        out_shape=jax.ShapeDtypeStruct((depth, rows, n), F32),
        compiler_params=_cparams(("arbitrary", "arbitrary")),
        name="ada_mod",
    )(c_pad, ada_w, ada_b.reshape(depth, 1, n))
    return out[:, :bn]


def _in_proj_kernel(x_ref, g_ref, sc_ref, sh_ref, w_ref, kvg_ref, o_ref, h_scr, *, kv_tile, kv_off):
    j = pl.program_id(1)

    @pl.when(j == 0)
    def _():
        h_scr[...] = _norm_mod(x_ref[...], g_ref[...], sc_ref[0], sh_ref[0]).astype(BF16)

    acc = jnp.dot(h_scr[...], w_ref[...], preferred_element_type=F32)
    if kv_tile is None:
        o_ref[...] = acc.astype(o_ref.dtype)
    else:
        @pl.when(j != kv_tile)
        def _():
            o_ref[...] = acc.astype(o_ref.dtype)

        @pl.when(j == kv_tile)
        def _():
            lat = acc[:, kv_off:kv_off + KV_RANK]
            ms = jnp.mean(lat * lat, axis=-1, keepdims=True)
            lat = lat * lax.rsqrt(ms + EPS) * kvg_ref[...]
            o_ref[...] = acc.astype(o_ref.dtype)
            o_ref[:, kv_off:kv_off + KV_RANK] = lat.astype(o_ref.dtype)


def _in_proj(x2, g, sc, sh, w, kv_g, seq, *, tm, tn, kv_col=None):
    n_rows, d = x2.shape
    n_out = w.shape[1]
    tiles_per_batch = seq // tm
    if kv_col is None:
        kv_tile, kv_off = None, 0
    else:
        kv_tile, kv_off = kv_col // tn, kv_col % tn
    kern = functools.partial(_in_proj_kernel, kv_tile=kv_tile, kv_off=kv_off)
    return pl.pallas_call(
        kern,
        grid=(n_rows // tm, n_out // tn),
        in_specs=[
            pl.BlockSpec((tm, d), lambda i, j: (i, 0)),
            pl.BlockSpec((1, d), lambda i, j: (0, 0)),
            pl.BlockSpec((1, 1, d), lambda i, j: (i // tiles_per_batch, 0, 0)),
            pl.BlockSpec((1, 1, d), lambda i, j: (i // tiles_per_batch, 0, 0)),
            pl.BlockSpec((d, tn), lambda i, j: (0, j)),
            pl.BlockSpec((1, KV_RANK), lambda i, j: (0, 0)),
        ],
        out_specs=pl.BlockSpec((tm, tn), lambda i, j: (i, j)),
        out_shape=jax.ShapeDtypeStruct((n_rows, n_out), BF16),
        scratch_shapes=[pltpu.VMEM((tm, d), BF16)],
        compiler_params=_cparams(("arbitrary", "arbitrary")),
        name="in_proj",
    )(x2, g, sc, sh, w, kv_g)


def _out_proj_kernel(*refs, n_act):
    acts = refs[:n_act]
    ws = refs[n_act:2 * n_act]
    x_ref, gate_ref, o_ref = refs[2 * n_act:]
    y = jnp.dot(acts[0][...], ws[0][...], preferred_element_type=F32)
    for a_ref, w_ref in zip(acts[1:], ws[1:]):
        y = y + jnp.dot(a_ref[...], w_ref[...], preferred_element_type=F32)
    o_ref[...] = x_ref[...] + gate_ref[0] * y


def _out_proj(acts, ws, x2, gate, seq, *, tm):
    n_rows, d = x2.shape
    tiles_per_batch = seq // tm
    n_act = len(acts)
    in_specs = ([pl.BlockSpec((tm, a.shape[1]), lambda i: (i, 0)) for a in acts]
                + [pl.BlockSpec(w.shape, lambda i: (0, 0)) for w in ws]
                + [pl.BlockSpec((tm, d), lambda i: (i, 0)),
                   pl.BlockSpec((1, 1, d), lambda i: (i // tiles_per_batch, 0, 0))])
    return pl.pallas_call(
        functools.partial(_out_proj_kernel, n_act=n_act),
        grid=(n_rows // tm,),
        in_specs=in_specs,
        out_specs=pl.BlockSpec((tm, d), lambda i: (i, 0)),
        out_shape=jax.ShapeDtypeStruct((n_rows, d), F32),
        compiler_params=_cparams(("arbitrary",)),
        name="out_proj",
    )(*acts, *ws, x2, gate)


def _ffn_kernel(x_ref, g_ref, sc_ref, sh_ref, gate_ref, wg_ref, wv_ref, cg_ref, cv_ref, wd_ref,
                fg_ref, o_ref, h_scr, ug_scr, uv_scr, carry_g, carry_v, *, tiles_per_batch, final):
    i = pl.program_id(0)
    f = pl.program_id(1)
    nf = pl.num_programs(1)
    tm = x_ref.shape[0]

    @pl.when(f == 0)
    def _():
        h_scr[...] = _norm_mod(x_ref[...], g_ref[...], sc_ref[0], sh_ref[0]).astype(BF16)

    batch_start = (i % tiles_per_batch) == 0

    def conv_branch(w_ref, c_ref, u_scr, carry):
        u = jnp.dot(h_scr[...], w_ref[...], preferred_element_type=F32)
        @pl.when(batch_start)
        def _():
            u_scr[0:SUBLANES, :] = jnp.zeros((SUBLANES, u_scr.shape[1]), F32)

        @pl.when(jnp.logical_not(batch_start))
        def _():
            u_scr[0:SUBLANES, :] = carry[f]

        u_scr[SUBLANES:, :] = u
        carry[f] = u[tm - SUBLANES:, :]
        cw = c_ref[...]
        y = cw[2:3] * u + cw[3:4]
        y = y + cw[1:2] * u_scr[SUBLANES - 1:SUBLANES - 1 + tm, :]
        y = y + cw[0:1] * u_scr[SUBLANES - 2:SUBLANES - 2 + tm, :]
        return y

    yg = conv_branch(wg_ref, cg_ref, ug_scr, carry_g)
    yv = conv_branch(wv_ref, cv_ref, uv_scr, carry_v)
    a = (_silu(yg) * yv).astype(BF16)
    contrib = jnp.dot(a, wd_ref[...], preferred_element_type=F32)

    @pl.when(f == 0)
    def _():
        o_ref[...] = contrib

    @pl.when(f != 0)
    def _():
        o_ref[...] += contrib

    @pl.when(f == nf - 1)
    def _():
        xo = x_ref[...] + gate_ref[0] * o_ref[...]
        if final:
            ms = jnp.mean(xo * xo, axis=-1, keepdims=True)
            xo = xo * lax.rsqrt(ms + EPS) * fg_ref[...]
        o_ref[...] = xo


def _conv_ffn(x2, g, sc, sh, gate, w_up, conv4, w_down, final_g, seq, *, tm, tf, final):
    n_rows, d = x2.shape
    d_ff = w_down.shape[0]
    nf = d_ff // tf
    tiles_per_batch = seq // tm
    kern = functools.partial(_ffn_kernel, tiles_per_batch=tiles_per_batch, final=final)
    return pl.pallas_call(
        kern,
        grid=(n_rows // tm, nf),
        in_specs=[
            pl.BlockSpec((tm, d), lambda i, f: (i, 0)),
            pl.BlockSpec((1, d), lambda i, f: (0, 0)),
            pl.BlockSpec((1, 1, d), lambda i, f: (i // tiles_per_batch, 0, 0)),
            pl.BlockSpec((1, 1, d), lambda i, f: (i // tiles_per_batch, 0, 0)),
            pl.BlockSpec((1, 1, d), lambda i, f: (i // tiles_per_batch, 0, 0)),
            pl.BlockSpec((d, tf), lambda i, f: (0, f)),
            pl.BlockSpec((d, tf), lambda i, f: (0, f + nf)),
            pl.BlockSpec((CONV_W + 1, tf), lambda i, f: (0, f)),
            pl.BlockSpec((CONV_W + 1, tf), lambda i, f: (0, f + nf)),
            pl.BlockSpec((tf, d), lambda i, f: (f, 0)),
            pl.BlockSpec((1, d), lambda i, f: (0, 0)),
        ],
        out_specs=pl.BlockSpec((tm, d), lambda i, f: (i, 0)),
        out_shape=jax.ShapeDtypeStruct((n_rows, d), F32),
        scratch_shapes=[
            pltpu.VMEM((tm, d), BF16),
            pltpu.VMEM((tm + SUBLANES, tf), F32),
            pltpu.VMEM((tm + SUBLANES, tf), F32),
            pltpu.VMEM((nf, SUBLANES, tf), F32),
            pltpu.VMEM((nf, SUBLANES, tf), F32),
        ],
        compiler_params=_cparams(("arbitrary", "arbitrary")),
        name="conv_ffn",
    )(x2, g, sc, sh, gate, w_up, w_up, conv4, conv4, w_down, final_g)


def _swa_kernel(q_ref, kp_ref, ko_ref, vp_ref, vo_ref, bias_ref, sink_ref, o_ref):
    n = pl.program_id(1)
    row = lax.broadcasted_iota(I32, (QB, 2 * QB), 0)
    col = lax.broadcasted_iota(I32, (QB, 2 * QB), 1)
    qc = row // CHUNK
    kc = col // CHUNK - QB // CHUNK
    allowed = (kc <= qc) & (kc >= qc - WINDOW_CHUNKS) & ((col >= QB) | (n > 0))
    allowed_g = jnp.concatenate([allowed] * A_GROUP, axis=0)
    k_all = jnp.concatenate([kp_ref[0], ko_ref[0]], axis=0)
    v_all = jnp.concatenate([vp_ref[0], vo_ref[0]], axis=0)
    q_all = q_ref[0] * (HEAD_DIM ** -0.5)
    outs = []
    for g in range(A_KV_HEADS):
        heads = range(g * A_GROUP, (g + 1) * A_GROUP)
        qg = jnp.concatenate([q_all[:, h * HEAD_DIM:(h + 1) * HEAD_DIM] for h in heads], axis=0)
        kg = k_all[:, g * HEAD_DIM:(g + 1) * HEAD_DIM]
        vg = v_all[:, g * HEAD_DIM:(g + 1) * HEAD_DIM]
        s = lax.dot_general(qg, kg, (((1,), (1,)), ((), ())), preferred_element_type=F32)
        bias = jnp.concatenate([bias_ref[h] for h in heads], axis=0)
        s = jnp.where(allowed_g, s + bias, NEG_BIG)
        sink = jnp.concatenate([sink_ref[h] for h in heads], axis=0)[:, 0:1]
        m = jnp.maximum(jnp.max(s, axis=-1, keepdims=True), sink)
        p = jnp.exp(s - m)
        denom = jnp.sum(p, axis=-1, keepdims=True) + jnp.exp(sink - m)
        o = jnp.dot(p.astype(BF16), vg, preferred_element_type=F32) / denom
        outs.extend(o[i * QB:(i + 1) * QB] for i in range(A_GROUP))
    o_ref[0] = jnp.concatenate(outs, axis=-1).astype(o_ref.dtype)


def _swa(proj3, bias_a, sinks):
    bn, seq, _ = proj3.shape
    nblk = seq // QB
    kvw = A_KV_HEADS * HEAD_DIM
    qw = A_HEADS * HEAD_DIM
    sink_b = jnp.broadcast_to(sinks.astype(F32)[:, None, None], (A_HEADS, QB, LANES))
    prev = lambda c: (lambda b, n: (b, jnp.maximum(n - 1, 0), c))
    own = lambda c: (lambda b, n: (b, n, c))
    return pl.pallas_call(
        _swa_kernel,
        grid=(bn, nblk),
        in_specs=[
            pl.BlockSpec((1, QB, qw), own(EV_QA // qw)),
            pl.BlockSpec((1, QB, kvw), prev(EV_KA // kvw)),
            pl.BlockSpec((1, QB, kvw), own(EV_KA // kvw)),
            pl.BlockSpec((1, QB, kvw), prev(EV_VA // kvw)),
            pl.BlockSpec((1, QB, kvw), own(EV_VA // kvw)),
            pl.BlockSpec((A_HEADS, QB, 2 * QB), lambda b, n: (0, 0, 0)),
            pl.BlockSpec((A_HEADS, QB, LANES), lambda b, n: (0, 0, 0)),
        ],
        out_specs=pl.BlockSpec((1, QB, qw), lambda b, n: (b, n, 0)),
        out_shape=jax.ShapeDtypeStruct((bn, seq, qw), BF16),
        compiler_params=_cparams(("arbitrary", "arbitrary")),
        name="swa",
    )(proj3, proj3, proj3, proj3, proj3, bias_a, sink_b)


KCH = 2 * QB


def _dsa_kernel(qb_ref, qi_ref, kq_ref, ckv_ref, kk_ref, wuk_ref, wuv_ref, bias_ref, o_ref,
                qst_scr, wb_scr, key_scr, qlat_scr, m_scr, l_scr, acc_scr, cut_scr, *, topk):
    n = pl.program_id(1)
    nch = n // 2 + 1
    spad = key_scr.shape[1]

    def chunk_start(e):
        return pl.multiple_of((n - 2 * e) * QB, QB)

    idx_scale = IDX_DIM ** -0.5 * IDX_HEADS ** -0.5
    for h in range(IDX_HEADS):
        qst_scr[h * QB:(h + 1) * QB, :] = qi_ref[0, :, h * LANES:(h + 1) * LANES]
        w = kq_ref[0, :, IDX_DIM + h:IDX_DIM + h + 1].astype(F32) * idx_scale
        wb_scr[h] = jnp.broadcast_to(w, (QB, KCH))
    q_all = qb_ref[0]
    for h in range(B_HEADS):
        ql = jnp.dot(q_all[:, h * HEAD_DIM:(h + 1) * HEAD_DIM], wuk_ref[h],
                     preferred_element_type=F32)
        qlat_scr[h * QB:(h + 1) * QB, :] = (ql * (HEAD_DIM ** -0.5 * LOG2E)).astype(BF16)

    row = lax.broadcasted_iota(I32, (QB, KCH), 0)
    lane = lax.broadcasted_iota(I32, (QB, KCH), 1)
    limit = (n * QB + CHUNK) + (row // CHUNK) * CHUNK

    def idx_body(e, _):
        ks = chunk_start(e)
        kj = kk_ref[0, pl.ds(ks, KCH), :]
        r = lax.dot_general(qst_scr[...], kj, (((1,), (1,)), ((), ())), preferred_element_type=F32)
        sc = jnp.maximum(r[0:QB], 0.0) * wb_scr[0]
        for h in range(1, IDX_HEADS):
            sc = sc + jnp.maximum(r[h * QB:(h + 1) * QB], 0.0) * wb_scr[h]
        bits = lax.bitcast_convert_type(sc, I32)
        key = bits ^ ((bits >> 31) & 0x7FFFFFFF)
        key = jnp.where(bits == INT_MIN, 0, key)
        kpos = ks - QB + lane
        key = jnp.where((kpos >= 0) & (kpos < limit), key, INT_MIN)
        key_scr[:, pl.ds(ks, KCH)] = key
        return 0

    lax.fori_loop(0, nch, idx_body, 0)

    def count(pred):
        def body(e, acc):
            k = key_scr[:, pl.ds(chunk_start(e), KCH)]
            return acc + jnp.where(pred(k, e), 1, 0)
        c = lax.fori_loop(0, nch, body, jnp.zeros((QB, KCH), I32))
        return jnp.sum(c, axis=1, keepdims=True)

    def bit_body(b, t):
        cand = t ^ jnp.left_shift(jnp.int32(1), 31 - b)
        tot = count(lambda k, e: k >= cand)
        return jnp.where(tot >= topk, cand, t)

    thr = lax.fori_loop(0, 32, bit_body, jnp.full((QB, 1), INT_MIN, I32))
    thr = jnp.maximum(thr, INT_MIN + 1)

    n_gt = count(lambda k, e: k > thr)
    n_eq = count(lambda k, e: k == thr)
    need = topk - n_gt
    cut_scr[...] = jnp.full((QB, LANES), 2 * spad, I32)
    excess = jnp.max(jnp.where(n_eq > need, 1, 0))

    @pl.when(excess > 0)
    def _():
        nbits = int(spad).bit_length()

        def cut_body(b, c):
            cand = c | jnp.left_shift(jnp.int32(1), nbits - 1 - b)
            tot = count(lambda k, e: (k == thr) & (chunk_start(e) + lane < cand))
            return jnp.where(tot <= need, cand, c)

        c = lax.fori_loop(0, nbits, cut_body, jnp.zeros((QB, 1), I32))
        cut_scr[...] = jnp.broadcast_to(jnp.where(n_eq > need, c, 2 * spad), (QB, LANES))

    cut = cut_scr[:, 0:1]

    def masked_scores(e, near):
        ks = chunk_start(e)
        kv = ckv_ref[0, pl.ds(ks, KCH), :]
        k = key_scr[:, pl.ds(ks, KCH)]
        sel = (k > thr) | ((k == thr) & (ks + lane < cut))
        addmask = jnp.where(sel, 0.0, NEG_BIG)
        s = lax.dot_general(qlat_scr[...], kv, (((1,), (1,)), ((), ())), preferred_element_type=F32)
        s = s.reshape(B_HEADS, QB, KCH) + addmask[None]
        if near:
            s = s + bias_ref[...]
        return s, kv

    def max_pass(e, near):
        s, _ = masked_scores(e, near)
        m_scr[...] = jnp.maximum(m_scr[...], jnp.maximum(s[:, :, :LANES], s[:, :, LANES:]))

    def sum_pass(e, near):
        s, kv = masked_scores(e, near)
        m = m_scr[...]
        p = jnp.concatenate([jnp.exp2(s[:, :, :LANES] - m), jnp.exp2(s[:, :, LANES:] - m)], axis=-1)
        p = p.astype(BF16).reshape(B_HEADS * QB, KCH)
        acc_scr[...] += jnp.dot(p, kv, preferred_element_type=F32)
        l_scr[...] += jnp.dot(p, jnp.ones((KCH, LANES), BF16), preferred_element_type=F32)

    def sweep(fn):
        def far_body(e, _):
            fn(e, False)
            return 0
        lax.fori_loop(1, nch, far_body, 0)
        fn(0, True)

    m_scr[...] = jnp.full(m_scr.shape, NEG_BIG, F32)
    sweep(max_pass)
    m_scr[...] = jnp.broadcast_to(jnp.max(m_scr[...], axis=-1, keepdims=True), m_scr.shape)
    l_scr[...] = jnp.zeros(l_scr.shape, F32)
    acc_scr[...] = jnp.zeros(acc_scr.shape, F32)
    sweep(sum_pass)

    outs = []
    for h in range(B_HEADS):
        rows = slice(h * QB, (h + 1) * QB)
        l = l_scr[rows, :]
        o_lat = (acc_scr[rows, :] / jnp.concatenate([l, l], axis=-1)).astype(BF16)
        outs.append(jnp.dot(o_lat, wuv_ref[h], preferred_element_type=F32))
    o_ref[0] = jnp.concatenate(outs, axis=-1).astype(o_ref.dtype)


def _dsa(proj3, ckv_pad, kk_pad, w_uk, w_uv, bias_b):
    bn, seq, _ = proj3.shape
    nblk = seq // QB
    spad = seq + QB
    qw = B_HEADS * HEAD_DIM
    iw = IDX_HEADS * LANES
    return pl.pallas_call(
        functools.partial(_dsa_kernel, topk=min(TOPK_MAX, seq // 4)),
        grid=(bn, nblk),
        in_specs=[
            pl.BlockSpec((1, QB, qw), lambda b, n: (b, n, EV_QB // qw)),
            pl.BlockSpec((1, QB, iw), lambda b, n: (b, n, EV_QI // iw)),
            pl.BlockSpec((1, QB, LANES), lambda b, n: (b, n, EV_KW // LANES)),
            pl.BlockSpec((1, spad, KV_RANK), lambda b, n: (b, 0, 0)),
            pl.BlockSpec((1, spad, LANES), lambda b, n: (b, 0, 0)),
            pl.BlockSpec(w_uk.shape, lambda b, n: (0, 0, 0)),
            pl.BlockSpec(w_uv.shape, lambda b, n: (0, 0, 0)),
            pl.BlockSpec((B_HEADS, QB, KCH), lambda b, n: (0, 0, 0)),
        ],
        out_specs=pl.BlockSpec((1, QB, qw), lambda b, n: (b, n, 0)),
        out_shape=jax.ShapeDtypeStruct((bn, seq, qw), BF16),
        scratch_shapes=[
            pltpu.VMEM((IDX_HEADS * QB, LANES), BF16),
            pltpu.VMEM((IDX_HEADS, QB, KCH), F32),
            pltpu.VMEM((QB, spad), I32),
            pltpu.VMEM((B_HEADS * QB, KV_RANK), BF16),
            pltpu.VMEM((B_HEADS, QB, LANES), F32),
            pltpu.VMEM((B_HEADS * QB, LANES), F32),
            pltpu.VMEM((B_HEADS * QB, KV_RANK), F32),
            pltpu.VMEM((QB, LANES), I32),
        ],
        compiler_params=_cparams(("arbitrary", "arbitrary")),
        name="dsa",
    )(proj3, proj3, proj3, ckv_pad, kk_pad, w_uk, w_uv, bias_b)


SB_HG = 8


def _sb_kernel(q_ref, k_ref, v_ref, o_ref, acc_scr):
    n = pl.program_id(2)
    scale = C_HEAD_DIM ** -0.5
    rows = SB_HG * QB
    row = lax.broadcasted_iota(I32, (QB, QB), 0)
    col = lax.broadcasted_iota(I32, (QB, QB), 1)
    suffix = jnp.where(row > col, 1.0, 0.0).astype(BF16)
    suffix2 = jnp.concatenate([suffix, suffix], axis=0)
    earlier = lax.broadcasted_iota(I32, (rows, QB), 1) < (lax.broadcasted_iota(I32, (rows, QB), 0) % QB)

    def head_cols(h):
        return slice(h * C_HEAD_DIM, (h + 1) * C_HEAD_DIM)

    def block(j, carry, diagonal):
        ks = pl.multiple_of(j * QB, QB)
        z = jnp.concatenate(
            [lax.dot_general(q_ref[0, :, head_cols(h)], k_ref[0, pl.ds(ks, QB), head_cols(h)],
                             (((1,), (1,)), ((), ())), preferred_element_type=F32)
             for h in range(SB_HG)], axis=0) * scale
        sp = jnp.maximum(z, 0.0) + jnp.log(1.0 + jnp.exp(-jnp.abs(z)))
        lk = jnp.where(earlier, -sp, 0.0) if diagonal else -sp
        hi = lk.astype(BF16)
        lo = (lk - hi.astype(F32)).astype(BF16)
        inner = jnp.dot(jnp.concatenate([hi, lo], axis=1), suffix2, preferred_element_type=F32)
        a = jnp.exp((z - sp) + (carry + inner))
        if diagonal:
            a = jnp.where(earlier, a, 0.0)
        a = a.astype(BF16)
        for h in range(SB_HG):
            r = slice(h * QB, (h + 1) * QB)
            acc_scr[r, :] += jnp.dot(a[r], v_ref[0, pl.ds(ks, QB), head_cols(h)],
                                     preferred_element_type=F32)
        return carry + jnp.sum(lk, axis=-1, keepdims=True)

    def all_done(carry):
        return (jnp.max(carry) < SB_DONE).astype(I32)

    acc_scr[...] = jnp.zeros(acc_scr.shape, F32)
    carry0 = block(n, jnp.zeros((rows, 1), F32), True)

    def cond(state):
        j, done, _ = state
        return (j >= 0) & (done == 0)

    def body(state):
        j, _, carry = state
        carry = block(j, carry, False)
        return j - 1, all_done(carry), carry

    lax.while_loop(cond, body, (n - 1, all_done(carry0), carry0))
    o_ref[0] = jnp.concatenate([acc_scr[h * QB:(h + 1) * QB, :] for h in range(SB_HG)],
                               axis=-1).astype(o_ref.dtype)


def _sb(qkv3):
    bn, seq, _ = qkv3.shape
    nblk = seq // QB
    gw = SB_HG * C_HEAD_DIM
    ngrp = C_HEADS // SB_HG
    return pl.pallas_call(
        _sb_kernel,
        grid=(bn, ngrp, nblk),
        in_specs=[
            pl.BlockSpec((1, QB, gw), lambda b, g, n: (b, n, g)),
            pl.BlockSpec((1, seq, gw), lambda b, g, n: (b, 0, ngrp + g)),
            pl.BlockSpec((1, seq, gw), lambda b, g, n: (b, 0, 2 * ngrp + g)),
        ],
        out_specs=pl.BlockSpec((1, QB, gw), lambda b, g, n: (b, n, g)),
        out_shape=jax.ShapeDtypeStruct((bn, seq, C_HEADS * C_HEAD_DIM), BF16),
        scratch_shapes=[pltpu.VMEM((SB_HG * QB, C_HEAD_DIM), F32)],
        compiler_params=_cparams(("arbitrary", "arbitrary", "arbitrary")),
        name="stick_breaking",
    )(qkv3, qkv3, qkv3)


def _t5_bucket(rel):
    half = NUM_BUCKETS // 2
    max_exact = half // 2
    n = jnp.abs(rel)
    nf = jnp.maximum(n, 1).astype(F32)
    large = max_exact + (jnp.log(nf / max_exact) / math.log(MAX_DISTANCE / max_exact)
                         * (half - max_exact)).astype(I32)
    large = jnp.minimum(large, half - 1)
    return jnp.where(rel > 0, half, 0) + jnp.where(n < max_exact, n, large)


def _band_bias(rel_bias):
    rel = (jnp.arange(2 * QB) - QB)[None, :] - jnp.arange(QB)[:, None]
    band = rel_bias.astype(F32)[_t5_bucket(rel)]
    far = rel_bias.astype(F32)[_t5_bucket(jnp.int32(-(QB + 1)))]
    return jnp.transpose(band, (2, 0, 1)), far


def _even_w_in(w_in):
    d = w_in.shape[0]
    sizes = (A_HEADS * HEAD_DIM, A_KV_HEADS * HEAD_DIM, A_KV_HEADS * HEAD_DIM,
             B_HEADS * HEAD_DIM, KV_RANK, IDX_HEADS * IDX_DIM, IDX_DIM, IDX_HEADS)
    offs = np.cumsum((0,) + sizes)
    qa, ka, va, qb, cl, qi, ki, wi = [w_in[:, offs[i]:offs[i + 1]] for i in range(8)]
    qi = jnp.pad(qi.reshape(d, IDX_HEADS, IDX_DIM), ((0, 0), (0, 0), (0, LANES - IDX_DIM)))
    kw = jnp.pad(jnp.concatenate([ki, wi], axis=1), ((0, 0), (0, LANES - IDX_DIM - IDX_HEADS)))
    w = jnp.concatenate([qa, qb, qi.reshape(d, IDX_HEADS * LANES), ka, va, cl, kw], axis=1)
    return jnp.pad(w, ((0, 0), (0, EV_WIDTH - w.shape[1]))).astype(BF16)


def kernel(x, c, rel_bias, ada_w, ada_b, norm_mix_g, norm_ffn_g, ev_w_in, ev_kv_norm_g, ev_w_uk,
           ev_w_uv, ev_sinks, ev_w_out, od_w_in, od_w_out, ffn_w_up, ffn_conv_w, ffn_conv_b,
           ffn_w_down, final_g):
    bn, seq, d = x.shape
    depth = ada_w.shape[0]
    x2 = x.reshape(bn * seq, d)

    mod = _ada_mod(c, ada_w, ada_b)
    band, far = _band_bias(rel_bias)
    bias_a = band[:A_HEADS]
    bias_b = (band[A_HEADS:] - far[A_HEADS:, None, None]) * LOG2E

    for i in range(depth):
        sh1, sc1, g1, sh2, sc2, g2 = [m.reshape(bn, 1, d) for m in jnp.split(mod[i], 6, axis=-1)]
        g_mix = norm_mix_g[i].reshape(1, d)
        j = i // 2
        if i % 2 == 0:
            proj = _in_proj(x2, g_mix, sc1, sh1, _even_w_in(ev_w_in[j]),
                            ev_kv_norm_g[j].reshape(1, KV_RANK), seq, tm=1024, tn=512, kv_col=EV_CL)
            proj3 = proj.reshape(bn, seq, EV_WIDTH)
            lead = ((0, 0), (QB, 0), (0, 0))
            ckv_pad = jnp.pad(proj3[:, :, EV_CL:EV_CL + KV_RANK], lead)
            kk_pad = jnp.pad(proj3[:, :, EV_KW:EV_KW + LANES], lead)
            o_a = _swa(proj3, bias_a, ev_sinks[j])
            o_b = _dsa(proj3, ckv_pad, kk_pad, ev_w_uk[j].astype(BF16), ev_w_uv[j].astype(BF16), bias_b)
            w_out = ev_w_out[j].astype(BF16)
            wa = w_out[:A_HEADS * HEAD_DIM]
            wb = w_out[A_HEADS * HEAD_DIM:]
            x2 = _out_proj([o_a.reshape(bn * seq, -1), o_b.reshape(bn * seq, -1)], [wa, wb],
                           x2, g1, seq, tm=512)
        else:
            qkv = _in_proj(x2, g_mix, sc1, sh1, od_w_in[j].astype(BF16),
                           jnp.ones((1, KV_RANK), F32), seq, tm=1024, tn=1024)
            o_c = _sb(qkv.reshape(bn, seq, -1))
            x2 = _out_proj([o_c.reshape(bn * seq, -1)], [od_w_out[j].astype(BF16)], x2, g1, seq, tm=512)
        conv4 = jnp.concatenate([ffn_conv_w[i], ffn_conv_b[i][None]], axis=0)
        x2 = _conv_ffn(x2, norm_ffn_g[i].reshape(1, d), sc2, sh2, g2, ffn_w_up[i].astype(BF16), conv4,
                       ffn_w_down[i].astype(BF16), final_g.reshape(1, d), seq,
                       tm=512, tf=512, final=(i == depth - 1))
    return x2.reshape(bn, seq, d)
```

```python
import functools
import math

import jax
import jax.numpy as jnp
import numpy as np
from jax import lax
from jax.experimental import pallas as pl
from jax.experimental.pallas import tpu as pltpu

F32 = jnp.float32
BF16 = jnp.bfloat16
I32 = jnp.int32

CHUNK = 64
QB = 128
HEAD_DIM = 64
A_HEADS = 16
A_KV_HEADS = 4
A_GROUP = A_HEADS // A_KV_HEADS
WINDOW_CHUNKS = 2
B_HEADS = 16
KV_RANK = 256
IDX_HEADS = 8
IDX_DIM = 64
TOPK_MAX = 256
C_HEADS = 16
C_HEAD_DIM = 128
NUM_BUCKETS = 32
MAX_DISTANCE = 128
CONV_W = 3
EPS = 1e-6

LANES = 128
SUBLANES = 8
MXU_DIM = 256
VMEM_LIMIT = 56 * 1024 * 1024

NEG_BIG = -1e30
LOG2E = math.log2(math.e)
INT_MIN = -(2 ** 31)
SB_DONE = -88.0

EV_QA = 0
EV_QB = EV_QA + A_HEADS * HEAD_DIM
EV_QI = EV_QB + B_HEADS * HEAD_DIM
EV_KA = EV_QI + IDX_HEADS * LANES
EV_VA = EV_KA + A_KV_HEADS * HEAD_DIM
EV_CL = EV_VA + A_KV_HEADS * HEAD_DIM
EV_KW = EV_CL + KV_RANK
EV_WIDTH = 4096


def _cparams(sem):
    return pltpu.CompilerParams(dimension_semantics=sem, vmem_limit_bytes=VMEM_LIMIT)


def _norm_mod(x, g, sc, sh):
    ms = jnp.mean(x * x, axis=-1, keepdims=True)
    y = x * lax.rsqrt(ms + EPS)
    return (y * g) * (1.0 + sc) + sh


def _silu(x):
    return x / (1.0 + jnp.exp(-x))


def _ada_kernel(c_ref, w_ref, b_ref, o_ref):
    a = _silu(c_ref[...]).astype(BF16)
    o_ref[0] = jnp.dot(a, w_ref[0].astype(BF16), preferred_element_type=F32) + b_ref[0]


def _ada_mod(c, ada_w, ada_b):
    depth, d, n = ada_w.shape
    bn = c.shape[0]
    rows = -(-bn // SUBLANES) * SUBLANES
    c_pad = jnp.pad(c, ((0, rows - bn), (0, 0)))
    tn = 1024
    out = pl.pallas_call(
        _ada_kernel,
        grid=(depth, n // tn),
        in_specs=[
            pl.BlockSpec((rows, d), lambda l, j: (0, 0)),
            pl.BlockSpec((1, d, tn), lambda l, j: (l, 0, j)),
            pl.BlockSpec((1, 1, tn), lambda l, j: (l, 0, j)),
        ],
        out_specs=pl.BlockSpec((1, rows, tn), lambda l, j: (l, 0, j)),
        out_shape=jax.ShapeDtypeStruct((depth, rows, n), F32),
        compiler_params=_cparams(("arbitrary", "arbitrary")),
        name="ada_mod",
    )(c_pad, ada_w, ada_b.reshape(depth, 1, n))
    return out[:, :bn]


def _in_proj_kernel(x_ref, g_ref, sc_ref, sh_ref, w_ref, kvg_ref, o_ref, h_scr, *, kv_tile, kv_off):
    j = pl.program_id(1)

    @pl.when(j == 0)
    def _():
        h_scr[...] = _norm_mod(x_ref[...], g_ref[...], sc_ref[0], sh_ref[0]).astype(BF16)

    def project():
        return jnp.dot(h_scr[...], w_ref[...], preferred_element_type=F32)

    if kv_tile is None:
        o_ref[...] = project().astype(o_ref.dtype)
    else:
        @pl.when(j != kv_tile)
        def _():
            o_ref[...] = project().astype(o_ref.dtype)

        @pl.when(j == kv_tile)
        def _():
            acc = project()
            lat = acc[:, kv_off:kv_off + KV_RANK]
            ms = jnp.mean(lat * lat, axis=-1, keepdims=True)
            lat = lat * lax.rsqrt(ms + EPS) * kvg_ref[...]
            o_ref[...] = acc.astype(o_ref.dtype)
            o_ref[:, kv_off:kv_off + KV_RANK] = lat.astype(o_ref.dtype)


def _in_proj(x2, g, sc, sh, w, kv_g, seq, *, tm, tn, kv_col=None):
    n_rows, d = x2.shape
    n_out = w.shape[1]
    tiles_per_batch = seq // tm
    if kv_col is None:
        kv_tile, kv_off = None, 0
    else:
        kv_tile, kv_off = kv_col // tn, kv_col % tn
    kern = functools.partial(_in_proj_kernel, kv_tile=kv_tile, kv_off=kv_off)
    return pl.pallas_call(
        kern,
        grid=(n_rows // tm, n_out // tn),
        in_specs=[
            pl.BlockSpec((tm, d), lambda i, j: (i, 0)),
            pl.BlockSpec((1, d), lambda i, j: (0, 0)),
            pl.BlockSpec((1, 1, d), lambda i, j: (i // tiles_per_batch, 0, 0)),
            pl.BlockSpec((1, 1, d), lambda i, j: (i // tiles_per_batch, 0, 0)),
            pl.BlockSpec((d, tn), lambda i, j: (0, j)),
            pl.BlockSpec((1, KV_RANK), lambda i, j: (0, 0)),
        ],
        out_specs=pl.BlockSpec((tm, tn), lambda i, j: (i, j)),
        out_shape=jax.ShapeDtypeStruct((n_rows, n_out), BF16),
        scratch_shapes=[pltpu.VMEM((tm, d), BF16)],
        compiler_params=_cparams(("arbitrary", "arbitrary")),
        name="in_proj",
    )(x2, g, sc, sh, w, kv_g)


def _out_proj_kernel(*refs, n_act):
    acts = refs[:n_act]
    ws = refs[n_act:2 * n_act]
    x_ref, gate_ref, o_ref = refs[2 * n_act:]
    y = jnp.dot(acts[0][...], ws[0][...], preferred_element_type=F32)
    for a_ref, w_ref in zip(acts[1:], ws[1:]):
        y = y + jnp.dot(a_ref[...], w_ref[...], preferred_element_type=F32)
    o_ref[...] = x_ref[...] + gate_ref[0] * y


def _out_proj(acts, ws, x2, gate, seq, *, tm):
    n_rows, d = x2.shape
    tiles_per_batch = seq // tm
    n_act = len(acts)
    in_specs = ([pl.BlockSpec((tm, a.shape[1]), lambda i: (i, 0)) for a in acts]
                + [pl.BlockSpec(w.shape, lambda i: (0, 0)) for w in ws]
                + [pl.BlockSpec((tm, d), lambda i: (i, 0)),
                   pl.BlockSpec((1, 1, d), lambda i: (i // tiles_per_batch, 0, 0))])
    return pl.pallas_call(
        functools.partial(_out_proj_kernel, n_act=n_act),
        grid=(n_rows // tm,),
        in_specs=in_specs,
        out_specs=pl.BlockSpec((tm, d), lambda i: (i, 0)),
        out_shape=jax.ShapeDtypeStruct((n_rows, d), F32),
        compiler_params=_cparams(("arbitrary",)),
        name="out_proj",
    )(*acts, *ws, x2, gate)


FFN_RB = 64


def _ffn_kernel(x_ref, g_ref, sc_ref, sh_ref, gate_ref, wg_ref, wv_ref, cg_ref, cv_ref, wd_ref,
                fg_ref, o_ref, h_scr, ug_scr, uv_scr, a_scr, carry_g, carry_v, *, tiles_per_batch,
                final):
    i = pl.program_id(0)
    f = pl.program_id(1)
    nf = pl.num_programs(1)
    tm = x_ref.shape[0]

    @pl.when(f == 0)
    def _():
        h_scr[...] = _norm_mod(x_ref[...], g_ref[...], sc_ref[0], sh_ref[0]).astype(BF16)
        o_ref[...] = jnp.zeros(o_ref.shape, F32)

    batch_start = (i % tiles_per_batch) == 0

    def up_branch(w_ref, u_scr, carry):
        @pl.when(batch_start)
        def _():
            u_scr[0:SUBLANES, :] = jnp.zeros((SUBLANES, u_scr.shape[1]), F32)

        @pl.when(jnp.logical_not(batch_start))
        def _():
            u_scr[0:SUBLANES, :] = carry[f]

        u_scr[SUBLANES:, :] = jnp.dot(h_scr[...], w_ref[...], preferred_element_type=F32)
        carry[f] = u_scr[tm:tm + SUBLANES, :]

    up_branch(wg_ref, ug_scr, carry_g)
    up_branch(wv_ref, uv_scr, carry_v)

    def conv(u_scr, cw, base):
        y = cw[2:3] * u_scr[base:base + FFN_RB, :] + cw[3:4]
        y = y + cw[1:2] * u_scr[base - 1:base - 1 + FFN_RB, :]
        return y + cw[0:1] * u_scr[base - 2:base - 2 + FFN_RB, :]

    cwg = cg_ref[...]
    cwv = cv_ref[...]
    for r in range(tm // FFN_RB):
        base = SUBLANES + r * FFN_RB
        act = _silu(conv(ug_scr, cwg, base)) * conv(uv_scr, cwv, base)
        a_scr[r * FFN_RB:(r + 1) * FFN_RB, :] = act.astype(BF16)
    o_ref[...] += jnp.dot(a_scr[...], wd_ref[...], preferred_element_type=F32)

    @pl.when(f == nf - 1)
    def _():
        xo = x_ref[...] + gate_ref[0] * o_ref[...]
        if final:
            ms = jnp.mean(xo * xo, axis=-1, keepdims=True)
            xo = xo * lax.rsqrt(ms + EPS) * fg_ref[...]
        o_ref[...] = xo


def _conv_ffn(x2, g, sc, sh, gate, w_up, conv4, w_down, final_g, seq, *, tm, tf, final):
    n_rows, d = x2.shape
    d_ff = w_down.shape[0]
    nf = d_ff // tf
    tiles_per_batch = seq // tm
    kern = functools.partial(_ffn_kernel, tiles_per_batch=tiles_per_batch, final=final)
    return pl.pallas_call(
        kern,
        grid=(n_rows // tm, nf),
        in_specs=[
            pl.BlockSpec((tm, d), lambda i, f: (i, 0)),
            pl.BlockSpec((1, d), lambda i, f: (0, 0)),
            pl.BlockSpec((1, 1, d), lambda i, f: (i // tiles_per_batch, 0, 0)),
            pl.BlockSpec((1, 1, d), lambda i, f: (i // tiles_per_batch, 0, 0)),
            pl.BlockSpec((1, 1, d), lambda i, f: (i // tiles_per_batch, 0, 0)),
            pl.BlockSpec((d, tf), lambda i, f: (0, f)),
            pl.BlockSpec((d, tf), lambda i, f: (0, f + nf)),
            pl.BlockSpec((CONV_W + 1, tf), lambda i, f: (0, f)),
            pl.BlockSpec((CONV_W + 1, tf), lambda i, f: (0, f + nf)),
            pl.BlockSpec((tf, d), lambda i, f: (f, 0)),
            pl.BlockSpec((1, d), lambda i, f: (0, 0)),
        ],
        out_specs=pl.BlockSpec((tm, d), lambda i, f: (i, 0)),
        out_shape=jax.ShapeDtypeStruct((n_rows, d), F32),
        scratch_shapes=[
            pltpu.VMEM((tm, d), BF16),
            pltpu.VMEM((tm + SUBLANES, tf), F32),
            pltpu.VMEM((tm + SUBLANES, tf), F32),
            pltpu.VMEM((tm, tf), BF16),
            pltpu.VMEM((nf, SUBLANES, tf), F32),
            pltpu.VMEM((nf, SUBLANES, tf), F32),
        ],
        compiler_params=_cparams(("arbitrary", "arbitrary")),
        name="conv_ffn",
    )(x2, g, sc, sh, gate, w_up, w_up, conv4, conv4, w_down, final_g)


def _swa_kernel(q_ref, kp_ref, ko_ref, vp_ref, vo_ref, bias_ref, sink_ref, o_ref):
    n = pl.program_id(1)
    row = lax.broadcasted_iota(I32, (QB, 2 * QB), 0)
    col = lax.broadcasted_iota(I32, (QB, 2 * QB), 1)
    qc = row // CHUNK
    kc = col // CHUNK - QB // CHUNK
    allowed = (kc <= qc) & (kc >= qc - WINDOW_CHUNKS) & ((col >= QB) | (n > 0))
    allowed_g = jnp.concatenate([allowed] * A_GROUP, axis=0)
    k_all = jnp.concatenate([kp_ref[0], ko_ref[0]], axis=0)
    v_all = jnp.concatenate([vp_ref[0], vo_ref[0]], axis=0)
    q_all = q_ref[0] * (HEAD_DIM ** -0.5)
    outs = []
    for g in range(A_KV_HEADS):
        heads = range(g * A_GROUP, (g + 1) * A_GROUP)
        qg = jnp.concatenate([q_all[:, h * HEAD_DIM:(h + 1) * HEAD_DIM] for h in heads], axis=0)
        kg = k_all[:, g * HEAD_DIM:(g + 1) * HEAD_DIM]
        vg = v_all[:, g * HEAD_DIM:(g + 1) * HEAD_DIM]
        s = lax.dot_general(qg, kg, (((1,), (1,)), ((), ())), preferred_element_type=F32)
        bias = jnp.concatenate([bias_ref[h] for h in heads], axis=0)
        s = jnp.where(allowed_g, s + bias, NEG_BIG)
        sink = jnp.concatenate([sink_ref[h] for h in heads], axis=0)[:, 0:1]
        m = jnp.maximum(jnp.max(s, axis=-1, keepdims=True), sink)
        p = jnp.exp(s - m)
        denom = jnp.sum(p, axis=-1, keepdims=True) + jnp.exp(sink - m)
        o = jnp.dot(p.astype(BF16), vg, preferred_element_type=F32) / denom
        outs.extend(o[i * QB:(i + 1) * QB] for i in range(A_GROUP))
    o_ref[0] = jnp.concatenate(outs, axis=-1).astype(o_ref.dtype)


def _swa(proj3, bias_a, sinks):
    bn, seq, _ = proj3.shape
    nblk = seq // QB
    kvw = A_KV_HEADS * HEAD_DIM
    qw = A_HEADS * HEAD_DIM
    sink_b = jnp.broadcast_to(sinks.astype(F32)[:, None, None], (A_HEADS, QB, LANES))
    prev = lambda c: (lambda b, n: (b, jnp.maximum(n - 1, 0), c))
    own = lambda c: (lambda b, n: (b, n, c))
    return pl.pallas_call(
        _swa_kernel,
        grid=(bn, nblk),
        in_specs=[
            pl.BlockSpec((1, QB, qw), own(EV_QA // qw)),
            pl.BlockSpec((1, QB, kvw), prev(EV_KA // kvw)),
            pl.BlockSpec((1, QB, kvw), own(EV_KA // kvw)),
            pl.BlockSpec((1, QB, kvw), prev(EV_VA // kvw)),
            pl.BlockSpec((1, QB, kvw), own(EV_VA // kvw)),
            pl.BlockSpec((A_HEADS, QB, 2 * QB), lambda b, n: (0, 0, 0)),
            pl.BlockSpec((A_HEADS, QB, LANES), lambda b, n: (0, 0, 0)),
        ],
        out_specs=pl.BlockSpec((1, QB, qw), lambda b, n: (b, n, 0)),
        out_shape=jax.ShapeDtypeStruct((bn, seq, qw), BF16),
        compiler_params=_cparams(("arbitrary", "arbitrary")),
        name="swa",
    )(proj3, proj3, proj3, proj3, proj3, bias_a, sink_b)


KCH = 2 * QB


def _dsa_kernel(qb_ref, qi_ref, kq_ref, ckv_ref, kk_ref, wuk_ref, wuv_ref, bias_ref, o_ref,
                qst_scr, wb_scr, key_scr, qlat_scr, m_scr, l_scr, acc_scr, cut_scr, *, topk):
    n = pl.program_id(1)
    nch = n // 2 + 1
    spad = key_scr.shape[1]

    def chunk_start(e):
        return pl.multiple_of((n - 2 * e) * QB, QB)

    idx_scale = IDX_DIM ** -0.5 * IDX_HEADS ** -0.5
    for h in range(IDX_HEADS):
        qst_scr[h * QB:(h + 1) * QB, :] = qi_ref[0, :, h * LANES:(h + 1) * LANES]
        w = kq_ref[0, :, IDX_DIM + h:IDX_DIM + h + 1].astype(F32) * idx_scale
        wb_scr[h] = jnp.broadcast_to(w, (QB, KCH))
    q_all = qb_ref[0]
    for h in range(B_HEADS):
        ql = jnp.dot(q_all[:, h * HEAD_DIM:(h + 1) * HEAD_DIM], wuk_ref[h],
                     preferred_element_type=F32)
        qlat_scr[h * QB:(h + 1) * QB, :] = (ql * (HEAD_DIM ** -0.5 * LOG2E)).astype(BF16)

    row = lax.broadcasted_iota(I32, (QB, KCH), 0)
    lane = lax.broadcasted_iota(I32, (QB, KCH), 1)
    limit = (n * QB + CHUNK) + (row // CHUNK) * CHUNK

    def idx_body(e, _):
        ks = chunk_start(e)
        kj = kk_ref[0, pl.ds(ks, KCH), :]
        r = lax.dot_general(qst_scr[...], kj, (((1,), (1,)), ((), ())), preferred_element_type=F32)
        sc = jnp.maximum(r[0:QB], 0.0) * wb_scr[0]
        for h in range(1, IDX_HEADS):
            sc = sc + jnp.maximum(r[h * QB:(h + 1) * QB], 0.0) * wb_scr[h]
        bits = lax.bitcast_convert_type(sc, I32)
        key = bits ^ ((bits >> 31) & 0x7FFFFFFF)
        key = jnp.where(bits == INT_MIN, 0, key)
        kpos = ks - QB + lane
        key = jnp.where((kpos >= 0) & (kpos < limit), key, INT_MIN)
        key_scr[:, pl.ds(ks, KCH)] = key
        return 0

    lax.fori_loop(0, nch, idx_body, 0)

    def count(pred):
        def body(e, acc):
            ks = chunk_start(e)
            for half in range(KCH // LANES):
                off = pl.multiple_of(ks + half * LANES, LANES)
                k = key_scr[:, pl.ds(off, LANES)]
                acc = acc + jnp.where(pred(k, off), 1, 0)
            return acc
        c = lax.fori_loop(0, nch, body, jnp.zeros((QB, LANES), I32))
        return jnp.sum(c, axis=1, keepdims=True)

    lane1 = lax.broadcasted_iota(I32, (QB, LANES), 1)

    def bit_body(b, t):
        cand = t ^ jnp.left_shift(jnp.int32(1), 31 - b)
        cand_b = jnp.broadcast_to(cand, (QB, LANES))
        tot = count(lambda k, off: k >= cand_b)
        return jnp.where(tot >= topk, cand, t)

    thr = lax.fori_loop(0, 32, bit_body, jnp.full((QB, 1), INT_MIN, I32))
    thr = jnp.maximum(thr, INT_MIN + 1)

    thr_b = jnp.broadcast_to(thr, (QB, LANES))
    n_gt = count(lambda k, off: k > thr_b)
    n_eq = count(lambda k, off: k == thr_b)
    need = topk - n_gt
    cut_scr[...] = jnp.full((QB, LANES), 2 * spad, I32)
    excess = jnp.max(jnp.where(n_eq > need, 1, 0))

    @pl.when(excess > 0)
    def _():
        nbits = int(spad).bit_length()

        def cut_body(b, c):
            cand = c | jnp.left_shift(jnp.int32(1), nbits - 1 - b)
            cand_b = jnp.broadcast_to(cand, (QB, LANES))
            tot = count(lambda k, off: (k == thr_b) & (off + lane1 < cand_b))
            return jnp.where(tot <= need, cand, c)

        c = lax.fori_loop(0, nbits, cut_body, jnp.zeros((QB, 1), I32))
        cut_scr[...] = jnp.broadcast_to(jnp.where(n_eq > need, c, 2 * spad), (QB, LANES))

    cut = cut_scr[:, 0:1]

    def masked_scores(e, near):
        ks = chunk_start(e)
        kv = ckv_ref[0, pl.ds(ks, KCH), :]
        k = key_scr[:, pl.ds(ks, KCH)]
        sel = (k > thr) | ((k == thr) & (ks + lane < cut))
        addmask = jnp.where(sel, 0.0, NEG_BIG)
        s = lax.dot_general(qlat_scr[...], kv, (((1,), (1,)), ((), ())), preferred_element_type=F32)
        s = s.reshape(B_HEADS, QB, KCH) + addmask[None]
        if near:
            s = s + bias_ref[...]
        return s, kv

    def max_pass(e, near):
        s, _ = masked_scores(e, near)
        m_scr[...] = jnp.maximum(m_scr[...], jnp.maximum(s[:, :, :LANES], s[:, :, LANES:]))

    def sum_pass(e, near):
        s, kv = masked_scores(e, near)
        m = m_scr[...]
        p_lo = jnp.exp2(s[:, :, :LANES] - m)
        p_hi = jnp.exp2(s[:, :, LANES:] - m)
        l_scr[...] += p_lo + p_hi
        p = jnp.concatenate([p_lo, p_hi], axis=-1).astype(BF16).reshape(B_HEADS * QB, KCH)
        acc_scr[...] += jnp.dot(p, kv, preferred_element_type=F32)

    def sweep(fn):
        def far_body(e, _):
            fn(e, False)
            return 0
        lax.fori_loop(1, nch, far_body, 0)
        fn(0, True)

    m_scr[...] = jnp.full(m_scr.shape, NEG_BIG, F32)
    sweep(max_pass)
    m_scr[...] = jnp.broadcast_to(jnp.max(m_scr[...], axis=-1, keepdims=True), m_scr.shape)
    l_scr[...] = jnp.zeros(l_scr.shape, F32)
    acc_scr[...] = jnp.zeros(acc_scr.shape, F32)
    sweep(sum_pass)

    outs = []
    for h in range(B_HEADS):
        l = jnp.sum(l_scr[h], axis=-1, keepdims=True)
        o_lat = (acc_scr[h * QB:(h + 1) * QB, :] / l).astype(BF16)
        outs.append(jnp.dot(o_lat, wuv_ref[h], preferred_element_type=F32))
    o_ref[0] = jnp.concatenate(outs, axis=-1).astype(o_ref.dtype)


def _dsa(proj3, ckv_pad, kk_pad, w_uk, w_uv, bias_b):
    bn, seq, _ = proj3.shape
    nblk = seq // QB
    spad = seq + QB
    qw = B_HEADS * HEAD_DIM
    iw = IDX_HEADS * LANES
    return pl.pallas_call(
        functools.partial(_dsa_kernel, topk=min(TOPK_MAX, seq // 4)),
        grid=(bn, nblk),
        in_specs=[
            pl.BlockSpec((1, QB, qw), lambda b, n: (b, n, EV_QB // qw)),
            pl.BlockSpec((1, QB, iw), lambda b, n: (b, n, EV_QI // iw)),
            pl.BlockSpec((1, QB, LANES), lambda b, n: (b, n, EV_KW // LANES)),
            pl.BlockSpec((1, spad, KV_RANK), lambda b, n: (b, 0, 0)),
            pl.BlockSpec((1, spad, LANES), lambda b, n: (b, 0, 0)),
            pl.BlockSpec(w_uk.shape, lambda b, n: (0, 0, 0)),
            pl.BlockSpec(w_uv.shape, lambda b, n: (0, 0, 0)),
            pl.BlockSpec((B_HEADS, QB, KCH), lambda b, n: (0, 0, 0)),
        ],
        out_specs=pl.BlockSpec((1, QB, qw), lambda b, n: (b, n, 0)),
        out_shape=jax.ShapeDtypeStruct((bn, seq, qw), BF16),
        scratch_shapes=[
            pltpu.VMEM((IDX_HEADS * QB, LANES), BF16),
            pltpu.VMEM((IDX_HEADS, QB, KCH), F32),
            pltpu.VMEM((QB, spad), I32),
            pltpu.VMEM((B_HEADS * QB, KV_RANK), BF16),
            pltpu.VMEM((B_HEADS, QB, LANES), F32),
            pltpu.VMEM((B_HEADS, QB, LANES), F32),
            pltpu.VMEM((B_HEADS * QB, KV_RANK), F32),
            pltpu.VMEM((QB, LANES), I32),
        ],
        compiler_params=_cparams(("arbitrary", "arbitrary")),
        name="dsa",
    )(proj3, proj3, proj3, ckv_pad, kk_pad, w_uk, w_uv, bias_b)


SB_HG = 8


def _sb_kernel(q_ref, k_ref, v_ref, o_ref, acc_scr):
    n = pl.program_id(2)
    scale = C_HEAD_DIM ** -0.5
    rows = SB_HG * QB
    row = lax.broadcasted_iota(I32, (QB, QB), 0)
    col = lax.broadcasted_iota(I32, (QB, QB), 1)
    suffix = jnp.where(row > col, 1.0, 0.0).astype(BF16)
    suffix2 = jnp.concatenate([suffix, suffix], axis=0)
    earlier = lax.broadcasted_iota(I32, (rows, QB), 1) < (lax.broadcasted_iota(I32, (rows, QB), 0) % QB)

    def head_cols(h):
        return slice(h * C_HEAD_DIM, (h + 1) * C_HEAD_DIM)

    def block(j, carry, diagonal):
        ks = pl.multiple_of(j * QB, QB)
        z = jnp.concatenate(
            [lax.dot_general(q_ref[0, :, head_cols(h)], k_ref[0, pl.ds(ks, QB), head_cols(h)],
                             (((1,), (1,)), ((), ())), preferred_element_type=F32)
             for h in range(SB_HG)], axis=0) * scale
        sp = jnp.maximum(z, 0.0) + jnp.log(1.0 + jnp.exp(-jnp.abs(z)))
        lk = jnp.where(earlier, -sp, 0.0) if diagonal else -sp
        hi = lk.astype(BF16)
        lo = (lk - hi.astype(F32)).astype(BF16)
        inner = jnp.dot(jnp.concatenate([hi, lo], axis=1), suffix2, preferred_element_type=F32)
        a = jnp.exp((z - sp) + (carry + inner))
        if diagonal:
            a = jnp.where(earlier, a, 0.0)
        a = a.astype(BF16)
        for h in range(SB_HG):
            r = slice(h * QB, (h + 1) * QB)
            acc_scr[r, :] += jnp.dot(a[r], v_ref[0, pl.ds(ks, QB), head_cols(h)],
                                     preferred_element_type=F32)
        return carry + jnp.sum(lk, axis=-1, keepdims=True)

    def all_done(carry):
        return (jnp.max(carry) < SB_DONE).astype(I32)

    acc_scr[...] = jnp.zeros(acc_scr.shape, F32)
    carry0 = block(n, jnp.zeros((rows, 1), F32), True)

    def cond(state):
        j, done, _ = state
        return (j >= 0) & (done == 0)

    def body(state):
        j, _, carry = state
        carry = block(j, carry, False)
        return j - 1, all_done(carry), carry

    lax.while_loop(cond, body, (n - 1, all_done(carry0), carry0))
    o_ref[0] = jnp.concatenate([acc_scr[h * QB:(h + 1) * QB, :] for h in range(SB_HG)],
                               axis=-1).astype(o_ref.dtype)


def _sb(qkv3):
    bn, seq, _ = qkv3.shape
    nblk = seq // QB
    gw = SB_HG * C_HEAD_DIM
    ngrp = C_HEADS // SB_HG
    return pl.pallas_call(
        _sb_kernel,
        grid=(bn, ngrp, nblk),
        in_specs=[
            pl.BlockSpec((1, QB, gw), lambda b, g, n: (b, n, g)),
            pl.BlockSpec((1, seq, gw), lambda b, g, n: (b, 0, ngrp + g)),
            pl.BlockSpec((1, seq, gw), lambda b, g, n: (b, 0, 2 * ngrp + g)),
        ],
        out_specs=pl.BlockSpec((1, QB, gw), lambda b, g, n: (b, n, g)),
        out_shape=jax.ShapeDtypeStruct((bn, seq, C_HEADS * C_HEAD_DIM), BF16),
        scratch_shapes=[pltpu.VMEM((SB_HG * QB, C_HEAD_DIM), F32)],
        compiler_params=_cparams(("arbitrary", "arbitrary", "arbitrary")),
        name="stick_breaking",
    )(qkv3, qkv3, qkv3)


def _t5_bucket(rel):
    half = NUM_BUCKETS // 2
    max_exact = half // 2
    n = jnp.abs(rel)
    nf = jnp.maximum(n, 1).astype(F32)
    large = max_exact + (jnp.log(nf / max_exact) / math.log(MAX_DISTANCE / max_exact)
                         * (half - max_exact)).astype(I32)
    large = jnp.minimum(large, half - 1)
    return jnp.where(rel > 0, half, 0) + jnp.where(n < max_exact, n, large)


def _band_bias(rel_bias):
    rel = (jnp.arange(2 * QB) - QB)[None, :] - jnp.arange(QB)[:, None]
    band = rel_bias.astype(F32)[_t5_bucket(rel)]
    far = rel_bias.astype(F32)[_t5_bucket(jnp.int32(-(QB + 1)))]
    return jnp.transpose(band, (2, 0, 1)), far


def _even_w_in(w_in):
    d = w_in.shape[0]
    sizes = (A_HEADS * HEAD_DIM, A_KV_HEADS * HEAD_DIM, A_KV_HEADS * HEAD_DIM,
             B_HEADS * HEAD_DIM, KV_RANK, IDX_HEADS * IDX_DIM, IDX_DIM, IDX_HEADS)
    offs = np.cumsum((0,) + sizes)
    qa, ka, va, qb, cl, qi, ki, wi = [w_in[:, offs[i]:offs[i + 1]] for i in range(8)]
    qi = jnp.pad(qi.reshape(d, IDX_HEADS, IDX_DIM), ((0, 0), (0, 0), (0, LANES - IDX_DIM)))
    kw = jnp.pad(jnp.concatenate([ki, wi], axis=1), ((0, 0), (0, LANES - IDX_DIM - IDX_HEADS)))
    w = jnp.concatenate([qa, qb, qi.reshape(d, IDX_HEADS * LANES), ka, va, cl, kw], axis=1)
    return jnp.pad(w, ((0, 0), (0, EV_WIDTH - w.shape[1]))).astype(BF16)


def kernel(x, c, rel_bias, ada_w, ada_b, norm_mix_g, norm_ffn_g, ev_w_in, ev_kv_norm_g, ev_w_uk,
           ev_w_uv, ev_sinks, ev_w_out, od_w_in, od_w_out, ffn_w_up, ffn_conv_w, ffn_conv_b,
           ffn_w_down, final_g):
    bn, seq, d = x.shape
    depth = ada_w.shape[0]
    x2 = x.reshape(bn * seq, d)

    mod = _ada_mod(c, ada_w, ada_b)
    band, far = _band_bias(rel_bias)
    bias_a = band[:A_HEADS]
    bias_b = (band[A_HEADS:] - far[A_HEADS:, None, None]) * LOG2E

    for i in range(depth):
        sh1, sc1, g1, sh2, sc2, g2 = [m.reshape(bn, 1, d) for m in jnp.split(mod[i], 6, axis=-1)]
        g_mix = norm_mix_g[i].reshape(1, d)
        j = i // 2
        if i % 2 == 0:
            proj = _in_proj(x2, g_mix, sc1, sh1, _even_w_in(ev_w_in[j]),
                            ev_kv_norm_g[j].reshape(1, KV_RANK), seq, tm=1024, tn=1024, kv_col=EV_CL)
            proj3 = proj.reshape(bn, seq, EV_WIDTH)
            lead = ((0, 0), (QB, 0), (0, 0))
            ckv_pad = jnp.pad(proj3[:, :, EV_CL:EV_CL + KV_RANK], lead)
            kk_pad = jnp.pad(proj3[:, :, EV_KW:EV_KW + LANES], lead)
            o_a = _swa(proj3, bias_a, ev_sinks[j])
            o_b = _dsa(proj3, ckv_pad, kk_pad, ev_w_uk[j].astype(BF16), ev_w_uv[j].astype(BF16), bias_b)
            w_out = ev_w_out[j].astype(BF16)
            wa = w_out[:A_HEADS * HEAD_DIM]
            wb = w_out[A_HEADS * HEAD_DIM:]
            x2 = _out_proj([o_a.reshape(bn * seq, -1), o_b.reshape(bn * seq, -1)], [wa, wb],
                           x2, g1, seq, tm=512)
        else:
            qkv = _in_proj(x2, g_mix, sc1, sh1, od_w_in[j].astype(BF16),
                           jnp.ones((1, KV_RANK), F32), seq, tm=1024, tn=1024)
            o_c = _sb(qkv.reshape(bn, seq, -1))
            x2 = _out_proj([o_c.reshape(bn * seq, -1)], [od_w_out[j].astype(BF16)], x2, g1, seq, tm=512)
        conv4 = jnp.concatenate([ffn_conv_w[i], ffn_conv_b[i][None]], axis=0)
        x2 = _conv_ffn(x2, norm_ffn_g[i].reshape(1, d), sc2, sh2, g2, ffn_w_up[i].astype(BF16), conv4,
                       ffn_w_down[i].astype(BF16), final_g.reshape(1, d), seq,
                       tm=512, tf=512, final=(i == depth - 1))
    return x2.reshape(bn, seq, d)
```

```python
import functools
import math

import jax
import jax.numpy as jnp
import numpy as np
from jax import lax
from jax.experimental import pallas as pl
from jax.experimental.pallas import tpu as pltpu

F32 = jnp.float32
BF16 = jnp.bfloat16
I32 = jnp.int32

CHUNK = 64
QB = 128
HEAD_DIM = 64
A_HEADS = 16
A_KV_HEADS = 4
A_GROUP = A_HEADS // A_KV_HEADS
WINDOW_CHUNKS = 2
B_HEADS = 16
KV_RANK = 256
IDX_HEADS = 8
IDX_DIM = 64
TOPK_MAX = 256
C_HEADS = 16
C_HEAD_DIM = 128
NUM_BUCKETS = 32
MAX_DISTANCE = 128
CONV_W = 3
EPS = 1e-6

LANES = 128
SUBLANES = 8
MXU_DIM = 256
VMEM_LIMIT = 56 * 1024 * 1024

NEG_BIG = -1e30
LOG2E = math.log2(math.e)
INT_MIN = -(2 ** 31)
SB_DONE = -88.0

EV_QA = 0
EV_QB = EV_QA + A_HEADS * HEAD_DIM
EV_QI = EV_QB + B_HEADS * HEAD_DIM
EV_KA = EV_QI + IDX_HEADS * LANES
EV_VA = EV_KA + A_KV_HEADS * HEAD_DIM
EV_CL = EV_VA + A_KV_HEADS * HEAD_DIM
EV_KW = EV_CL + KV_RANK
EV_WIDTH = 4096


def _cparams(sem):
    return pltpu.CompilerParams(dimension_semantics=sem, vmem_limit_bytes=VMEM_LIMIT)


NORM_RB = 16
NORM_UNROLL = 8


def _row_blocks(n_rows, fn):
    def body(r, _):
        fn(pl.ds(pl.multiple_of(r * NORM_RB, NORM_RB), NORM_RB))
        return 0
    lax.fori_loop(0, n_rows // NORM_RB, body, 0, unroll=NORM_UNROLL)


def _norm_mod_to(h_scr, x_ref, g_ref, sc_ref, sh_ref):
    def block(rows):
        x = x_ref[rows, :]
        ms = jnp.mean(x * x, axis=-1, keepdims=True)
        y = x * lax.rsqrt(ms + EPS)
        h_scr[rows, :] = ((y * g_ref[...]) * (1.0 + sc_ref[0]) + sh_ref[0]).astype(BF16)
    _row_blocks(x_ref.shape[0], block)


def _silu(x):
    return x / (1.0 + jnp.exp(-x))


def _ada_kernel(c_ref, w_ref, b_ref, o_ref):
    a = _silu(c_ref[...]).astype(BF16)
    o_ref[0] = jnp.dot(a, w_ref[0].astype(BF16), preferred_element_type=F32) + b_ref[0]


def _ada_mod(c, ada_w, ada_b):
    depth, d, n = ada_w.shape
    bn = c.shape[0]
    rows = -(-bn // SUBLANES) * SUBLANES
    c_pad = jnp.pad(c, ((0, rows - bn), (0, 0)))
    tn = 1024
    out = pl.pallas_call(
        _ada_kernel,
        grid=(depth, n // tn),
        in_specs=[
            pl.BlockSpec((rows, d), lambda l, j: (0, 0)),
            pl.BlockSpec((1, d, tn), lambda l, j: (l, 0, j)),
            pl.BlockSpec((1, 1, tn), lambda l, j: (l, 0, j)),
        ],
        out_specs=pl.BlockSpec((1, rows, tn), lambda l, j: (l, 0, j)),
        out_shape=jax.ShapeDtypeStruct((depth, rows, n), F32),
        compiler_params=_cparams(("arbitrary", "arbitrary")),
        name="ada_mod",
    )(c_pad, ada_w, ada_b.reshape(depth, 1, n))
    return out[:, :bn]


def _in_proj_kernel(x_ref, g_ref, sc_ref, sh_ref, w_ref, kvg_ref, o_ref, h_scr, *, kv_tile, kv_off):
    j = pl.program_id(1)

    @pl.when(j == 0)
    def _():
        _norm_mod_to(h_scr, x_ref, g_ref, sc_ref, sh_ref)

    def project():
        return jnp.dot(h_scr[...], w_ref[...], preferred_element_type=F32)

    if kv_tile is None:
        o_ref[...] = project().astype(o_ref.dtype)
    else:
        @pl.when(j != kv_tile)
        def _():
            o_ref[...] = project().astype(o_ref.dtype)

        @pl.when(j == kv_tile)
        def _():
            acc = project()
            lat = acc[:, kv_off:kv_off + KV_RANK]
            ms = jnp.mean(lat * lat, axis=-1, keepdims=True)
            lat = lat * lax.rsqrt(ms + EPS) * kvg_ref[...]
            o_ref[...] = acc.astype(o_ref.dtype)
            o_ref[:, kv_off:kv_off + KV_RANK] = lat.astype(o_ref.dtype)


def _in_proj(x2, g, sc, sh, w, kv_g, seq, *, tm, tn, kv_col=None):
    n_rows, d = x2.shape
    n_out = w.shape[1]
    tiles_per_batch = seq // tm
    if kv_col is None:
        kv_tile, kv_off = None, 0
    else:
        kv_tile, kv_off = kv_col // tn, kv_col % tn
    kern = functools.partial(_in_proj_kernel, kv_tile=kv_tile, kv_off=kv_off)
    return pl.pallas_call(
        kern,
        grid=(n_rows // tm, n_out // tn),
        in_specs=[
            pl.BlockSpec((tm, d), lambda i, j: (i, 0)),
            pl.BlockSpec((1, d), lambda i, j: (0, 0)),
            pl.BlockSpec((1, 1, d), lambda i, j: (i // tiles_per_batch, 0, 0)),
            pl.BlockSpec((1, 1, d), lambda i, j: (i // tiles_per_batch, 0, 0)),
            pl.BlockSpec((d, tn), lambda i, j: (0, j)),
            pl.BlockSpec((1, KV_RANK), lambda i, j: (0, 0)),
        ],
        out_specs=pl.BlockSpec((tm, tn), lambda i, j: (i, j)),
        out_shape=jax.ShapeDtypeStruct((n_rows, n_out), BF16),
        scratch_shapes=[pltpu.VMEM((tm, d), BF16)],
        compiler_params=_cparams(("arbitrary", "arbitrary")),
        name="in_proj",
    )(x2, g, sc, sh, w, kv_g)


def _out_proj_kernel(*refs, n_act):
    acts = refs[:n_act]
    ws = refs[n_act:2 * n_act]
    x_ref, gate_ref, o_ref = refs[2 * n_act:]
    y = jnp.dot(acts[0][...], ws[0][...], preferred_element_type=F32)
    for a_ref, w_ref in zip(acts[1:], ws[1:]):
        y = y + jnp.dot(a_ref[...], w_ref[...], preferred_element_type=F32)
    o_ref[...] = x_ref[...] + gate_ref[0] * y


def _out_proj(acts, ws, x2, gate, seq, *, tm):
    n_rows, d = x2.shape
    tiles_per_batch = seq // tm
    n_act = len(acts)
    in_specs = ([pl.BlockSpec((tm, a.shape[1]), lambda i: (i, 0)) for a in acts]
                + [pl.BlockSpec(w.shape, lambda i: (0, 0)) for w in ws]
                + [pl.BlockSpec((tm, d), lambda i: (i, 0)),
                   pl.BlockSpec((1, 1, d), lambda i: (i // tiles_per_batch, 0, 0))])
    return pl.pallas_call(
        functools.partial(_out_proj_kernel, n_act=n_act),
        grid=(n_rows // tm,),
        in_specs=in_specs,
        out_specs=pl.BlockSpec((tm, d), lambda i: (i, 0)),
        out_shape=jax.ShapeDtypeStruct((n_rows, d), F32),
        compiler_params=_cparams(("arbitrary",)),
        name="out_proj",
    )(*acts, *ws, x2, gate)


FFN_RB = 64


def _ffn_kernel(x_ref, g_ref, sc_ref, sh_ref, gate_ref, wg_ref, wv_ref, cg_ref, cv_ref, wd_ref,
                wdl_ref, fg_ref, o_ref, h_scr, ug_scr, uv_scr, a0_scr, a1_scr, carry_g, carry_v, *,
                nf, tiles_per_batch, final):
    i = pl.program_id(0)
    f = pl.program_id(1)
    tm = x_ref.shape[0]

    @pl.when(f == 0)
    def _():
        _norm_mod_to(h_scr, x_ref, g_ref, sc_ref, sh_ref)
        o_ref[...] = jnp.zeros(o_ref.shape, F32)
        a1_scr[...] = jnp.zeros(a1_scr.shape, BF16)

    batch_start = (i % tiles_per_batch) == 0

    @pl.when(batch_start)
    def _():
        ug_scr[0:SUBLANES, :] = jnp.zeros((SUBLANES, ug_scr.shape[1]), F32)
        uv_scr[0:SUBLANES, :] = jnp.zeros((SUBLANES, uv_scr.shape[1]), F32)

    @pl.when(jnp.logical_not(batch_start))
    def _():
        ug_scr[0:SUBLANES, :] = carry_g[f]
        uv_scr[0:SUBLANES, :] = carry_v[f]

    def conv(u_scr, cw, base):
        y = cw[2:3] * u_scr[base:base + FFN_RB, :] + cw[3:4]
        y = y + cw[1:2] * u_scr[base - 1:base - 1 + FFN_RB, :]
        return y + cw[0:1] * u_scr[base - 2:base - 2 + FFN_RB, :]

    def step(a_cur, a_prev):
        ug_scr[SUBLANES:, :] = jnp.dot(h_scr[...], wg_ref[...], preferred_element_type=F32)
        uv_scr[SUBLANES:, :] = jnp.dot(h_scr[...], wv_ref[...], preferred_element_type=F32)
        o_ref[...] += jnp.dot(a_prev[...], wd_ref[...], preferred_element_type=F32)
        carry_g[f] = ug_scr[tm:tm + SUBLANES, :]
        carry_v[f] = uv_scr[tm:tm + SUBLANES, :]
        cwg = cg_ref[...]
        cwv = cv_ref[...]
        for r in range(tm // FFN_RB):
            base = SUBLANES + r * FFN_RB
            act = _silu(conv(ug_scr, cwg, base)) * conv(uv_scr, cwv, base)
            a_cur[r * FFN_RB:(r + 1) * FFN_RB, :] = act.astype(BF16)

    @pl.when(f % 2 == 0)
    def _():
        step(a0_scr, a1_scr)

    @pl.when(f % 2 == 1)
    def _():
        step(a1_scr, a0_scr)

    @pl.when(f == nf - 1)
    def _():
        a_last = a0_scr if (nf - 1) % 2 == 0 else a1_scr
        o_ref[...] += jnp.dot(a_last[...], wdl_ref[...], preferred_element_type=F32)

        def residual(rows):
            xo = x_ref[rows, :] + gate_ref[0] * o_ref[rows, :]
            if final:
                ms = jnp.mean(xo * xo, axis=-1, keepdims=True)
                xo = xo * lax.rsqrt(ms + EPS) * fg_ref[...]
            o_ref[rows, :] = xo
        _row_blocks(tm, residual)


def _conv_ffn(x2, g, sc, sh, gate, w_up, conv4, w_down, final_g, seq, *, tm, tf, final):
    n_rows, d = x2.shape
    d_ff = w_down.shape[0]
    nf = d_ff // tf
    tiles_per_batch = seq // tm
    kern = functools.partial(_ffn_kernel, nf=nf, tiles_per_batch=tiles_per_batch, final=final)
    return pl.pallas_call(
        kern,
        grid=(n_rows // tm, nf),
        in_specs=[
            pl.BlockSpec((tm, d), lambda i, f: (i, 0)),
            pl.BlockSpec((1, d), lambda i, f: (0, 0)),
            pl.BlockSpec((1, 1, d), lambda i, f: (i // tiles_per_batch, 0, 0)),
            pl.BlockSpec((1, 1, d), lambda i, f: (i // tiles_per_batch, 0, 0)),
            pl.BlockSpec((1, 1, d), lambda i, f: (i // tiles_per_batch, 0, 0)),
            pl.BlockSpec((d, tf), lambda i, f: (0, f)),
            pl.BlockSpec((d, tf), lambda i, f: (0, f + nf)),
            pl.BlockSpec((CONV_W + 1, tf), lambda i, f: (0, f)),
            pl.BlockSpec((CONV_W + 1, tf), lambda i, f: (0, f + nf)),
            pl.BlockSpec((tf, d), lambda i, f: (jnp.maximum(f - 1, 0), 0)),
            pl.BlockSpec((tf, d), lambda i, f: (nf - 1, 0)),
            pl.BlockSpec((1, d), lambda i, f: (0, 0)),
        ],
        out_specs=pl.BlockSpec((tm, d), lambda i, f: (i, 0)),
        out_shape=jax.ShapeDtypeStruct((n_rows, d), F32),
        scratch_shapes=[
            pltpu.VMEM((tm, d), BF16),
            pltpu.VMEM((tm + SUBLANES, tf), F32),
            pltpu.VMEM((tm + SUBLANES, tf), F32),
            pltpu.VMEM((tm, tf), BF16),
            pltpu.VMEM((tm, tf), BF16),
            pltpu.VMEM((nf, SUBLANES, tf), F32),
            pltpu.VMEM((nf, SUBLANES, tf), F32),
        ],
        compiler_params=_cparams(("arbitrary", "arbitrary")),
        name="conv_ffn",
    )(x2, g, sc, sh, gate, w_up, w_up, conv4, conv4, w_down, w_down, final_g)


def _swa_kernel(q_ref, kp_ref, ko_ref, vp_ref, vo_ref, bias_ref, sink_ref, o_ref):
    n = pl.program_id(1)
    row = lax.broadcasted_iota(I32, (QB, 2 * QB), 0)
    col = lax.broadcasted_iota(I32, (QB, 2 * QB), 1)
    qc = row // CHUNK
    kc = col // CHUNK - QB // CHUNK
    allowed = (kc <= qc) & (kc >= qc - WINDOW_CHUNKS) & ((col >= QB) | (n > 0))
    allowed_g = jnp.concatenate([allowed] * A_GROUP, axis=0)
    k_all = jnp.concatenate([kp_ref[0], ko_ref[0]], axis=0)
    v_all = jnp.concatenate([vp_ref[0], vo_ref[0]], axis=0)
    q_all = q_ref[0] * (HEAD_DIM ** -0.5)
    low = lax.broadcasted_iota(I32, (QB, LANES), 1) < HEAD_DIM
    zero = jnp.zeros((QB, LANES), q_all.dtype)
    outs = []
    for g in range(A_KV_HEADS):
        heads = range(g * A_GROUP, (g + 1) * A_GROUP)
        kg = k_all[:, g * HEAD_DIM:(g + 1) * HEAD_DIM]
        vg = v_all[:, g * HEAD_DIM:(g + 1) * HEAD_DIM]
        k2 = jnp.concatenate([kg, kg], axis=1)
        v2 = jnp.concatenate([vg, vg], axis=1)
        parts = []
        for c in range(A_GROUP // 2):
            col = (g * A_GROUP + 2 * c) * HEAD_DIM
            q2 = q_all[:, col:col + LANES]
            parts += [jnp.where(low, q2, zero), jnp.where(low, zero, q2)]
        qg = jnp.concatenate(parts, axis=0)
        s = lax.dot_general(qg, k2, (((1,), (1,)), ((), ())), preferred_element_type=F32)
        bias = jnp.concatenate([bias_ref[h] for h in heads], axis=0)
        s = jnp.where(allowed_g, s + bias, NEG_BIG)
        sink = jnp.concatenate([sink_ref[h] for h in heads], axis=0)[:, 0:1]
        m = jnp.maximum(jnp.max(s, axis=-1, keepdims=True), sink)
        p = jnp.exp(s - m)
        denom = jnp.sum(p, axis=-1, keepdims=True) + jnp.exp(sink - m)
        o = jnp.dot(p.astype(BF16), v2, preferred_element_type=F32) / denom
        for c in range(A_GROUP // 2):
            even = o[(2 * c) * QB:(2 * c + 1) * QB]
            odd = o[(2 * c + 1) * QB:(2 * c + 2) * QB]
            outs.append(jnp.where(low, even, odd))
    o_ref[0] = jnp.concatenate(outs, axis=-1).astype(o_ref.dtype)


def _swa(proj3, bias_a, sinks):
    bn, seq, _ = proj3.shape
    nblk = seq // QB
    kvw = A_KV_HEADS * HEAD_DIM
    qw = A_HEADS * HEAD_DIM
    sink_b = jnp.broadcast_to(sinks.astype(F32)[:, None, None], (A_HEADS, QB, LANES))
    prev = lambda c: (lambda b, n: (b, jnp.maximum(n - 1, 0), c))
    own = lambda c: (lambda b, n: (b, n, c))
    return pl.pallas_call(
        _swa_kernel,
        grid=(bn, nblk),
        in_specs=[
            pl.BlockSpec((1, QB, qw), own(EV_QA // qw)),
            pl.BlockSpec((1, QB, kvw), prev(EV_KA // kvw)),
            pl.BlockSpec((1, QB, kvw), own(EV_KA // kvw)),
            pl.BlockSpec((1, QB, kvw), prev(EV_VA // kvw)),
            pl.BlockSpec((1, QB, kvw), own(EV_VA // kvw)),
            pl.BlockSpec((A_HEADS, QB, 2 * QB), lambda b, n: (0, 0, 0)),
            pl.BlockSpec((A_HEADS, QB, LANES), lambda b, n: (0, 0, 0)),
        ],
        out_specs=pl.BlockSpec((1, QB, qw), lambda b, n: (b, n, 0)),
        out_shape=jax.ShapeDtypeStruct((bn, seq, qw), BF16),
        compiler_params=_cparams(("arbitrary", "arbitrary")),
        name="swa",
    )(proj3, proj3, proj3, proj3, proj3, bias_a, sink_b)


KCH = 2 * QB


def _dsa_kernel(qb_ref, qi_ref, kq_ref, ckv_ref, kk_ref, wuk_ref, wuv_ref, bias_ref, o_ref,
                qst_scr, wb_scr, key_scr, qlat_scr, m_scr, l_scr, acc_scr, cut_scr, *, topk):
    n = pl.program_id(1)
    nch = n // 2 + 1
    spad = key_scr.shape[1]

    def chunk_start(e):
        return pl.multiple_of((n - 2 * e) * QB, QB)

    idx_scale = IDX_DIM ** -0.5 * IDX_HEADS ** -0.5
    for h in range(IDX_HEADS):
        qst_scr[h * QB:(h + 1) * QB, :] = qi_ref[0, :, h * LANES:(h + 1) * LANES]
        w = kq_ref[0, :, IDX_DIM + h:IDX_DIM + h + 1].astype(F32) * idx_scale
        wb_scr[h] = jnp.broadcast_to(w, (QB, KCH))
    q_all = qb_ref[0]
    for h in range(B_HEADS):
        ql = jnp.dot(q_all[:, h * HEAD_DIM:(h + 1) * HEAD_DIM], wuk_ref[h],
                     preferred_element_type=F32)
        qlat_scr[h * QB:(h + 1) * QB, :] = (ql * (HEAD_DIM ** -0.5 * LOG2E)).astype(BF16)

    row = lax.broadcasted_iota(I32, (QB, KCH), 0)
    lane = lax.broadcasted_iota(I32, (QB, KCH), 1)
    limit = (n * QB + CHUNK) + (row // CHUNK) * CHUNK

    def idx_body(e, _):
        ks = chunk_start(e)
        kj = kk_ref[0, pl.ds(ks, KCH), :]
        r = lax.dot_general(qst_scr[...], kj, (((1,), (1,)), ((), ())), preferred_element_type=F32)
        sc = jnp.maximum(r[0:QB], 0.0) * wb_scr[0]
        for h in range(1, IDX_HEADS):
            sc = sc + jnp.maximum(r[h * QB:(h + 1) * QB], 0.0) * wb_scr[h]
        bits = lax.bitcast_convert_type(sc, I32)
        key = bits ^ ((bits >> 31) & 0x7FFFFFFF)
        key = jnp.where(bits == INT_MIN, 0, key)
        kpos = ks - QB + lane
        key = jnp.where((kpos >= 0) & (kpos < limit), key, INT_MIN)
        key_scr[:, pl.ds(ks, KCH)] = key
        return 0

    lax.fori_loop(0, nch, idx_body, 0)

    def count(pred):
        def body(e, acc):
            ks = chunk_start(e)
            for half in range(KCH // LANES):
                off = pl.multiple_of(ks + half * LANES, LANES)
                k = key_scr[:, pl.ds(off, LANES)]
                acc = acc + jnp.where(pred(k, off), 1, 0)
            return acc
        c = lax.fori_loop(0, nch, body, jnp.zeros((QB, LANES), I32))
        return jnp.sum(c, axis=1, keepdims=True)

    lane1 = lax.broadcasted_iota(I32, (QB, LANES), 1)

    def bit_body(b, t):
        cand = t ^ jnp.left_shift(jnp.int32(1), 31 - b)
        cand_b = jnp.broadcast_to(cand, (QB, LANES))
        tot = count(lambda k, off: k >= cand_b)
        return jnp.where(tot >= topk, cand, t)

    thr = lax.fori_loop(0, 32, bit_body, jnp.full((QB, 1), INT_MIN, I32))
    thr = jnp.maximum(thr, INT_MIN + 1)

    thr_b = jnp.broadcast_to(thr, (QB, LANES))
    n_gt = count(lambda k, off: k > thr_b)
    n_eq = count(lambda k, off: k == thr_b)
    need = topk - n_gt
    cut_scr[...] = jnp.full((QB, LANES), 2 * spad, I32)
    excess = jnp.max(jnp.where(n_eq > need, 1, 0))

    @pl.when(excess > 0)
    def _():
        nbits = int(spad).bit_length()

        def cut_body(b, c):
            cand = c | jnp.left_shift(jnp.int32(1), nbits - 1 - b)
            cand_b = jnp.broadcast_to(cand, (QB, LANES))
            tot = count(lambda k, off: (k == thr_b) & (off + lane1 < cand_b))
            return jnp.where(tot <= need, cand, c)

        c = lax.fori_loop(0, nbits, cut_body, jnp.zeros((QB, 1), I32))
        cut_scr[...] = jnp.broadcast_to(jnp.where(n_eq > need, c, 2 * spad), (QB, LANES))

    cut = cut_scr[:, 0:1]

    def masked_scores(e, near):
        ks = chunk_start(e)
        kv = ckv_ref[0, pl.ds(ks, KCH), :]
        k = key_scr[:, pl.ds(ks, KCH)]
        sel = (k > thr) | ((k == thr) & (ks + lane < cut))
        addmask = jnp.where(sel, 0.0, NEG_BIG)
        s = lax.dot_general(qlat_scr[...], kv, (((1,), (1,)), ((), ())), preferred_element_type=F32)
        s = s.reshape(B_HEADS, QB, KCH) + addmask[None]
        if near:
            s = s + bias_ref[...]
        return s, kv

    def max_pass(e, near):
        s, _ = masked_scores(e, near)
        m_scr[...] = jnp.maximum(m_scr[...], jnp.maximum(s[:, :, :LANES], s[:, :, LANES:]))

    def sum_pass(e, near):
        s, kv = masked_scores(e, near)
        m = m_scr[...]
        p_lo = jnp.exp2(s[:, :, :LANES] - m)
        p_hi = jnp.exp2(s[:, :, LANES:] - m)
        l_scr[...] += p_lo + p_hi
        p = jnp.concatenate([p_lo, p_hi], axis=-1).astype(BF16).reshape(B_HEADS * QB, KCH)
        acc_scr[...] += jnp.dot(p, kv, preferred_element_type=F32)

    def sweep(fn):
        def far_body(e, _):
            fn(e, False)
            return 0
        lax.fori_loop(1, nch, far_body, 0)
        fn(0, True)

    m_scr[...] = jnp.full(m_scr.shape, NEG_BIG, F32)
    sweep(max_pass)
    m_scr[...] = jnp.broadcast_to(jnp.max(m_scr[...], axis=-1, keepdims=True), m_scr.shape)
    l_scr[...] = jnp.zeros(l_scr.shape, F32)
    acc_scr[...] = jnp.zeros(acc_scr.shape, F32)
    sweep(sum_pass)

    outs = []
    for h in range(B_HEADS):
        l = jnp.sum(l_scr[h], axis=-1, keepdims=True)
        o_lat = (acc_scr[h * QB:(h + 1) * QB, :] / l).astype(BF16)
        outs.append(jnp.dot(o_lat, wuv_ref[h], preferred_element_type=F32))
    o_ref[0] = jnp.concatenate(outs, axis=-1).astype(o_ref.dtype)


def _dsa(proj3, ckv_pad, kk_pad, w_uk, w_uv, bias_b):
    bn, seq, _ = proj3.shape
    nblk = seq // QB
    spad = seq + QB
    qw = B_HEADS * HEAD_DIM
    iw = IDX_HEADS * LANES
    return pl.pallas_call(
        functools.partial(_dsa_kernel, topk=min(TOPK_MAX, seq // 4)),
        grid=(bn, nblk),
        in_specs=[
            pl.BlockSpec((1, QB, qw), lambda b, n: (b, n, EV_QB // qw)),
            pl.BlockSpec((1, QB, iw), lambda b, n: (b, n, EV_QI // iw)),
            pl.BlockSpec((1, QB, LANES), lambda b, n: (b, n, EV_KW // LANES)),
            pl.BlockSpec((1, spad, KV_RANK), lambda b, n: (b, 0, 0)),
            pl.BlockSpec((1, spad, LANES), lambda b, n: (b, 0, 0)),
            pl.BlockSpec(w_uk.shape, lambda b, n: (0, 0, 0)),
            pl.BlockSpec(w_uv.shape, lambda b, n: (0, 0, 0)),
            pl.BlockSpec((B_HEADS, QB, KCH), lambda b, n: (0, 0, 0)),
        ],
        out_specs=pl.BlockSpec((1, QB, qw), lambda b, n: (b, n, 0)),
        out_shape=jax.ShapeDtypeStruct((bn, seq, qw), BF16),
        scratch_shapes=[
            pltpu.VMEM((IDX_HEADS * QB, LANES), BF16),
            pltpu.VMEM((IDX_HEADS, QB, KCH), F32),
            pltpu.VMEM((QB, spad), I32),
            pltpu.VMEM((B_HEADS * QB, KV_RANK), BF16),
            pltpu.VMEM((B_HEADS, QB, LANES), F32),
            pltpu.VMEM((B_HEADS, QB, LANES), F32),
            pltpu.VMEM((B_HEADS * QB, KV_RANK), F32),
            pltpu.VMEM((QB, LANES), I32),
        ],
        compiler_params=_cparams(("arbitrary", "arbitrary")),
        name="dsa",
    )(proj3, proj3, proj3, ckv_pad, kk_pad, w_uk, w_uv, bias_b)


SB_HG = 8


def _sb_kernel(q_ref, k_ref, v_ref, o_ref, acc_scr):
    n = pl.program_id(2)
    scale = C_HEAD_DIM ** -0.5
    rows = SB_HG * QB
    row = lax.broadcasted_iota(I32, (QB, QB), 0)
    col = lax.broadcasted_iota(I32, (QB, QB), 1)
    suffix = jnp.where(row > col, 1.0, 0.0).astype(BF16)
    suffix2 = jnp.concatenate([suffix, suffix], axis=0)
    earlier = lax.broadcasted_iota(I32, (rows, QB), 1) < (lax.broadcasted_iota(I32, (rows, QB), 0) % QB)

    def head_cols(h):
        return slice(h * C_HEAD_DIM, (h + 1) * C_HEAD_DIM)

    def block(j, carry, diagonal):
        ks = pl.multiple_of(j * QB, QB)
        z = jnp.concatenate(
            [lax.dot_general(q_ref[0, :, head_cols(h)], k_ref[0, pl.ds(ks, QB), head_cols(h)],
                             (((1,), (1,)), ((), ())), preferred_element_type=F32)
             for h in range(SB_HG)], axis=0) * scale
        sp = jnp.maximum(z, 0.0) + jnp.log(1.0 + jnp.exp(-jnp.abs(z)))
        lk = jnp.where(earlier, -sp, 0.0) if diagonal else -sp
        hi = lk.astype(BF16)
        lo = (lk - hi.astype(F32)).astype(BF16)
        inner = jnp.dot(jnp.concatenate([hi, lo], axis=1), suffix2, preferred_element_type=F32)
        a = jnp.exp((z - sp) + (carry + inner))
        if diagonal:
            a = jnp.where(earlier, a, 0.0)
        a = a.astype(BF16)
        for h in range(SB_HG):
            r = slice(h * QB, (h + 1) * QB)
            acc_scr[r, :] += jnp.dot(a[r], v_ref[0, pl.ds(ks, QB), head_cols(h)],
                                     preferred_element_type=F32)
        return carry + jnp.sum(lk, axis=-1, keepdims=True)

    def all_done(carry):
        return (jnp.max(carry) < SB_DONE).astype(I32)

    acc_scr[...] = jnp.zeros(acc_scr.shape, F32)
    carry0 = block(n, jnp.zeros((rows, 1), F32), True)

    def cond(state):
        j, done, _ = state
        return (j >= 0) & (done == 0)

    def body(state):
        j, _, carry = state
        carry = block(j, carry, False)
        return j - 1, all_done(carry), carry

    lax.while_loop(cond, body, (n - 1, all_done(carry0), carry0))
    o_ref[0] = jnp.concatenate([acc_scr[h * QB:(h + 1) * QB, :] for h in range(SB_HG)],
                               axis=-1).astype(o_ref.dtype)


def _sb(qkv3):
    bn, seq, _ = qkv3.shape
    nblk = seq // QB
    gw = SB_HG * C_HEAD_DIM
    ngrp = C_HEADS // SB_HG
    return pl.pallas_call(
        _sb_kernel,
        grid=(bn, ngrp, nblk),
        in_specs=[
            pl.BlockSpec((1, QB, gw), lambda b, g, n: (b, n, g)),
            pl.BlockSpec((1, seq, gw), lambda b, g, n: (b, 0, ngrp + g)),
            pl.BlockSpec((1, seq, gw), lambda b, g, n: (b, 0, 2 * ngrp + g)),
        ],
        out_specs=pl.BlockSpec((1, QB, gw), lambda b, g, n: (b, n, g)),
        out_shape=jax.ShapeDtypeStruct((bn, seq, C_HEADS * C_HEAD_DIM), BF16),
        scratch_shapes=[pltpu.VMEM((SB_HG * QB, C_HEAD_DIM), F32)],
        compiler_params=_cparams(("arbitrary", "arbitrary", "arbitrary")),
        name="stick_breaking",
    )(qkv3, qkv3, qkv3)


def _t5_bucket(rel):
    half = NUM_BUCKETS // 2
    max_exact = half // 2
    n = jnp.abs(rel)
    nf = jnp.maximum(n, 1).astype(F32)
    large = max_exact + (jnp.log(nf / max_exact) / math.log(MAX_DISTANCE / max_exact)
                         * (half - max_exact)).astype(I32)
    large = jnp.minimum(large, half - 1)
    return jnp.where(rel > 0, half, 0) + jnp.where(n < max_exact, n, large)


def _band_bias(rel_bias):
    rel = (jnp.arange(2 * QB) - QB)[None, :] - jnp.arange(QB)[:, None]
    band = rel_bias.astype(F32)[_t5_bucket(rel)]
    far = rel_bias.astype(F32)[_t5_bucket(jnp.int32(-(QB + 1)))]
    return jnp.transpose(band, (2, 0, 1)), far


def _even_w_in(w_in):
    d = w_in.shape[0]
    sizes = (A_HEADS * HEAD_DIM, A_KV_HEADS * HEAD_DIM, A_KV_HEADS * HEAD_DIM,
             B_HEADS * HEAD_DIM, KV_RANK, IDX_HEADS * IDX_DIM, IDX_DIM, IDX_HEADS)
    offs = [int(o) for o in np.cumsum((0,) + sizes)]
    w = w_in.astype(BF16)
    moves = [(offs[0], offs[1], EV_QA), (offs[1], offs[2], EV_KA), (offs[2], offs[3], EV_VA),
             (offs[3], offs[4], EV_QB), (offs[4], offs[5], EV_CL), (offs[6], offs[8], EV_KW)]
    moves += [(offs[5] + h * IDX_DIM, offs[5] + (h + 1) * IDX_DIM, EV_QI + h * LANES)
              for h in range(IDX_HEADS)]
    out = jnp.zeros((d, EV_WIDTH), BF16)
    for lo, hi, dst in moves:
        out = lax.dynamic_update_slice(out, w[:, lo:hi], (0, dst))
    return out


def kernel(x, c, rel_bias, ada_w, ada_b, norm_mix_g, norm_ffn_g, ev_w_in, ev_kv_norm_g, ev_w_uk,
           ev_w_uv, ev_sinks, ev_w_out, od_w_in, od_w_out, ffn_w_up, ffn_conv_w, ffn_conv_b,
           ffn_w_down, final_g):
    bn, seq, d = x.shape
    depth = ada_w.shape[0]
    x2 = x.reshape(bn * seq, d)

    mod = _ada_mod(c, ada_w, ada_b)
    band, far = _band_bias(rel_bias)
    bias_a = band[:A_HEADS]
    bias_b = (band[A_HEADS:] - far[A_HEADS:, None, None]) * LOG2E

    for i in range(depth):
        sh1, sc1, g1, sh2, sc2, g2 = [m.reshape(bn, 1, d) for m in jnp.split(mod[i], 6, axis=-1)]
        g_mix = norm_mix_g[i].reshape(1, d)
        j = i // 2
        if i % 2 == 0:
            proj = _in_proj(x2, g_mix, sc1, sh1, _even_w_in(ev_w_in[j]),
                            ev_kv_norm_g[j].reshape(1, KV_RANK), seq, tm=1024, tn=1024, kv_col=EV_CL)
            proj3 = proj.reshape(bn, seq, EV_WIDTH)
            lead = ((0, 0), (QB, 0), (0, 0))
            ckv_pad = jnp.pad(proj3[:, :, EV_CL:EV_CL + KV_RANK], lead)
            kk_pad = jnp.pad(proj3[:, :, EV_KW:EV_KW + LANES], lead)
            o_a = _swa(proj3, bias_a, ev_sinks[j])
            o_b = _dsa(proj3, ckv_pad, kk_pad, ev_w_uk[j].astype(BF16), ev_w_uv[j].astype(BF16), bias_b)
            w_out = ev_w_out[j].astype(BF16)
            wa = w_out[:A_HEADS * HEAD_DIM]
            wb = w_out[A_HEADS * HEAD_DIM:]
            x2 = _out_proj([o_a.reshape(bn * seq, -1), o_b.reshape(bn * seq, -1)], [wa, wb],
                           x2, g1, seq, tm=512)
        else:
            qkv = _in_proj(x2, g_mix, sc1, sh1, od_w_in[j].astype(BF16),
                           jnp.ones((1, KV_RANK), F32), seq, tm=1024, tn=1024)
            o_c = _sb(qkv.reshape(bn, seq, -1))
            x2 = _out_proj([o_c.reshape(bn * seq, -1)], [od_w_out[j].astype(BF16)], x2, g1, seq, tm=512)
        conv4 = jnp.concatenate([ffn_conv_w[i], ffn_conv_b[i][None]], axis=0)
        x2 = _conv_ffn(x2, norm_ffn_g[i].reshape(1, d), sc2, sh2, g2, ffn_w_up[i].astype(BF16), conv4,
                       ffn_w_down[i].astype(BF16), final_g.reshape(1, d), seq,
                       tm=512, tf=512, final=(i == depth - 1))
    return x2.reshape(bn, seq, d)
```

```python
import functools
import math

import jax
import jax.numpy as jnp
import numpy as np
from jax import lax
from jax.experimental import pallas as pl
from jax.experimental.pallas import tpu as pltpu

F32 = jnp.float32
BF16 = jnp.bfloat16
I32 = jnp.int32

CHUNK = 64
QB = 128
HEAD_DIM = 64
A_HEADS = 16
A_KV_HEADS = 4
A_GROUP = A_HEADS // A_KV_HEADS
WINDOW_CHUNKS = 2
B_HEADS = 16
KV_RANK = 256
IDX_HEADS = 8
IDX_DIM = 64
TOPK_MAX = 256
C_HEADS = 16
C_HEAD_DIM = 128
NUM_BUCKETS = 32
MAX_DISTANCE = 128
CONV_W = 3
EPS = 1e-6

LANES = 128
SUBLANES = 8
MXU_DIM = 256
VMEM_LIMIT = 56 * 1024 * 1024

NEG_BIG = -1e30
LOG2E = math.log2(math.e)
INT_MIN = -(2 ** 31)
SB_DONE = -88.0

EV_QA = 0
EV_QB = EV_QA + A_HEADS * HEAD_DIM
EV_QI = EV_QB + B_HEADS * HEAD_DIM
EV_KA = EV_QI + IDX_HEADS * LANES
EV_VA = EV_KA + A_KV_HEADS * HEAD_DIM
EV_CL = EV_VA + A_KV_HEADS * HEAD_DIM
EV_KW = EV_CL + KV_RANK
EV_WIDTH = 4096


def _cparams(sem):
    return pltpu.CompilerParams(dimension_semantics=sem, vmem_limit_bytes=VMEM_LIMIT)


NORM_RB = 16
NORM_UNROLL = 8


def _row_blocks(n_rows, fn):
    def body(r, _):
        fn(pl.ds(pl.multiple_of(r * NORM_RB, NORM_RB), NORM_RB))
        return 0
    lax.fori_loop(0, n_rows // NORM_RB, body, 0, unroll=NORM_UNROLL)


def _norm_mod_to(h_scr, x_ref, g_ref, sc_ref, sh_ref):
    def block(rows):
        x = x_ref[rows, :]
        ms = jnp.mean(x * x, axis=-1, keepdims=True)
        y = x * lax.rsqrt(ms + EPS)
        h_scr[rows, :] = ((y * g_ref[...]) * (1.0 + sc_ref[0]) + sh_ref[0]).astype(BF16)
    _row_blocks(x_ref.shape[0], block)


def _silu(x):
    return x / (1.0 + jnp.exp(-x))


def _ada_kernel(c_ref, w_ref, b_ref, o_ref):
    a = _silu(c_ref[...]).astype(BF16)
    o_ref[0] = jnp.dot(a, w_ref[0].astype(BF16), preferred_element_type=F32) + b_ref[0]


def _ada_mod(c, ada_w, ada_b):
    depth, d, n = ada_w.shape
    bn = c.shape[0]
    rows = -(-bn // SUBLANES) * SUBLANES
    c_pad = jnp.pad(c, ((0, rows - bn), (0, 0)))
    tn = 1024
    out = pl.pallas_call(
        _ada_kernel,
        grid=(depth, n // tn),
        in_specs=[
            pl.BlockSpec((rows, d), lambda l, j: (0, 0)),
            pl.BlockSpec((1, d, tn), lambda l, j: (l, 0, j)),
            pl.BlockSpec((1, 1, tn), lambda l, j: (l, 0, j)),
        ],
        out_specs=pl.BlockSpec((1, rows, tn), lambda l, j: (l, 0, j)),
        out_shape=jax.ShapeDtypeStruct((depth, rows, n), F32),
        compiler_params=_cparams(("arbitrary", "arbitrary")),
        name="ada_mod",
    )(c_pad, ada_w, ada_b.reshape(depth, 1, n))
    return out[:, :bn]


def _in_proj_kernel(x_ref, g_ref, sc_ref, sh_ref, w_ref, kvg_ref, o_ref, h_scr, *, kv_tile, kv_off):
    j = pl.program_id(1)

    @pl.when(j == 0)
    def _():
        _norm_mod_to(h_scr, x_ref, g_ref, sc_ref, sh_ref)

    def project():
        return jnp.dot(h_scr[...], w_ref[...], preferred_element_type=F32)

    if kv_tile is None:
        o_ref[...] = project().astype(o_ref.dtype)
    else:
        @pl.when(j != kv_tile)
        def _():
            o_ref[...] = project().astype(o_ref.dtype)

        @pl.when(j == kv_tile)
        def _():
            acc = project()
            lat = acc[:, kv_off:kv_off + KV_RANK]
            ms = jnp.mean(lat * lat, axis=-1, keepdims=True)
            lat = lat * lax.rsqrt(ms + EPS) * kvg_ref[...]
            o_ref[...] = acc.astype(o_ref.dtype)
            o_ref[:, kv_off:kv_off + KV_RANK] = lat.astype(o_ref.dtype)


def _in_proj(x2, g, sc, sh, w, kv_g, seq, *, tm, tn, kv_col=None):
    n_rows, d = x2.shape
    n_out = w.shape[1]
    tiles_per_batch = seq // tm
    if kv_col is None:
        kv_tile, kv_off = None, 0
    else:
        kv_tile, kv_off = kv_col // tn, kv_col % tn
    kern = functools.partial(_in_proj_kernel, kv_tile=kv_tile, kv_off=kv_off)
    return pl.pallas_call(
        kern,
        grid=(n_rows // tm, n_out // tn),
        in_specs=[
            pl.BlockSpec((tm, d), lambda i, j: (i, 0)),
            pl.BlockSpec((1, d), lambda i, j: (0, 0)),
            pl.BlockSpec((1, 1, d), lambda i, j: (i // tiles_per_batch, 0, 0)),
            pl.BlockSpec((1, 1, d), lambda i, j: (i // tiles_per_batch, 0, 0)),
            pl.BlockSpec((d, tn), lambda i, j: (0, j)),
            pl.BlockSpec((1, KV_RANK), lambda i, j: (0, 0)),
        ],
        out_specs=pl.BlockSpec((tm, tn), lambda i, j: (i, j)),
        out_shape=jax.ShapeDtypeStruct((n_rows, n_out), BF16),
        scratch_shapes=[pltpu.VMEM((tm, d), BF16)],
        compiler_params=_cparams(("arbitrary", "arbitrary")),
        name="in_proj",
    )(x2, g, sc, sh, w, kv_g)


def _out_proj_kernel(*refs, n_act):
    acts = refs[:n_act]
    ws = refs[n_act:2 * n_act]
    x_ref, gate_ref, o_ref = refs[2 * n_act:]
    y = jnp.dot(acts[0][...], ws[0][...], preferred_element_type=F32)
    for a_ref, w_ref in zip(acts[1:], ws[1:]):
        y = y + jnp.dot(a_ref[...], w_ref[...], preferred_element_type=F32)
    o_ref[...] = x_ref[...] + gate_ref[0] * y


def _out_proj(acts, ws, x2, gate, seq, *, tm):
    n_rows, d = x2.shape
    tiles_per_batch = seq // tm
    n_act = len(acts)
    in_specs = ([pl.BlockSpec((tm, a.shape[1]), lambda i: (i, 0)) for a in acts]
                + [pl.BlockSpec(w.shape, lambda i: (0, 0)) for w in ws]
                + [pl.BlockSpec((tm, d), lambda i: (i, 0)),
                   pl.BlockSpec((1, 1, d), lambda i: (i // tiles_per_batch, 0, 0))])
    return pl.pallas_call(
        functools.partial(_out_proj_kernel, n_act=n_act),
        grid=(n_rows // tm,),
        in_specs=in_specs,
        out_specs=pl.BlockSpec((tm, d), lambda i: (i, 0)),
        out_shape=jax.ShapeDtypeStruct((n_rows, d), F32),
        compiler_params=_cparams(("arbitrary",)),
        name="out_proj",
    )(*acts, *ws, x2, gate)


FFN_RB = 64


def _ffn_kernel(x_ref, g_ref, sc_ref, sh_ref, gate_ref, wg_ref, wv_ref, cg_ref, cv_ref, wd_ref,
                wdl_ref, fg_ref, o_ref, h_scr, ug_scr, uv_scr, a0_scr, a1_scr, carry_g, carry_v, *,
                nf, tiles_per_batch, final):
    i = pl.program_id(0)
    f = pl.program_id(1)
    tm = x_ref.shape[0]

    @pl.when(f == 0)
    def _():
        _norm_mod_to(h_scr, x_ref, g_ref, sc_ref, sh_ref)
        o_ref[...] = jnp.zeros(o_ref.shape, F32)
        a1_scr[...] = jnp.zeros(a1_scr.shape, BF16)

    batch_start = (i % tiles_per_batch) == 0

    @pl.when(batch_start)
    def _():
        ug_scr[0:SUBLANES, :] = jnp.zeros((SUBLANES, ug_scr.shape[1]), F32)
        uv_scr[0:SUBLANES, :] = jnp.zeros((SUBLANES, uv_scr.shape[1]), F32)

    @pl.when(jnp.logical_not(batch_start))
    def _():
        ug_scr[0:SUBLANES, :] = carry_g[f]
        uv_scr[0:SUBLANES, :] = carry_v[f]

    def conv(u_scr, cw, base):
        y = cw[2:3] * u_scr[base:base + FFN_RB, :] + cw[3:4]
        y = y + cw[1:2] * u_scr[base - 1:base - 1 + FFN_RB, :]
        return y + cw[0:1] * u_scr[base - 2:base - 2 + FFN_RB, :]

    def step(a_cur, a_prev):
        ug_scr[SUBLANES:, :] = jnp.dot(h_scr[...], wg_ref[...], preferred_element_type=F32)
        uv_scr[SUBLANES:, :] = jnp.dot(h_scr[...], wv_ref[...], preferred_element_type=F32)
        o_ref[...] += jnp.dot(a_prev[...], wd_ref[...], preferred_element_type=F32)
        carry_g[f] = ug_scr[tm:tm + SUBLANES, :]
        carry_v[f] = uv_scr[tm:tm + SUBLANES, :]
        cwg = cg_ref[...]
        cwv = cv_ref[...]
        for r in range(tm // FFN_RB):
            base = SUBLANES + r * FFN_RB
            act = _silu(conv(ug_scr, cwg, base)) * conv(uv_scr, cwv, base)
            a_cur[r * FFN_RB:(r + 1) * FFN_RB, :] = act.astype(BF16)

    @pl.when(f % 2 == 0)
    def _():
        step(a0_scr, a1_scr)

    @pl.when(f % 2 == 1)
    def _():
        step(a1_scr, a0_scr)

    @pl.when(f == nf - 1)
    def _():
        a_last = a0_scr if (nf - 1) % 2 == 0 else a1_scr
        o_ref[...] += jnp.dot(a_last[...], wdl_ref[...], preferred_element_type=F32)

        def residual(rows):
            xo = x_ref[rows, :] + gate_ref[0] * o_ref[rows, :]
            if final:
                ms = jnp.mean(xo * xo, axis=-1, keepdims=True)
                xo = xo * lax.rsqrt(ms + EPS) * fg_ref[...]
            o_ref[rows, :] = xo
        _row_blocks(tm, residual)


def _conv_ffn(x2, g, sc, sh, gate, w_up, conv4, w_down, final_g, seq, *, tm, tf, final):
    n_rows, d = x2.shape
    d_ff = w_down.shape[0]
    nf = d_ff // tf
    tiles_per_batch = seq // tm
    kern = functools.partial(_ffn_kernel, nf=nf, tiles_per_batch=tiles_per_batch, final=final)
    return pl.pallas_call(
        kern,
        grid=(n_rows // tm, nf),
        in_specs=[
            pl.BlockSpec((tm, d), lambda i, f: (i, 0)),
            pl.BlockSpec((1, d), lambda i, f: (0, 0)),
            pl.BlockSpec((1, 1, d), lambda i, f: (i // tiles_per_batch, 0, 0)),
            pl.BlockSpec((1, 1, d), lambda i, f: (i // tiles_per_batch, 0, 0)),
            pl.BlockSpec((1, 1, d), lambda i, f: (i // tiles_per_batch, 0, 0)),
            pl.BlockSpec((d, tf), lambda i, f: (0, f)),
            pl.BlockSpec((d, tf), lambda i, f: (0, f + nf)),
            pl.BlockSpec((CONV_W + 1, tf), lambda i, f: (0, f)),
            pl.BlockSpec((CONV_W + 1, tf), lambda i, f: (0, f + nf)),
            pl.BlockSpec((tf, d), lambda i, f: (jnp.maximum(f - 1, 0), 0)),
            pl.BlockSpec((tf, d), lambda i, f: (nf - 1, 0)),
            pl.BlockSpec((1, d), lambda i, f: (0, 0)),
        ],
        out_specs=pl.BlockSpec((tm, d), lambda i, f: (i, 0)),
        out_shape=jax.ShapeDtypeStruct((n_rows, d), F32),
        scratch_shapes=[
            pltpu.VMEM((tm, d), BF16),
            pltpu.VMEM((tm + SUBLANES, tf), F32),
            pltpu.VMEM((tm + SUBLANES, tf), F32),
            pltpu.VMEM((tm, tf), BF16),
            pltpu.VMEM((tm, tf), BF16),
            pltpu.VMEM((nf, SUBLANES, tf), F32),
            pltpu.VMEM((nf, SUBLANES, tf), F32),
        ],
        compiler_params=_cparams(("arbitrary", "arbitrary")),
        name="conv_ffn",
    )(x2, g, sc, sh, gate, w_up, w_up, conv4, conv4, w_down, w_down, final_g)


def _swa_kernel(q_ref, kp_ref, ko_ref, vp_ref, vo_ref, bias_ref, sink_ref, o_ref):
    n = pl.program_id(1)
    row = lax.broadcasted_iota(I32, (QB, 2 * QB), 0)
    col = lax.broadcasted_iota(I32, (QB, 2 * QB), 1)
    qc = row // CHUNK
    kc = col // CHUNK - QB // CHUNK
    allowed = (kc <= qc) & (kc >= qc - WINDOW_CHUNKS) & ((col >= QB) | (n > 0))
    allowed_g = jnp.concatenate([allowed] * A_GROUP, axis=0)
    k_all = jnp.concatenate([kp_ref[0], ko_ref[0]], axis=0)
    v_all = jnp.concatenate([vp_ref[0], vo_ref[0]], axis=0)
    q_all = q_ref[0] * (HEAD_DIM ** -0.5)
    low = lax.broadcasted_iota(I32, (QB, LANES), 1) < HEAD_DIM
    zero = jnp.zeros((QB, LANES), q_all.dtype)
    outs = []
    for g in range(A_KV_HEADS):
        heads = range(g * A_GROUP, (g + 1) * A_GROUP)
        kg = k_all[:, g * HEAD_DIM:(g + 1) * HEAD_DIM]
        vg = v_all[:, g * HEAD_DIM:(g + 1) * HEAD_DIM]
        k2 = jnp.concatenate([kg, kg], axis=1)
        v2 = jnp.concatenate([vg, vg], axis=1)
        parts = []
        for c in range(A_GROUP // 2):
            col = (g * A_GROUP + 2 * c) * HEAD_DIM
            q2 = q_all[:, col:col + LANES]
            parts += [jnp.where(low, q2, zero), jnp.where(low, zero, q2)]
        qg = jnp.concatenate(parts, axis=0)
        s = lax.dot_general(qg, k2, (((1,), (1,)), ((), ())), preferred_element_type=F32)
        bias = jnp.concatenate([bias_ref[h] for h in heads], axis=0)
        s = jnp.where(allowed_g, s + bias, NEG_BIG)
        sink = jnp.concatenate([sink_ref[h] for h in heads], axis=0)[:, 0:1]
        m = jnp.maximum(jnp.max(s, axis=-1, keepdims=True), sink)
        p = jnp.exp(s - m)
        denom = jnp.sum(p, axis=-1, keepdims=True) + jnp.exp(sink - m)
        o = jnp.dot(p.astype(BF16), v2, preferred_element_type=F32) / denom
        for c in range(A_GROUP // 2):
            even = o[(2 * c) * QB:(2 * c + 1) * QB]
            odd = o[(2 * c + 1) * QB:(2 * c + 2) * QB]
            outs.append(jnp.where(low, even, odd))
    o_ref[0] = jnp.concatenate(outs, axis=-1).astype(o_ref.dtype)


def _swa(proj3, bias_a, sinks):
    bn, seq, _ = proj3.shape
    nblk = seq // QB
    kvw = A_KV_HEADS * HEAD_DIM
    qw = A_HEADS * HEAD_DIM
    sink_b = jnp.broadcast_to(sinks.astype(F32)[:, None, None], (A_HEADS, QB, LANES))
    prev = lambda c: (lambda b, n: (b, jnp.maximum(n - 1, 0), c))
    own = lambda c: (lambda b, n: (b, n, c))
    return pl.pallas_call(
        _swa_kernel,
        grid=(bn, nblk),
        in_specs=[
            pl.BlockSpec((1, QB, qw), own(EV_QA // qw)),
            pl.BlockSpec((1, QB, kvw), prev(EV_KA // kvw)),
            pl.BlockSpec((1, QB, kvw), own(EV_KA // kvw)),
            pl.BlockSpec((1, QB, kvw), prev(EV_VA // kvw)),
            pl.BlockSpec((1, QB, kvw), own(EV_VA // kvw)),
            pl.BlockSpec((A_HEADS, QB, 2 * QB), lambda b, n: (0, 0, 0)),
            pl.BlockSpec((A_HEADS, QB, LANES), lambda b, n: (0, 0, 0)),
        ],
        out_specs=pl.BlockSpec((1, QB, qw), lambda b, n: (b, n, 0)),
        out_shape=jax.ShapeDtypeStruct((bn, seq, qw), BF16),
        compiler_params=_cparams(("arbitrary", "arbitrary")),
        name="swa",
    )(proj3, proj3, proj3, proj3, proj3, bias_a, sink_b)


KCH = 2 * QB
DSA_MIN_DENOM = 2.0 ** -60
BF16_SLACK = 1.0 + 2.0 ** -7


def _dsa_kernel(qb_ref, qi_ref, kq_ref, ckv_ref, kk_ref, wuk_ref, wuv_ref, bias_ref, cb_ref,
                mb_ref, o_ref,
                qst_scr, wb_scr, key_scr, qlat_scr, m_scr, l_scr, acc_scr, cut_scr, *, topk):
    n = pl.program_id(1)
    nch = n // 2 + 1
    spad = key_scr.shape[1]

    def chunk_start(e):
        return pl.multiple_of((n - 2 * e) * QB, QB)

    idx_scale = IDX_DIM ** -0.5 * IDX_HEADS ** -0.5
    for h in range(IDX_HEADS):
        qst_scr[h * QB:(h + 1) * QB, :] = qi_ref[0, :, h * LANES:(h + 1) * LANES]
        w = kq_ref[0, :, IDX_DIM + h:IDX_DIM + h + 1].astype(F32) * idx_scale
        wb_scr[h] = jnp.broadcast_to(w, (QB, KCH))
    q_all = qb_ref[0]
    for h in range(B_HEADS):
        ql = jnp.dot(q_all[:, h * HEAD_DIM:(h + 1) * HEAD_DIM], wuk_ref[h],
                     preferred_element_type=F32)
        ql = (ql * (HEAD_DIM ** -0.5 * LOG2E)).astype(BF16)
        qlat_scr[h * QB:(h + 1) * QB, :] = ql
        qf = ql.astype(F32)
        qn = jnp.sqrt(jnp.sum(qf * qf, axis=-1, keepdims=True))
        m_scr[h] = jnp.broadcast_to(qn, (QB, LANES)) * cb_ref[...] + mb_ref[h]

    row = lax.broadcasted_iota(I32, (QB, KCH), 0)
    lane = lax.broadcasted_iota(I32, (QB, KCH), 1)
    limit = (n * QB + CHUNK) + (row // CHUNK) * CHUNK

    def idx_chunk(e):
        ks = chunk_start(e)
        kj = kk_ref[0, pl.ds(ks, KCH), :]
        r = lax.dot_general(qst_scr[...], kj, (((1,), (1,)), ((), ())), preferred_element_type=F32)
        sc = jnp.maximum(r[0:QB], 0.0) * wb_scr[0]
        for h in range(1, IDX_HEADS):
            sc = sc + jnp.maximum(r[h * QB:(h + 1) * QB], 0.0) * wb_scr[h]
        bits = lax.bitcast_convert_type(sc, I32)
        key = bits ^ ((bits >> 31) & 0x7FFFFFFF)
        key = jnp.where(bits == INT_MIN, 0, key)
        kpos = ks - QB + lane
        key = jnp.where((kpos >= 0) & (kpos < limit), key, INT_MIN)
        key_scr[:, pl.ds(ks, KCH)] = key

    def idx_body(i, _):
        idx_chunk(2 * i)
        idx_chunk(jnp.minimum(2 * i + 1, nch - 1))
        return 0

    lax.fori_loop(0, (nch + 1) // 2, idx_body, 0)

    def count(pred):
        def body(e, acc):
            ks = chunk_start(e)
            for half in range(KCH // LANES):
                off = pl.multiple_of(ks + half * LANES, LANES)
                k = key_scr[:, pl.ds(off, LANES)]
                acc = acc + jnp.where(pred(k, off), 1, 0)
            return acc
        c = lax.fori_loop(0, nch, body, jnp.zeros((QB, LANES), I32))
        return jnp.sum(c, axis=1, keepdims=True)

    lane1 = lax.broadcasted_iota(I32, (QB, LANES), 1)

    def bit_body(b, t):
        cand = t ^ jnp.left_shift(jnp.int32(1), 31 - b)
        cand_b = jnp.broadcast_to(cand, (QB, LANES))
        tot = count(lambda k, off: k >= cand_b)
        return jnp.where(tot >= topk, cand, t)

    thr = lax.fori_loop(0, 32, bit_body, jnp.full((QB, 1), INT_MIN, I32))
    thr = jnp.maximum(thr, INT_MIN + 1)

    thr_b = jnp.broadcast_to(thr, (QB, LANES))
    n_gt = count(lambda k, off: k > thr_b)
    n_eq = count(lambda k, off: k == thr_b)
    need = topk - n_gt
    cut_scr[...] = jnp.full((QB, LANES), 2 * spad, I32)
    excess = jnp.max(jnp.where(n_eq > need, 1, 0))

    @pl.when(excess > 0)
    def _():
        nbits = int(spad).bit_length()

        def cut_body(b, c):
            cand = c | jnp.left_shift(jnp.int32(1), nbits - 1 - b)
            cand_b = jnp.broadcast_to(cand, (QB, LANES))
            tot = count(lambda k, off: (k == thr_b) & (off + lane1 < cand_b))
            return jnp.where(tot <= need, cand, c)

        c = lax.fori_loop(0, nbits, cut_body, jnp.zeros((QB, 1), I32))
        cut_scr[...] = jnp.broadcast_to(jnp.where(n_eq > need, c, 2 * spad), (QB, LANES))

    cut = cut_scr[:, 0:1]

    def masked_scores(e, near):
        ks = chunk_start(e)
        kv = ckv_ref[0, pl.ds(ks, KCH), :]
        k = key_scr[:, pl.ds(ks, KCH)]
        sel = (k > thr) | ((k == thr) & (ks + lane < cut))
        addmask = jnp.where(sel, 0.0, NEG_BIG)
        s = lax.dot_general(qlat_scr[...], kv, (((1,), (1,)), ((), ())), preferred_element_type=F32)
        s = s.reshape(B_HEADS, QB, KCH) + addmask[None]
        if near:
            s = s + bias_ref[...]
        return s, kv

    def max_pass(e, near):
        s, _ = masked_scores(e, near)
        m_scr[...] = jnp.maximum(m_scr[...], jnp.maximum(s[:, :, :LANES], s[:, :, LANES:]))

    def sum_pass(e, near):
        s, kv = masked_scores(e, near)
        m = m_scr[...]
        p_lo = jnp.exp2(s[:, :, :LANES] - m)
        p_hi = jnp.exp2(s[:, :, LANES:] - m)
        l_scr[...] += p_lo + p_hi
        p = jnp.concatenate([p_lo, p_hi], axis=-1).astype(BF16).reshape(B_HEADS * QB, KCH)
        acc_scr[...] += jnp.dot(p, kv, preferred_element_type=F32)

    def sweep(fn):
        def far_body(e, _):
            fn(e, False)
            return 0
        lax.fori_loop(1, nch, far_body, 0)
        fn(0, True)

    def accumulate():
        l_scr[...] = jnp.zeros(l_scr.shape, F32)
        acc_scr[...] = jnp.zeros(acc_scr.shape, F32)
        sweep(sum_pass)

    accumulate()
    smallest = jnp.min(jnp.sum(l_scr[...], axis=-1, keepdims=True))

    @pl.when(jnp.logical_not(smallest >= DSA_MIN_DENOM))
    def _():
        m_scr[...] = jnp.full(m_scr.shape, NEG_BIG, F32)
        sweep(max_pass)
        m_scr[...] = jnp.broadcast_to(jnp.max(m_scr[...], axis=-1, keepdims=True), m_scr.shape)
        accumulate()

    outs = []
    for h in range(B_HEADS):
        l = jnp.sum(l_scr[h], axis=-1, keepdims=True)
        o_lat = (acc_scr[h * QB:(h + 1) * QB, :] / l).astype(BF16)
        outs.append(jnp.dot(o_lat, wuv_ref[h], preferred_element_type=F32))
    o_ref[0] = jnp.concatenate(outs, axis=-1).astype(o_ref.dtype)


def _dsa(proj3, ckv_pad, kk_pad, w_uk, w_uv, bias_b, kv_norm_g):
    bn, seq, _ = proj3.shape
    nblk = seq // QB
    spad = seq + QB
    qw = B_HEADS * HEAD_DIM
    iw = IDX_HEADS * LANES
    key_norm = jnp.max(jnp.abs(kv_norm_g.astype(F32))) * (KV_RANK ** 0.5 * BF16_SLACK)
    cb = jnp.broadcast_to(key_norm, (1, LANES))
    mb = jnp.broadcast_to(jnp.maximum(jnp.max(bias_b, axis=(1, 2)), 0.0)[:, None, None],
                          (B_HEADS, 1, LANES))
    return pl.pallas_call(
        functools.partial(_dsa_kernel, topk=min(TOPK_MAX, seq // 4)),
        grid=(bn, nblk),
        in_specs=[
            pl.BlockSpec((1, QB, qw), lambda b, n: (b, n, EV_QB // qw)),
            pl.BlockSpec((1, QB, iw), lambda b, n: (b, n, EV_QI // iw)),
            pl.BlockSpec((1, QB, LANES), lambda b, n: (b, n, EV_KW // LANES)),
            pl.BlockSpec((1, spad, KV_RANK), lambda b, n: (b, 0, 0)),
            pl.BlockSpec((1, spad, LANES), lambda b, n: (b, 0, 0)),
            pl.BlockSpec(w_uk.shape, lambda b, n: (0, 0, 0)),
            pl.BlockSpec(w_uv.shape, lambda b, n: (0, 0, 0)),
            pl.BlockSpec((B_HEADS, QB, KCH), lambda b, n: (0, 0, 0)),
            pl.BlockSpec((1, LANES), lambda b, n: (0, 0)),
            pl.BlockSpec((B_HEADS, 1, LANES), lambda b, n: (0, 0, 0)),
        ],
        out_specs=pl.BlockSpec((1, QB, qw), lambda b, n: (b, n, 0)),
        out_shape=jax.ShapeDtypeStruct((bn, seq, qw), BF16),
        scratch_shapes=[
            pltpu.VMEM((IDX_HEADS * QB, LANES), BF16),
            pltpu.VMEM((IDX_HEADS, QB, KCH), F32),
            pltpu.VMEM((QB, spad), I32),
            pltpu.VMEM((B_HEADS * QB, KV_RANK), BF16),
            pltpu.VMEM((B_HEADS, QB, LANES), F32),
            pltpu.VMEM((B_HEADS, QB, LANES), F32),
            pltpu.VMEM((B_HEADS * QB, KV_RANK), F32),
            pltpu.VMEM((QB, LANES), I32),
        ],
        compiler_params=_cparams(("arbitrary", "arbitrary")),
        name="dsa",
    )(proj3, proj3, proj3, ckv_pad, kk_pad, w_uk, w_uv, bias_b, cb, mb)


SB_HG = 8


def _sb_kernel(q_ref, k_ref, v_ref, o_ref, acc_scr):
    n = pl.program_id(2)
    scale = C_HEAD_DIM ** -0.5
    rows = SB_HG * QB
    row = lax.broadcasted_iota(I32, (QB, QB), 0)
    col = lax.broadcasted_iota(I32, (QB, QB), 1)
    suffix = jnp.where(row > col, 1.0, 0.0).astype(BF16)
    suffix2 = jnp.concatenate([suffix, suffix], axis=0)
    earlier = lax.broadcasted_iota(I32, (rows, QB), 1) < (lax.broadcasted_iota(I32, (rows, QB), 0) % QB)

    def head_cols(h):
        return slice(h * C_HEAD_DIM, (h + 1) * C_HEAD_DIM)

    def block(j, carry, diagonal):
        ks = pl.multiple_of(j * QB, QB)
        z = jnp.concatenate(
            [lax.dot_general(q_ref[0, :, head_cols(h)], k_ref[0, pl.ds(ks, QB), head_cols(h)],
                             (((1,), (1,)), ((), ())), preferred_element_type=F32)
             for h in range(SB_HG)], axis=0) * scale
        sp = jnp.maximum(z, 0.0) + jnp.log(1.0 + jnp.exp(-jnp.abs(z)))
        lk = jnp.where(earlier, -sp, 0.0) if diagonal else -sp
        hi = lk.astype(BF16)
        lo = (lk - hi.astype(F32)).astype(BF16)
        inner = jnp.dot(jnp.concatenate([hi, lo], axis=1), suffix2, preferred_element_type=F32)
        a = jnp.exp((z - sp) + (carry + inner))
        if diagonal:
            a = jnp.where(earlier, a, 0.0)
        a = a.astype(BF16)
        for h in range(SB_HG):
            r = slice(h * QB, (h + 1) * QB)
            acc_scr[r, :] += jnp.dot(a[r], v_ref[0, pl.ds(ks, QB), head_cols(h)],
                                     preferred_element_type=F32)
        return carry + jnp.sum(lk, axis=-1, keepdims=True)

    def all_done(carry):
        return (jnp.max(carry) < SB_DONE).astype(I32)

    acc_scr[...] = jnp.zeros(acc_scr.shape, F32)
    carry0 = block(n, jnp.zeros((rows, 1), F32), True)

    def cond(state):
        j, done, _ = state
        return (j >= 0) & (done == 0)

    def body(state):
        j, _, carry = state
        carry = block(j, carry, False)
        return j - 1, all_done(carry), carry

    lax.while_loop(cond, body, (n - 1, all_done(carry0), carry0))
    o_ref[0] = jnp.concatenate([acc_scr[h * QB:(h + 1) * QB, :] for h in range(SB_HG)],
                               axis=-1).astype(o_ref.dtype)


def _sb(qkv3):
    bn, seq, _ = qkv3.shape
    nblk = seq // QB
    gw = SB_HG * C_HEAD_DIM
    ngrp = C_HEADS // SB_HG
    return pl.pallas_call(
        _sb_kernel,
        grid=(bn, ngrp, nblk),
        in_specs=[
            pl.BlockSpec((1, QB, gw), lambda b, g, n: (b, n, g)),
            pl.BlockSpec((1, seq, gw), lambda b, g, n: (b, 0, ngrp + g)),
            pl.BlockSpec((1, seq, gw), lambda b, g, n: (b, 0, 2 * ngrp + g)),
        ],
        out_specs=pl.BlockSpec((1, QB, gw), lambda b, g, n: (b, n, g)),
        out_shape=jax.ShapeDtypeStruct((bn, seq, C_HEADS * C_HEAD_DIM), BF16),
        scratch_shapes=[pltpu.VMEM((SB_HG * QB, C_HEAD_DIM), F32)],
        compiler_params=_cparams(("arbitrary", "arbitrary", "arbitrary")),
        name="stick_breaking",
    )(qkv3, qkv3, qkv3)


def _t5_bucket(rel):
    half = NUM_BUCKETS // 2
    max_exact = half // 2
    n = jnp.abs(rel)
    nf = jnp.maximum(n, 1).astype(F32)
    large = max_exact + (jnp.log(nf / max_exact) / math.log(MAX_DISTANCE / max_exact)
                         * (half - max_exact)).astype(I32)
    large = jnp.minimum(large, half - 1)
    return jnp.where(rel > 0, half, 0) + jnp.where(n < max_exact, n, large)


def _band_bias(rel_bias):
    rel = (jnp.arange(2 * QB) - QB)[None, :] - jnp.arange(QB)[:, None]
    table = rel_bias.astype(F32)
    bucket = _t5_bucket(rel)[None]
    band = sum(jnp.where(bucket == k, table[k][:, None, None], 0.0) for k in range(NUM_BUCKETS))
    far = table[_t5_bucket(jnp.int32(-(QB + 1)))]
    return band, far


def _even_w_in(w_in):
    d = w_in.shape[0]
    sizes = (A_HEADS * HEAD_DIM, A_KV_HEADS * HEAD_DIM, A_KV_HEADS * HEAD_DIM,
             B_HEADS * HEAD_DIM, KV_RANK, IDX_HEADS * IDX_DIM, IDX_DIM, IDX_HEADS)
    offs = [int(o) for o in np.cumsum((0,) + sizes)]
    w = w_in.astype(BF16)
    moves = [(offs[0], offs[1], EV_QA), (offs[1], offs[2], EV_KA), (offs[2], offs[3], EV_VA),
             (offs[3], offs[4], EV_QB), (offs[4], offs[5], EV_CL), (offs[6], offs[8], EV_KW)]
    moves += [(offs[5] + h * IDX_DIM, offs[5] + (h + 1) * IDX_DIM, EV_QI + h * LANES)
              for h in range(IDX_HEADS)]
    out = jnp.zeros((d, EV_WIDTH), BF16)
    for lo, hi, dst in moves:
        out = lax.dynamic_update_slice(out, w[:, lo:hi], (0, dst))
    return out


def kernel(x, c, rel_bias, ada_w, ada_b, norm_mix_g, norm_ffn_g, ev_w_in, ev_kv_norm_g, ev_w_uk,
           ev_w_uv, ev_sinks, ev_w_out, od_w_in, od_w_out, ffn_w_up, ffn_conv_w, ffn_conv_b,
           ffn_w_down, final_g):
    bn, seq, d = x.shape
    depth = ada_w.shape[0]
    x2 = x.reshape(bn * seq, d)

    mod = _ada_mod(c, ada_w, ada_b)
    band, far = _band_bias(rel_bias)
    bias_a = band[:A_HEADS]
    bias_b = (band[A_HEADS:] - far[A_HEADS:, None, None]) * LOG2E

    for i in range(depth):
        sh1, sc1, g1, sh2, sc2, g2 = [m.reshape(bn, 1, d) for m in jnp.split(mod[i], 6, axis=-1)]
        g_mix = norm_mix_g[i].reshape(1, d)
        j = i // 2
        if i % 2 == 0:
            proj = _in_proj(x2, g_mix, sc1, sh1, _even_w_in(ev_w_in[j]),
                            ev_kv_norm_g[j].reshape(1, KV_RANK), seq, tm=1024, tn=1024, kv_col=EV_CL)
            proj3 = proj.reshape(bn, seq, EV_WIDTH)
            lead = ((0, 0), (QB, 0), (0, 0))
            ckv_pad = jnp.pad(proj3[:, :, EV_CL:EV_CL + KV_RANK], lead)
            kk_pad = jnp.pad(proj3[:, :, EV_KW:EV_KW + LANES], lead)
            o_a = _swa(proj3, bias_a, ev_sinks[j])
            o_b = _dsa(proj3, ckv_pad, kk_pad, ev_w_uk[j].astype(BF16), ev_w_uv[j].astype(BF16), bias_b,
                       ev_kv_norm_g[j])
            w_out = ev_w_out[j].astype(BF16)
            wa = w_out[:A_HEADS * HEAD_DIM]
            wb = w_out[A_HEADS * HEAD_DIM:]
            x2 = _out_proj([o_a.reshape(bn * seq, -1), o_b.reshape(bn * seq, -1)], [wa, wb],
                           x2, g1, seq, tm=512)
        else:
            qkv = _in_proj(x2, g_mix, sc1, sh1, od_w_in[j].astype(BF16),
                           jnp.ones((1, KV_RANK), F32), seq, tm=1024, tn=1024)
            o_c = _sb(qkv.reshape(bn, seq, -1))
            x2 = _out_proj([o_c.reshape(bn * seq, -1)], [od_w_out[j].astype(BF16)], x2, g1, seq, tm=512)
        conv4 = jnp.concatenate([ffn_conv_w[i], ffn_conv_b[i][None]], axis=0)
        x2 = _conv_ffn(x2, norm_ffn_g[i].reshape(1, d), sc2, sh2, g2, ffn_w_up[i].astype(BF16), conv4,
                       ffn_w_down[i].astype(BF16), final_g.reshape(1, d), seq,
                       tm=512, tf=512, final=(i == depth - 1))
    return x2.reshape(bn, seq, d)
```

```python
import functools
import math

import jax
import jax.numpy as jnp
import numpy as np
from jax import lax
from jax.experimental import pallas as pl
from jax.experimental.pallas import tpu as pltpu

F32 = jnp.float32
BF16 = jnp.bfloat16
I32 = jnp.int32

CHUNK = 64
QB = 128
HEAD_DIM = 64
A_HEADS = 16
A_KV_HEADS = 4
A_GROUP = A_HEADS // A_KV_HEADS
WINDOW_CHUNKS = 2
B_HEADS = 16
KV_RANK = 256
IDX_HEADS = 8
IDX_DIM = 64
TOPK_MAX = 256
C_HEADS = 16
C_HEAD_DIM = 128
NUM_BUCKETS = 32
MAX_DISTANCE = 128
CONV_W = 3
EPS = 1e-6

LANES = 128
SUBLANES = 8
MXU_DIM = 256
VMEM_LIMIT = 56 * 1024 * 1024

NEG_BIG = -1e30
LOG2E = math.log2(math.e)
INT_MIN = -(2 ** 31)
SB_DONE = -88.0

EV_QA = 0
EV_QB = EV_QA + A_HEADS * HEAD_DIM
EV_QI = EV_QB + B_HEADS * HEAD_DIM
EV_KA = EV_QI + IDX_HEADS * LANES
EV_VA = EV_KA + A_KV_HEADS * HEAD_DIM
EV_CL = EV_VA + A_KV_HEADS * HEAD_DIM
EV_KW = EV_CL + KV_RANK
EV_WIDTH = 4096


def _cparams(sem):
    return pltpu.CompilerParams(dimension_semantics=sem, vmem_limit_bytes=VMEM_LIMIT)


NORM_RB = 16
NORM_UNROLL = 8


def _row_blocks(n_rows, fn):
    def body(r, _):
        fn(pl.ds(pl.multiple_of(r * NORM_RB, NORM_RB), NORM_RB))
        return 0
    lax.fori_loop(0, n_rows // NORM_RB, body, 0, unroll=NORM_UNROLL)


def _norm_mod_to(h_scr, x_ref, g_ref, sc_ref, sh_ref):
    def block(rows):
        x = x_ref[rows, :]
        ms = jnp.mean(x * x, axis=-1, keepdims=True)
        y = x * lax.rsqrt(ms + EPS)
        h_scr[rows, :] = ((y * g_ref[...]) * (1.0 + sc_ref[0]) + sh_ref[0]).astype(BF16)
    _row_blocks(x_ref.shape[0], block)


def _silu(x):
    return x / (1.0 + jnp.exp(-x))


def _ada_kernel(c_ref, w_ref, b_ref, o_ref):
    a = _silu(c_ref[...]).astype(BF16)
    o_ref[0] = jnp.dot(a, w_ref[0].astype(BF16), preferred_element_type=F32) + b_ref[0]


def _ada_mod(c, ada_w, ada_b):
    depth, d, n = ada_w.shape
    bn = c.shape[0]
    rows = -(-bn // SUBLANES) * SUBLANES
    c_pad = jnp.pad(c, ((0, rows - bn), (0, 0)))
    tn = 1024
    out = pl.pallas_call(
        _ada_kernel,
        grid=(depth, n // tn),
        in_specs=[
            pl.BlockSpec((rows, d), lambda l, j: (0, 0)),
            pl.BlockSpec((1, d, tn), lambda l, j: (l, 0, j)),
            pl.BlockSpec((1, 1, tn), lambda l, j: (l, 0, j)),
        ],
        out_specs=pl.BlockSpec((1, rows, tn), lambda l, j: (l, 0, j)),
        out_shape=jax.ShapeDtypeStruct((depth, rows, n), F32),
        compiler_params=_cparams(("arbitrary", "arbitrary")),
        name="ada_mod",
    )(c_pad, ada_w, ada_b.reshape(depth, 1, n))
    return out[:, :bn]


def _in_proj_kernel(x_ref, g_ref, sc_ref, sh_ref, w_ref, kvg_ref, o_ref, h_scr, *, kv_tile, kv_off):
    j = pl.program_id(1)

    @pl.when(j == 0)
    def _():
        _norm_mod_to(h_scr, x_ref, g_ref, sc_ref, sh_ref)

    def project():
        return jnp.dot(h_scr[...], w_ref[...], preferred_element_type=F32)

    if kv_tile is None:
        o_ref[...] = project().astype(o_ref.dtype)
    else:
        @pl.when(j != kv_tile)
        def _():
            o_ref[...] = project().astype(o_ref.dtype)

        @pl.when(j == kv_tile)
        def _():
            acc = project()
            lat = acc[:, kv_off:kv_off + KV_RANK]
            ms = jnp.mean(lat * lat, axis=-1, keepdims=True)
            lat = lat * lax.rsqrt(ms + EPS) * kvg_ref[...]
            o_ref[...] = acc.astype(o_ref.dtype)
            o_ref[:, kv_off:kv_off + KV_RANK] = lat.astype(o_ref.dtype)


def _in_proj(x2, g, sc, sh, w, kv_g, seq, *, tm, tn, kv_col=None):
    n_rows, d = x2.shape
    n_out = w.shape[1]
    tiles_per_batch = seq // tm
    if kv_col is None:
        kv_tile, kv_off = None, 0
    else:
        kv_tile, kv_off = kv_col // tn, kv_col % tn
    kern = functools.partial(_in_proj_kernel, kv_tile=kv_tile, kv_off=kv_off)
    return pl.pallas_call(
        kern,
        grid=(n_rows // tm, n_out // tn),
        in_specs=[
            pl.BlockSpec((tm, d), lambda i, j: (i, 0)),
            pl.BlockSpec((1, d), lambda i, j: (0, 0)),
            pl.BlockSpec((1, 1, d), lambda i, j: (i // tiles_per_batch, 0, 0)),
            pl.BlockSpec((1, 1, d), lambda i, j: (i // tiles_per_batch, 0, 0)),
            pl.BlockSpec((d, tn), lambda i, j: (0, j)),
            pl.BlockSpec((1, KV_RANK), lambda i, j: (0, 0)),
        ],
        out_specs=pl.BlockSpec((tm, tn), lambda i, j: (i, j)),
        out_shape=jax.ShapeDtypeStruct((n_rows, n_out), BF16),
        scratch_shapes=[pltpu.VMEM((tm, d), BF16)],
        compiler_params=_cparams(("arbitrary", "arbitrary")),
        name="in_proj",
    )(x2, g, sc, sh, w, kv_g)


def _out_proj_kernel(*refs, n_act):
    acts = refs[:n_act]
    ws = refs[n_act:2 * n_act]
    x_ref, gate_ref, o_ref = refs[2 * n_act:]
    y = jnp.dot(acts[0][...], ws[0][...], preferred_element_type=F32)
    for a_ref, w_ref in zip(acts[1:], ws[1:]):
        y = y + jnp.dot(a_ref[...], w_ref[...], preferred_element_type=F32)
    o_ref[...] = x_ref[...] + gate_ref[0] * y


def _out_proj(acts, ws, x2, gate, seq, *, tm):
    n_rows, d = x2.shape
    tiles_per_batch = seq // tm
    n_act = len(acts)
    in_specs = ([pl.BlockSpec((tm, a.shape[1]), lambda i: (i, 0)) for a in acts]
                + [pl.BlockSpec(w.shape, lambda i: (0, 0)) for w in ws]
                + [pl.BlockSpec((tm, d), lambda i: (i, 0)),
                   pl.BlockSpec((1, 1, d), lambda i: (i // tiles_per_batch, 0, 0))])
    return pl.pallas_call(
        functools.partial(_out_proj_kernel, n_act=n_act),
        grid=(n_rows // tm,),
        in_specs=in_specs,
        out_specs=pl.BlockSpec((tm, d), lambda i: (i, 0)),
        out_shape=jax.ShapeDtypeStruct((n_rows, d), F32),
        compiler_params=_cparams(("arbitrary",)),
        name="out_proj",
    )(*acts, *ws, x2, gate)


FFN_RB = 64


def _ffn_kernel(x_ref, g_ref, sc_ref, sh_ref, gate_ref, wg_ref, wv_ref, cg_ref, cv_ref, wd_ref,
                wdl_ref, fg_ref, o_ref, h_scr, ug_scr, uv_scr, a0_scr, a1_scr, carry_g, carry_v, *,
                nf, tiles_per_batch, final):
    i = pl.program_id(0)
    f = pl.program_id(1)
    tm = x_ref.shape[0]

    @pl.when(f == 0)
    def _():
        _norm_mod_to(h_scr, x_ref, g_ref, sc_ref, sh_ref)
        o_ref[...] = jnp.zeros(o_ref.shape, F32)
        a1_scr[...] = jnp.zeros(a1_scr.shape, BF16)

    batch_start = (i % tiles_per_batch) == 0

    @pl.when(batch_start)
    def _():
        ug_scr[0:SUBLANES, :] = jnp.zeros((SUBLANES, ug_scr.shape[1]), F32)
        uv_scr[0:SUBLANES, :] = jnp.zeros((SUBLANES, uv_scr.shape[1]), F32)

    @pl.when(jnp.logical_not(batch_start))
    def _():
        ug_scr[0:SUBLANES, :] = carry_g[f]
        uv_scr[0:SUBLANES, :] = carry_v[f]

    def conv(u_scr, cw, base):
        y = cw[2:3] * u_scr[base:base + FFN_RB, :] + cw[3:4]
        y = y + cw[1:2] * u_scr[base - 1:base - 1 + FFN_RB, :]
        return y + cw[0:1] * u_scr[base - 2:base - 2 + FFN_RB, :]

    def step(a_cur, a_prev):
        ug_scr[SUBLANES:, :] = jnp.dot(h_scr[...], wg_ref[...], preferred_element_type=F32)
        uv_scr[SUBLANES:, :] = jnp.dot(h_scr[...], wv_ref[...], preferred_element_type=F32)
        o_ref[...] += jnp.dot(a_prev[...], wd_ref[...], preferred_element_type=F32)
        carry_g[f] = ug_scr[tm:tm + SUBLANES, :]
        carry_v[f] = uv_scr[tm:tm + SUBLANES, :]
        cwg = cg_ref[...]
        cwv = cv_ref[...]
        for r in range(tm // FFN_RB):
            base = SUBLANES + r * FFN_RB
            act = _silu(conv(ug_scr, cwg, base)) * conv(uv_scr, cwv, base)
            a_cur[r * FFN_RB:(r + 1) * FFN_RB, :] = act.astype(BF16)

    @pl.when(f % 2 == 0)
    def _():
        step(a0_scr, a1_scr)

    @pl.when(f % 2 == 1)
    def _():
        step(a1_scr, a0_scr)

    @pl.when(f == nf - 1)
    def _():
        a_last = a0_scr if (nf - 1) % 2 == 0 else a1_scr
        o_ref[...] += jnp.dot(a_last[...], wdl_ref[...], preferred_element_type=F32)

        def residual(rows):
            xo = x_ref[rows, :] + gate_ref[0] * o_ref[rows, :]
            if final:
                ms = jnp.mean(xo * xo, axis=-1, keepdims=True)
                xo = xo * lax.rsqrt(ms + EPS) * fg_ref[...]
            o_ref[rows, :] = xo
        _row_blocks(tm, residual)


def _conv_ffn(x2, g, sc, sh, gate, w_up, conv4, w_down, final_g, seq, *, tm, tf, final):
    n_rows, d = x2.shape
    d_ff = w_down.shape[0]
    nf = d_ff // tf
    tiles_per_batch = seq // tm
    kern = functools.partial(_ffn_kernel, nf=nf, tiles_per_batch=tiles_per_batch, final=final)
    return pl.pallas_call(
        kern,
        grid=(n_rows // tm, nf),
        in_specs=[
            pl.BlockSpec((tm, d), lambda i, f: (i, 0)),
            pl.BlockSpec((1, d), lambda i, f: (0, 0)),
            pl.BlockSpec((1, 1, d), lambda i, f: (i // tiles_per_batch, 0, 0)),
            pl.BlockSpec((1, 1, d), lambda i, f: (i // tiles_per_batch, 0, 0)),
            pl.BlockSpec((1, 1, d), lambda i, f: (i // tiles_per_batch, 0, 0)),
            pl.BlockSpec((d, tf), lambda i, f: (0, f)),
            pl.BlockSpec((d, tf), lambda i, f: (0, f + nf)),
            pl.BlockSpec((CONV_W + 1, tf), lambda i, f: (0, f)),
            pl.BlockSpec((CONV_W + 1, tf), lambda i, f: (0, f + nf)),
            pl.BlockSpec((tf, d), lambda i, f: (jnp.maximum(f - 1, 0), 0)),
            pl.BlockSpec((tf, d), lambda i, f: (nf - 1, 0)),
            pl.BlockSpec((1, d), lambda i, f: (0, 0)),
        ],
        out_specs=pl.BlockSpec((tm, d), lambda i, f: (i, 0)),
        out_shape=jax.ShapeDtypeStruct((n_rows, d), F32),
        scratch_shapes=[
            pltpu.VMEM((tm, d), BF16),
            pltpu.VMEM((tm + SUBLANES, tf), F32),
            pltpu.VMEM((tm + SUBLANES, tf), F32),
            pltpu.VMEM((tm, tf), BF16),
            pltpu.VMEM((tm, tf), BF16),
            pltpu.VMEM((nf, SUBLANES, tf), F32),
            pltpu.VMEM((nf, SUBLANES, tf), F32),
        ],
        compiler_params=_cparams(("arbitrary", "arbitrary")),
        name="conv_ffn",
    )(x2, g, sc, sh, gate, w_up, w_up, conv4, conv4, w_down, w_down, final_g)


def _swa_kernel(q_ref, kp_ref, ko_ref, vp_ref, vo_ref, bias_ref, sink_ref, o_ref):
    n = pl.program_id(1)
    row = lax.broadcasted_iota(I32, (QB, 2 * QB), 0)
    col = lax.broadcasted_iota(I32, (QB, 2 * QB), 1)
    qc = row // CHUNK
    kc = col // CHUNK - QB // CHUNK
    allowed = (kc <= qc) & (kc >= qc - WINDOW_CHUNKS) & ((col >= QB) | (n > 0))
    allowed_g = jnp.concatenate([allowed] * A_GROUP, axis=0)
    k_all = jnp.concatenate([kp_ref[0], ko_ref[0]], axis=0)
    v_all = jnp.concatenate([vp_ref[0], vo_ref[0]], axis=0)
    q_all = q_ref[0] * (HEAD_DIM ** -0.5)
    low = lax.broadcasted_iota(I32, (QB, LANES), 1) < HEAD_DIM
    zero = jnp.zeros((QB, LANES), q_all.dtype)
    outs = []
    for g in range(A_KV_HEADS):
        heads = range(g * A_GROUP, (g + 1) * A_GROUP)
        kg = k_all[:, g * HEAD_DIM:(g + 1) * HEAD_DIM]
        vg = v_all[:, g * HEAD_DIM:(g + 1) * HEAD_DIM]
        k2 = jnp.concatenate([kg, kg], axis=1)
        v2 = jnp.concatenate([vg, vg], axis=1)
        parts = []
        for c in range(A_GROUP // 2):
            col = (g * A_GROUP + 2 * c) * HEAD_DIM
            q2 = q_all[:, col:col + LANES]
            parts += [jnp.where(low, q2, zero), jnp.where(low, zero, q2)]
        qg = jnp.concatenate(parts, axis=0)
        s = lax.dot_general(qg, k2, (((1,), (1,)), ((), ())), preferred_element_type=F32)
        bias = jnp.concatenate([bias_ref[h] for h in heads], axis=0)
        s = jnp.where(allowed_g, s + bias, NEG_BIG)
        sink = jnp.concatenate([sink_ref[h] for h in heads], axis=0)[:, 0:1]
        m = jnp.maximum(jnp.max(s, axis=-1, keepdims=True), sink)
        p = jnp.exp(s - m)
        denom = jnp.sum(p, axis=-1, keepdims=True) + jnp.exp(sink - m)
        o = jnp.dot(p.astype(BF16), v2, preferred_element_type=F32) / denom
        for c in range(A_GROUP // 2):
            even = o[(2 * c) * QB:(2 * c + 1) * QB]
            odd = o[(2 * c + 1) * QB:(2 * c + 2) * QB]
            outs.append(jnp.where(low, even, odd))
    o_ref[0] = jnp.concatenate(outs, axis=-1).astype(o_ref.dtype)


def _swa(proj3, bias_a, sinks):
    bn, seq, _ = proj3.shape
    nblk = seq // QB
    kvw = A_KV_HEADS * HEAD_DIM
    qw = A_HEADS * HEAD_DIM
    sink_b = jnp.broadcast_to(sinks.astype(F32)[:, None, None], (A_HEADS, QB, LANES))
    prev = lambda c: (lambda b, n: (b, jnp.maximum(n - 1, 0), c))
    own = lambda c: (lambda b, n: (b, n, c))
    return pl.pallas_call(
        _swa_kernel,
        grid=(bn, nblk),
        in_specs=[
            pl.BlockSpec((1, QB, qw), own(EV_QA // qw)),
            pl.BlockSpec((1, QB, kvw), prev(EV_KA // kvw)),
            pl.BlockSpec((1, QB, kvw), own(EV_KA // kvw)),
            pl.BlockSpec((1, QB, kvw), prev(EV_VA // kvw)),
            pl.BlockSpec((1, QB, kvw), own(EV_VA // kvw)),
            pl.BlockSpec((A_HEADS, QB, 2 * QB), lambda b, n: (0, 0, 0)),
            pl.BlockSpec((A_HEADS, QB, LANES), lambda b, n: (0, 0, 0)),
        ],
        out_specs=pl.BlockSpec((1, QB, qw), lambda b, n: (b, n, 0)),
        out_shape=jax.ShapeDtypeStruct((bn, seq, qw), BF16),
        compiler_params=_cparams(("arbitrary", "arbitrary")),
        name="swa",
    )(proj3, proj3, proj3, proj3, proj3, bias_a, sink_b)


KCH = 2 * QB
DSA_MIN_DENOM = 2.0 ** -60
BF16_SLACK = 1.0 + 2.0 ** -7


def _dsa_kernel(qb_ref, qi_ref, kq_ref, ckv_ref, kk_ref, wuk_ref, wuv_ref, bias_ref, cb_ref,
                mb_ref, o_ref,
                qst_scr, wrow_scr, key_scr, mask_scr, qlat_scr, m_scr, l_scr, acc_scr, cut_scr,
                ckv_scr, kk_scr, *, topk):
    n = pl.program_id(1)
    nch = n // 2 + 1
    spad = key_scr.shape[0]

    @pl.when(n == 0)
    def _():
        ckv_scr[0:QB, :] = jnp.zeros((QB, ckv_scr.shape[1]), ckv_scr.dtype)
        kk_scr[0:QB, :] = jnp.zeros((QB, kk_scr.shape[1]), kk_scr.dtype)
        ckv_scr[QB:, :] = ckv_ref[0]
        kk_scr[QB:, :] = kk_ref[0]

    def chunk_start(e):
        return pl.multiple_of((n - 2 * e) * QB, QB)

    idx_scale = IDX_DIM ** -0.5 * IDX_HEADS ** -0.5
    w_t = jnp.transpose(kq_ref[0].astype(F32)) * idx_scale
    for h in range(IDX_HEADS):
        qst_scr[h * QB:(h + 1) * QB, :] = qi_ref[0, :, h * LANES:(h + 1) * LANES]
        wrow_scr[h] = jnp.broadcast_to(w_t[IDX_DIM + h:IDX_DIM + h + 1, :], (SUBLANES, LANES))
    q_all = qb_ref[0]
    for h in range(B_HEADS):
        ql = jnp.dot(q_all[:, h * HEAD_DIM:(h + 1) * HEAD_DIM], wuk_ref[h],
                     preferred_element_type=F32)
        ql = (ql * (HEAD_DIM ** -0.5 * LOG2E)).astype(BF16)
        qlat_scr[h * QB:(h + 1) * QB, :] = ql
        qf = ql.astype(F32)
        qn = jnp.sqrt(jnp.sum(qf * qf, axis=-1, keepdims=True))
        m_scr[h] = jnp.broadcast_to(qn, (QB, LANES)) * cb_ref[...] + mb_ref[h]

    key_row = lax.broadcasted_iota(I32, (KCH, LANES), 0)
    limit = (n * QB + CHUNK) + (lax.broadcasted_iota(I32, (1, LANES), 1) // CHUNK) * CHUNK

    def idx_chunk(e):
        ks = chunk_start(e)
        kj = kk_scr[pl.ds(ks, KCH), :]
        r = lax.dot_general(kj, qst_scr[...], (((1,), (1,)), ((), ())), preferred_element_type=F32)
        sc = jnp.maximum(r[:, 0:LANES], 0.0) * wrow_scr[0][0:1, :]
        for h in range(1, IDX_HEADS):
            sc = sc + jnp.maximum(r[:, h * LANES:(h + 1) * LANES], 0.0) * wrow_scr[h][0:1, :]
        bits = lax.bitcast_convert_type(sc, I32)
        key = bits ^ ((bits >> 31) & 0x7FFFFFFF)
        key = jnp.where(bits == INT_MIN, 0, key)
        kpos = ks - QB + key_row
        key = jnp.where((kpos >= 0) & (kpos < limit), key, INT_MIN)
        key_scr[pl.ds(ks, KCH), :] = key

    def idx_body(i, _):
        idx_chunk(2 * i)
        idx_chunk(jnp.minimum(2 * i + 1, nch - 1))
        return 0

    lax.fori_loop(0, (nch + 1) // 2, idx_body, 0)

    def count(pred):
        def hits(e):
            ks = chunk_start(e)
            hit = jnp.where(pred(key_scr[pl.ds(ks, KCH), :], ks + key_row), 1, 0)
            return jnp.sum(hit.reshape(KCH // SUBLANES, SUBLANES, LANES), axis=0)

        def pair(i, acc):
            return acc + (hits(2 * i) + hits(2 * i + 1))

        c = lax.fori_loop(0, nch // 2, pair, jnp.zeros((SUBLANES, LANES), I32))
        c = c + jnp.where(nch % 2 == 1, hits(nch - 1), 0)
        return jnp.sum(c, axis=0, keepdims=True)

    def bit_body(b, t):
        cand = t ^ jnp.left_shift(jnp.int32(1), 31 - b)
        tot = count(lambda k, pos: k >= cand)
        return jnp.where(tot >= topk, cand, t)

    thr = lax.fori_loop(0, 32, bit_body, jnp.full((1, LANES), INT_MIN, I32))
    thr = jnp.maximum(thr, INT_MIN + 1)

    n_gt = count(lambda k, pos: k > thr)
    n_eq = count(lambda k, pos: k == thr)
    need = topk - n_gt
    cut_scr[...] = jnp.full(cut_scr.shape, 2 * spad, I32)
    excess = jnp.max(jnp.where(n_eq > need, 1, 0))

    @pl.when(excess > 0)
    def _():
        nbits = int(spad).bit_length()

        def cut_body(b, c):
            cand = c | jnp.left_shift(jnp.int32(1), nbits - 1 - b)
            tot = count(lambda k, pos: (k == thr) & (pos < cand))
            return jnp.where(tot <= need, cand, c)

        c = lax.fori_loop(0, nbits, cut_body, jnp.zeros((1, LANES), I32))
        cut_scr[...] = jnp.broadcast_to(jnp.where(n_eq > need, c, 2 * spad), cut_scr.shape)

    cut = cut_scr[0:1, :]

    def mask_body(e, _):
        ks = chunk_start(e)
        k = key_scr[pl.ds(ks, KCH), :]
        sel = (k > thr) | ((k == thr) & (ks + key_row < cut))
        mask_scr[:, pl.ds(ks, KCH)] = jnp.transpose(jnp.where(sel, 0.0, NEG_BIG))
        return 0

    lax.fori_loop(0, nch, mask_body, 0)

    def masked_scores(e, near):
        ks = chunk_start(e)
        kv = ckv_scr[pl.ds(ks, KCH), :]
        addmask = mask_scr[:, pl.ds(ks, KCH)]
        s = lax.dot_general(qlat_scr[...], kv, (((1,), (1,)), ((), ())), preferred_element_type=F32)
        s = s.reshape(B_HEADS, QB, KCH) + addmask[None]
        if near:
            s = s + bias_ref[...]
        return s, kv

    def max_pass(e, near):
        s, _ = masked_scores(e, near)
        m_scr[...] = jnp.maximum(m_scr[...], jnp.maximum(s[:, :, :LANES], s[:, :, LANES:]))

    def sum_pass(e, near):
        s, kv = masked_scores(e, near)
        m = m_scr[...]
        p_lo = jnp.exp2(s[:, :, :LANES] - m)
        p_hi = jnp.exp2(s[:, :, LANES:] - m)
        l_scr[...] += p_lo + p_hi
        p = jnp.concatenate([p_lo, p_hi], axis=-1).astype(BF16).reshape(B_HEADS * QB, KCH)
        acc_scr[...] += jnp.dot(p, kv, preferred_element_type=F32)

    def sweep(fn):
        def far_body(e, _):
            fn(e, False)
            return 0
        lax.fori_loop(1, nch, far_body, 0)
        fn(0, True)

    def accumulate():
        l_scr[...] = jnp.zeros(l_scr.shape, F32)
        acc_scr[...] = jnp.zeros(acc_scr.shape, F32)
        sweep(sum_pass)

    accumulate()
    smallest = jnp.min(jnp.sum(l_scr[...], axis=-1, keepdims=True))

    @pl.when(jnp.logical_not(smallest >= DSA_MIN_DENOM))
    def _():
        m_scr[...] = jnp.full(m_scr.shape, NEG_BIG, F32)
        sweep(max_pass)
        m_scr[...] = jnp.broadcast_to(jnp.max(m_scr[...], axis=-1, keepdims=True), m_scr.shape)
        accumulate()

    outs = []
    for h in range(B_HEADS):
        l = jnp.sum(l_scr[h], axis=-1, keepdims=True)
        o_lat = (acc_scr[h * QB:(h + 1) * QB, :] / l).astype(BF16)
        outs.append(jnp.dot(o_lat, wuv_ref[h], preferred_element_type=F32))
    o_ref[0] = jnp.concatenate(outs, axis=-1).astype(o_ref.dtype)


def _dsa(proj3, w_uk, w_uv, bias_b, kv_norm_g):
    bn, seq, _ = proj3.shape
    nblk = seq // QB
    spad = seq + QB
    qw = B_HEADS * HEAD_DIM
    iw = IDX_HEADS * LANES
    key_norm = jnp.max(jnp.abs(kv_norm_g.astype(F32))) * (KV_RANK ** 0.5 * BF16_SLACK)
    cb = jnp.broadcast_to(key_norm, (1, LANES))
    mb = jnp.broadcast_to(jnp.maximum(jnp.max(bias_b, axis=(1, 2)), 0.0)[:, None, None],
                          (B_HEADS, 1, LANES))
    return pl.pallas_call(
        functools.partial(_dsa_kernel, topk=min(TOPK_MAX, seq // 4)),
        grid=(bn, nblk),
        in_specs=[
            pl.BlockSpec((1, QB, qw), lambda b, n: (b, n, EV_QB // qw)),
            pl.BlockSpec((1, QB, iw), lambda b, n: (b, n, EV_QI // iw)),
            pl.BlockSpec((1, QB, LANES), lambda b, n: (b, n, EV_KW // LANES)),
            pl.BlockSpec((1, seq, KV_RANK), lambda b, n: (b, 0, EV_CL // KV_RANK)),
            pl.BlockSpec((1, seq, LANES), lambda b, n: (b, 0, EV_KW // LANES)),
            pl.BlockSpec(w_uk.shape, lambda b, n: (0, 0, 0)),
            pl.BlockSpec(w_uv.shape, lambda b, n: (0, 0, 0)),
            pl.BlockSpec((B_HEADS, QB, KCH), lambda b, n: (0, 0, 0)),
            pl.BlockSpec((1, LANES), lambda b, n: (0, 0)),
            pl.BlockSpec((B_HEADS, 1, LANES), lambda b, n: (0, 0, 0)),
        ],
        out_specs=pl.BlockSpec((1, QB, qw), lambda b, n: (b, n, 0)),
        out_shape=jax.ShapeDtypeStruct((bn, seq, qw), BF16),
        scratch_shapes=[
            pltpu.VMEM((IDX_HEADS * QB, LANES), BF16),
            pltpu.VMEM((IDX_HEADS, SUBLANES, LANES), F32),
            pltpu.VMEM((spad, LANES), I32),
            pltpu.VMEM((QB, spad), F32),
            pltpu.VMEM((B_HEADS * QB, KV_RANK), BF16),
            pltpu.VMEM((B_HEADS, QB, LANES), F32),
            pltpu.VMEM((B_HEADS, QB, LANES), F32),
            pltpu.VMEM((B_HEADS * QB, KV_RANK), F32),
            pltpu.VMEM((SUBLANES, LANES), I32),
            pltpu.VMEM((spad, KV_RANK), BF16),
            pltpu.VMEM((spad, LANES), BF16),
        ],
        compiler_params=_cparams(("arbitrary", "arbitrary")),
        name="dsa",
    )(proj3, proj3, proj3, proj3, proj3, w_uk, w_uv, bias_b, cb, mb)


SB_HG = 8


def _sb_kernel(q_ref, k_ref, v_ref, o_ref, acc_scr):
    n = pl.program_id(2)
    scale = C_HEAD_DIM ** -0.5
    rows = SB_HG * QB
    row = lax.broadcasted_iota(I32, (QB, QB), 0)
    col = lax.broadcasted_iota(I32, (QB, QB), 1)
    suffix = jnp.where(row > col, 1.0, 0.0).astype(BF16)
    suffix2 = jnp.concatenate([suffix, suffix], axis=0)
    earlier = lax.broadcasted_iota(I32, (rows, QB), 1) < (lax.broadcasted_iota(I32, (rows, QB), 0) % QB)

    def head_cols(h):
        return slice(h * C_HEAD_DIM, (h + 1) * C_HEAD_DIM)

    def block(j, carry, diagonal):
        ks = pl.multiple_of(j * QB, QB)
        z = jnp.concatenate(
            [lax.dot_general(q_ref[0, :, head_cols(h)], k_ref[0, pl.ds(ks, QB), head_cols(h)],
                             (((1,), (1,)), ((), ())), preferred_element_type=F32)
             for h in range(SB_HG)], axis=0) * scale
        sp = jnp.maximum(z, 0.0) + jnp.log(1.0 + jnp.exp(-jnp.abs(z)))
        lk = jnp.where(earlier, -sp, 0.0) if diagonal else -sp
        hi = lk.astype(BF16)
        lo = (lk - hi.astype(F32)).astype(BF16)
        inner = jnp.dot(jnp.concatenate([hi, lo], axis=1), suffix2, preferred_element_type=F32)
        a = jnp.exp((z - sp) + (carry + inner))
        if diagonal:
            a = jnp.where(earlier, a, 0.0)
        a = a.astype(BF16)
        for h in range(SB_HG):
            r = slice(h * QB, (h + 1) * QB)
            acc_scr[r, :] += jnp.dot(a[r], v_ref[0, pl.ds(ks, QB), head_cols(h)],
                                     preferred_element_type=F32)
        return carry + jnp.sum(lk, axis=-1, keepdims=True)

    def all_done(carry):
        return (jnp.max(carry) < SB_DONE).astype(I32)

    acc_scr[...] = jnp.zeros(acc_scr.shape, F32)
    carry0 = block(n, jnp.zeros((rows, 1), F32), True)

    def cond(state):
        j, done, _ = state
        return (j >= 0) & (done == 0)

    def body(state):
        j, _, carry = state
        carry = block(j, carry, False)
        return j - 1, all_done(carry), carry

    lax.while_loop(cond, body, (n - 1, jnp.int32(0), carry0))
    o_ref[0] = jnp.concatenate([acc_scr[h * QB:(h + 1) * QB, :] for h in range(SB_HG)],
                               axis=-1).astype(o_ref.dtype)


def _sb(qkv3):
    bn, seq, _ = qkv3.shape
    nblk = seq // QB
    gw = SB_HG * C_HEAD_DIM
    ngrp = C_HEADS // SB_HG
    return pl.pallas_call(
        _sb_kernel,
        grid=(bn, ngrp, nblk),
        in_specs=[
            pl.BlockSpec((1, QB, gw), lambda b, g, n: (b, n, g)),
            pl.BlockSpec((1, seq, gw), lambda b, g, n: (b, 0, ngrp + g)),
            pl.BlockSpec((1, seq, gw), lambda b, g, n: (b, 0, 2 * ngrp + g)),
        ],
        out_specs=pl.BlockSpec((1, QB, gw), lambda b, g, n: (b, n, g)),
        out_shape=jax.ShapeDtypeStruct((bn, seq, C_HEADS * C_HEAD_DIM), BF16),
        scratch_shapes=[pltpu.VMEM((SB_HG * QB, C_HEAD_DIM), F32)],
        compiler_params=_cparams(("arbitrary", "arbitrary", "arbitrary")),
        name="stick_breaking",
    )(qkv3, qkv3, qkv3)


def _t5_bucket(rel):
    half = NUM_BUCKETS // 2
    max_exact = half // 2
    n = jnp.abs(rel)
    nf = jnp.maximum(n, 1).astype(F32)
    large = max_exact + (jnp.log(nf / max_exact) / math.log(MAX_DISTANCE / max_exact)
                         * (half - max_exact)).astype(I32)
    large = jnp.minimum(large, half - 1)
    return jnp.where(rel > 0, half, 0) + jnp.where(n < max_exact, n, large)


def _band_bias(rel_bias):
    rel = (jnp.arange(2 * QB) - QB)[None, :] - jnp.arange(QB)[:, None]
    table = rel_bias.astype(F32)
    bucket = _t5_bucket(rel)[None]
    band = sum(jnp.where(bucket == k, table[k][:, None, None], 0.0) for k in range(NUM_BUCKETS))
    far = table[_t5_bucket(jnp.int32(-(QB + 1)))]
    return band, far


def _even_w_in(w_in):
    d = w_in.shape[0]
    sizes = (A_HEADS * HEAD_DIM, A_KV_HEADS * HEAD_DIM, A_KV_HEADS * HEAD_DIM,
             B_HEADS * HEAD_DIM, KV_RANK, IDX_HEADS * IDX_DIM, IDX_DIM, IDX_HEADS)
    offs = [int(o) for o in np.cumsum((0,) + sizes)]
    w = w_in.astype(BF16)
    moves = [(offs[0], offs[1], EV_QA), (offs[1], offs[2], EV_KA), (offs[2], offs[3], EV_VA),
             (offs[3], offs[4], EV_QB), (offs[4], offs[5], EV_CL), (offs[6], offs[8], EV_KW)]
    moves += [(offs[5] + h * IDX_DIM, offs[5] + (h + 1) * IDX_DIM, EV_QI + h * LANES)
              for h in range(IDX_HEADS)]
    out = jnp.zeros((d, EV_WIDTH), BF16)
    for lo, hi, dst in moves:
        out = lax.dynamic_update_slice(out, w[:, lo:hi], (0, dst))
    return out


def kernel(x, c, rel_bias, ada_w, ada_b, norm_mix_g, norm_ffn_g, ev_w_in, ev_kv_norm_g, ev_w_uk,
           ev_w_uv, ev_sinks, ev_w_out, od_w_in, od_w_out, ffn_w_up, ffn_conv_w, ffn_conv_b,
           ffn_w_down, final_g):
    bn, seq, d = x.shape
    depth = ada_w.shape[0]
    x2 = x.reshape(bn * seq, d)

    mod = _ada_mod(c, ada_w, ada_b)
    band, far = _band_bias(rel_bias)
    bias_a = band[:A_HEADS]
    bias_b = (band[A_HEADS:] - far[A_HEADS:, None, None]) * LOG2E

    for i in range(depth):
        sh1, sc1, g1, sh2, sc2, g2 = [m.reshape(bn, 1, d) for m in jnp.split(mod[i], 6, axis=-1)]
        g_mix = norm_mix_g[i].reshape(1, d)
        j = i // 2
        if i % 2 == 0:
            proj = _in_proj(x2, g_mix, sc1, sh1, _even_w_in(ev_w_in[j]),
                            ev_kv_norm_g[j].reshape(1, KV_RANK), seq, tm=1024, tn=1024, kv_col=EV_CL)
            proj3 = proj.reshape(bn, seq, EV_WIDTH)
            o_a = _swa(proj3, bias_a, ev_sinks[j])
            o_b = _dsa(proj3, ev_w_uk[j].astype(BF16), ev_w_uv[j].astype(BF16), bias_b, ev_kv_norm_g[j])
            w_out = ev_w_out[j].astype(BF16)
            wa = w_out[:A_HEADS * HEAD_DIM]
            wb = w_out[A_HEADS * HEAD_DIM:]
            x2 = _out_proj([o_a.reshape(bn * seq, -1), o_b.reshape(bn * seq, -1)], [wa, wb],
                           x2, g1, seq, tm=512)
        else:
            qkv = _in_proj(x2, g_mix, sc1, sh1, od_w_in[j].astype(BF16),
                           jnp.ones((1, KV_RANK), F32), seq, tm=1024, tn=1024)
            o_c = _sb(qkv.reshape(bn, seq, -1))
            x2 = _out_proj([o_c.reshape(bn * seq, -1)], [od_w_out[j].astype(BF16)], x2, g1, seq, tm=512)
        conv4 = jnp.concatenate([ffn_conv_w[i], ffn_conv_b[i][None]], axis=0)
        x2 = _conv_ffn(x2, norm_ffn_g[i].reshape(1, d), sc2, sh2, g2, ffn_w_up[i].astype(BF16), conv4,
                       ffn_w_down[i].astype(BF16), final_g.reshape(1, d), seq,
                       tm=512, tf=512, final=(i == depth - 1))
    return x2.reshape(bn, seq, d)
```

```python
import functools
import math

import jax
import jax.numpy as jnp
import numpy as np
from jax import lax
from jax.experimental import pallas as pl
from jax.experimental.pallas import tpu as pltpu

F32 = jnp.float32
BF16 = jnp.bfloat16
I32 = jnp.int32

CHUNK = 64
QB = 128
HEAD_DIM = 64
A_HEADS = 16
A_KV_HEADS = 4
A_GROUP = A_HEADS // A_KV_HEADS
WINDOW_CHUNKS = 2
B_HEADS = 16
KV_RANK = 256
IDX_HEADS = 8
IDX_DIM = 64
TOPK_MAX = 256
C_HEADS = 16
C_HEAD_DIM = 128
NUM_BUCKETS = 32
MAX_DISTANCE = 128
CONV_W = 3
EPS = 1e-6

LANES = 128
SUBLANES = 8
MXU_DIM = 256
VMEM_LIMIT = 56 * 1024 * 1024

NEG_BIG = -1e30
LOG2E = math.log2(math.e)
INT_MIN = -(2 ** 31)
SB_DONE = -88.0

EV_QA = 0
EV_QB = EV_QA + A_HEADS * HEAD_DIM
EV_QI = EV_QB + B_HEADS * HEAD_DIM
EV_KA = EV_QI + IDX_HEADS * LANES
EV_VA = EV_KA + A_KV_HEADS * HEAD_DIM
EV_CL = EV_VA + A_KV_HEADS * HEAD_DIM
EV_KW = EV_CL + KV_RANK
EV_WIDTH = 4096


def _cparams(sem):
    return pltpu.CompilerParams(dimension_semantics=sem, vmem_limit_bytes=VMEM_LIMIT)


NORM_RB = 16
NORM_UNROLL = 8


def _row_blocks(n_rows, fn):
    def body(r, _):
        fn(pl.ds(pl.multiple_of(r * NORM_RB, NORM_RB), NORM_RB))
        return 0
    lax.fori_loop(0, n_rows // NORM_RB, body, 0, unroll=NORM_UNROLL)


def _norm_mod_to(h_scr, x_ref, g_ref, sc_ref, sh_ref):
    def block(rows):
        x = x_ref[rows, :]
        ms = jnp.mean(x * x, axis=-1, keepdims=True)
        y = x * lax.rsqrt(ms + EPS)
        h_scr[rows, :] = ((y * g_ref[...]) * (1.0 + sc_ref[0]) + sh_ref[0]).astype(BF16)
    _row_blocks(x_ref.shape[0], block)


def _silu(x):
    return x / (1.0 + jnp.exp(-x))


def _ada_kernel(c_ref, w_ref, b_ref, o_ref):
    a = _silu(c_ref[...]).astype(BF16)
    o_ref[0] = jnp.dot(a, w_ref[0].astype(BF16), preferred_element_type=F32) + b_ref[0]


def _ada_mod(c, ada_w, ada_b):
    depth, d, n = ada_w.shape
    bn = c.shape[0]
    rows = -(-bn // SUBLANES) * SUBLANES
    c_pad = jnp.pad(c, ((0, rows - bn), (0, 0)))
    tn = 1024
    out = pl.pallas_call(
        _ada_kernel,
        grid=(depth, n // tn),
        in_specs=[
            pl.BlockSpec((rows, d), lambda l, j: (0, 0)),
            pl.BlockSpec((1, d, tn), lambda l, j: (l, 0, j)),
            pl.BlockSpec((1, 1, tn), lambda l, j: (l, 0, j)),
        ],
        out_specs=pl.BlockSpec((1, rows, tn), lambda l, j: (l, 0, j)),
        out_shape=jax.ShapeDtypeStruct((depth, rows, n), F32),
        compiler_params=_cparams(("arbitrary", "arbitrary")),
        name="ada_mod",
    )(c_pad, ada_w, ada_b.reshape(depth, 1, n))
    return out[:, :bn]


def _in_proj_kernel(x_ref, g_ref, sc_ref, sh_ref, w_ref, kvg_ref, o_ref, h_scr, *, kv_tile, kv_off):
    j = pl.program_id(1)

    @pl.when(j == 0)
    def _():
        _norm_mod_to(h_scr, x_ref, g_ref, sc_ref, sh_ref)

    def project():
        return jnp.dot(h_scr[...], w_ref[...], preferred_element_type=F32)

    if kv_tile is None:
        o_ref[...] = project().astype(o_ref.dtype)
    else:
        @pl.when(j != kv_tile)
        def _():
            o_ref[...] = project().astype(o_ref.dtype)

        @pl.when(j == kv_tile)
        def _():
            acc = project()
            lat = acc[:, kv_off:kv_off + KV_RANK]
            ms = jnp.mean(lat * lat, axis=-1, keepdims=True)
            lat = lat * lax.rsqrt(ms + EPS) * kvg_ref[...]
            o_ref[...] = acc.astype(o_ref.dtype)
            o_ref[:, kv_off:kv_off + KV_RANK] = lat.astype(o_ref.dtype)


def _in_proj(x2, g, sc, sh, w, kv_g, seq, *, tm, tn, kv_col=None):
    n_rows, d = x2.shape
    n_out = w.shape[1]
    tiles_per_batch = seq // tm
    if kv_col is None:
        kv_tile, kv_off = None, 0
    else:
        kv_tile, kv_off = kv_col // tn, kv_col % tn
    kern = functools.partial(_in_proj_kernel, kv_tile=kv_tile, kv_off=kv_off)
    return pl.pallas_call(
        kern,
        grid=(n_rows // tm, n_out // tn),
        in_specs=[
            pl.BlockSpec((tm, d), lambda i, j: (i, 0)),
            pl.BlockSpec((1, d), lambda i, j: (0, 0)),
            pl.BlockSpec((1, 1, d), lambda i, j: (i // tiles_per_batch, 0, 0)),
            pl.BlockSpec((1, 1, d), lambda i, j: (i // tiles_per_batch, 0, 0)),
            pl.BlockSpec((d, tn), lambda i, j: (0, j)),
            pl.BlockSpec((1, KV_RANK), lambda i, j: (0, 0)),
        ],
        out_specs=pl.BlockSpec((tm, tn), lambda i, j: (i, j)),
        out_shape=jax.ShapeDtypeStruct((n_rows, n_out), BF16),
        scratch_shapes=[pltpu.VMEM((tm, d), BF16)],
        compiler_params=_cparams(("arbitrary", "arbitrary")),
        name="in_proj",
    )(x2, g, sc, sh, w, kv_g)


def _out_proj_kernel(*refs, n_act):
    acts = refs[:n_act]
    ws = refs[n_act:2 * n_act]
    x_ref, gate_ref, o_ref = refs[2 * n_act:]
    y = jnp.dot(acts[0][...], ws[0][...], preferred_element_type=F32)
    for a_ref, w_ref in zip(acts[1:], ws[1:]):
        y = y + jnp.dot(a_ref[...], w_ref[...], preferred_element_type=F32)
    o_ref[...] = x_ref[...] + gate_ref[0] * y


def _out_proj(acts, ws, x2, gate, seq, *, tm):
    n_rows, d = x2.shape
    tiles_per_batch = seq // tm
    n_act = len(acts)
    in_specs = ([pl.BlockSpec((tm, a.shape[1]), lambda i: (i, 0)) for a in acts]
                + [pl.BlockSpec(w.shape, lambda i: (0, 0)) for w in ws]
                + [pl.BlockSpec((tm, d), lambda i: (i, 0)),
                   pl.BlockSpec((1, 1, d), lambda i: (i // tiles_per_batch, 0, 0))])
    return pl.pallas_call(
        functools.partial(_out_proj_kernel, n_act=n_act),
        grid=(n_rows // tm,),
        in_specs=in_specs,
        out_specs=pl.BlockSpec((tm, d), lambda i: (i, 0)),
        out_shape=jax.ShapeDtypeStruct((n_rows, d), F32),
        compiler_params=_cparams(("arbitrary",)),
        name="out_proj",
    )(*acts, *ws, x2, gate)


FFN_RB = 64


def _ffn_kernel(x_ref, g_ref, sc_ref, sh_ref, gate_ref, wg_ref, wv_ref, cg_ref, cv_ref, wd_ref,
                wdl_ref, fg_ref, o_ref, h_scr, ug_scr, uv_scr, a0_scr, a1_scr, carry_g, carry_v, *,
                nf, tiles_per_batch, final):
    i = pl.program_id(0)
    f = pl.program_id(1)
    tm = x_ref.shape[0]

    @pl.when(f == 0)
    def _():
        _norm_mod_to(h_scr, x_ref, g_ref, sc_ref, sh_ref)
        o_ref[...] = jnp.zeros(o_ref.shape, F32)
        a1_scr[...] = jnp.zeros(a1_scr.shape, BF16)

    batch_start = (i % tiles_per_batch) == 0

    @pl.when(batch_start)
    def _():
        ug_scr[0:SUBLANES, :] = jnp.zeros((SUBLANES, ug_scr.shape[1]), F32)
        uv_scr[0:SUBLANES, :] = jnp.zeros((SUBLANES, uv_scr.shape[1]), F32)

    @pl.when(jnp.logical_not(batch_start))
    def _():
        ug_scr[0:SUBLANES, :] = carry_g[f]
        uv_scr[0:SUBLANES, :] = carry_v[f]

    def conv(u_scr, cw, base):
        y = cw[2:3] * u_scr[base:base + FFN_RB, :] + cw[3:4]
        y = y + cw[1:2] * u_scr[base - 1:base - 1 + FFN_RB, :]
        return y + cw[0:1] * u_scr[base - 2:base - 2 + FFN_RB, :]

    def step(a_cur, a_prev):
        ug_scr[SUBLANES:, :] = jnp.dot(h_scr[...], wg_ref[...], preferred_element_type=F32)
        uv_scr[SUBLANES:, :] = jnp.dot(h_scr[...], wv_ref[...], preferred_element_type=F32)
        o_ref[...] += jnp.dot(a_prev[...], wd_ref[...], preferred_element_type=F32)
        carry_g[f] = ug_scr[tm:tm + SUBLANES, :]
        carry_v[f] = uv_scr[tm:tm + SUBLANES, :]
        cwg = cg_ref[...]
        cwv = cv_ref[...]
        for r in range(tm // FFN_RB):
            base = SUBLANES + r * FFN_RB
            act = _silu(conv(ug_scr, cwg, base)) * conv(uv_scr, cwv, base)
            a_cur[r * FFN_RB:(r + 1) * FFN_RB, :] = act.astype(BF16)

    @pl.when(f % 2 == 0)
    def _():
        step(a0_scr, a1_scr)

    @pl.when(f % 2 == 1)
    def _():
        step(a1_scr, a0_scr)

    @pl.when(f == nf - 1)
    def _():
        a_last = a0_scr if (nf - 1) % 2 == 0 else a1_scr
        o_ref[...] += jnp.dot(a_last[...], wdl_ref[...], preferred_element_type=F32)

        def residual(rows):
            xo = x_ref[rows, :] + gate_ref[0] * o_ref[rows, :]
            if final:
                ms = jnp.mean(xo * xo, axis=-1, keepdims=True)
                xo = xo * lax.rsqrt(ms + EPS) * fg_ref[...]
            o_ref[rows, :] = xo
        _row_blocks(tm, residual)


def _conv_ffn(x2, g, sc, sh, gate, w_up, conv4, w_down, final_g, seq, *, tm, tf, final):
    n_rows, d = x2.shape
    d_ff = w_down.shape[0]
    nf = d_ff // tf
    tiles_per_batch = seq // tm
    kern = functools.partial(_ffn_kernel, nf=nf, tiles_per_batch=tiles_per_batch, final=final)
    return pl.pallas_call(
        kern,
        grid=(n_rows // tm, nf),
        in_specs=[
            pl.BlockSpec((tm, d), lambda i, f: (i, 0)),
            pl.BlockSpec((1, d), lambda i, f: (0, 0)),
            pl.BlockSpec((1, 1, d), lambda i, f: (i // tiles_per_batch, 0, 0)),
            pl.BlockSpec((1, 1, d), lambda i, f: (i // tiles_per_batch, 0, 0)),
            pl.BlockSpec((1, 1, d), lambda i, f: (i // tiles_per_batch, 0, 0)),
            pl.BlockSpec((d, tf), lambda i, f: (0, f)),
            pl.BlockSpec((d, tf), lambda i, f: (0, f + nf)),
            pl.BlockSpec((CONV_W + 1, tf), lambda i, f: (0, f)),
            pl.BlockSpec((CONV_W + 1, tf), lambda i, f: (0, f + nf)),
            pl.BlockSpec((tf, d), lambda i, f: (jnp.maximum(f - 1, 0), 0)),
            pl.BlockSpec((tf, d), lambda i, f: (nf - 1, 0)),
            pl.BlockSpec((1, d), lambda i, f: (0, 0)),
        ],
        out_specs=pl.BlockSpec((tm, d), lambda i, f: (i, 0)),
        out_shape=jax.ShapeDtypeStruct((n_rows, d), F32),
        scratch_shapes=[
            pltpu.VMEM((tm, d), BF16),
            pltpu.VMEM((tm + SUBLANES, tf), F32),
            pltpu.VMEM((tm + SUBLANES, tf), F32),
            pltpu.VMEM((tm, tf), BF16),
            pltpu.VMEM((tm, tf), BF16),
            pltpu.VMEM((nf, SUBLANES, tf), F32),
            pltpu.VMEM((nf, SUBLANES, tf), F32),
        ],
        compiler_params=_cparams(("arbitrary", "arbitrary")),
        name="conv_ffn",
    )(x2, g, sc, sh, gate, w_up, w_up, conv4, conv4, w_down, w_down, final_g)


def _swa_kernel(q_ref, kp_ref, ko_ref, vp_ref, vo_ref, bias_ref, sink_ref, o_ref):
    n = pl.program_id(1)
    row = lax.broadcasted_iota(I32, (QB, 2 * QB), 0)
    col = lax.broadcasted_iota(I32, (QB, 2 * QB), 1)
    qc = row // CHUNK
    kc = col // CHUNK - QB // CHUNK
    allowed = (kc <= qc) & (kc >= qc - WINDOW_CHUNKS) & ((col >= QB) | (n > 0))
    allowed_g = jnp.concatenate([allowed] * A_GROUP, axis=0)
    k_all = jnp.concatenate([kp_ref[0], ko_ref[0]], axis=0)
    v_all = jnp.concatenate([vp_ref[0], vo_ref[0]], axis=0)
    q_all = q_ref[0] * (HEAD_DIM ** -0.5)
    low = lax.broadcasted_iota(I32, (QB, LANES), 1) < HEAD_DIM
    zero = jnp.zeros((QB, LANES), q_all.dtype)
    outs = []
    for g in range(A_KV_HEADS):
        heads = range(g * A_GROUP, (g + 1) * A_GROUP)
        kg = k_all[:, g * HEAD_DIM:(g + 1) * HEAD_DIM]
        vg = v_all[:, g * HEAD_DIM:(g + 1) * HEAD_DIM]
        k2 = jnp.concatenate([kg, kg], axis=1)
        v2 = jnp.concatenate([vg, vg], axis=1)
        parts = []
        for c in range(A_GROUP // 2):
            col = (g * A_GROUP + 2 * c) * HEAD_DIM
            q2 = q_all[:, col:col + LANES]
            parts += [jnp.where(low, q2, zero), jnp.where(low, zero, q2)]
        qg = jnp.concatenate(parts, axis=0)
        s = lax.dot_general(qg, k2, (((1,), (1,)), ((), ())), preferred_element_type=F32)
        bias = jnp.concatenate([bias_ref[h] for h in heads], axis=0)
        s = jnp.where(allowed_g, s + bias, NEG_BIG)
        sink = jnp.concatenate([sink_ref[h] for h in heads], axis=0)[:, 0:1]
        m = jnp.maximum(jnp.max(s, axis=-1, keepdims=True), sink)
        p = jnp.exp(s - m)
        denom = jnp.sum(p, axis=-1, keepdims=True) + jnp.exp(sink - m)
        o = jnp.dot(p.astype(BF16), v2, preferred_element_type=F32) / denom
        for c in range(A_GROUP // 2):
            even = o[(2 * c) * QB:(2 * c + 1) * QB]
            odd = o[(2 * c + 1) * QB:(2 * c + 2) * QB]
            outs.append(jnp.where(low, even, odd))
    o_ref[0] = jnp.concatenate(outs, axis=-1).astype(o_ref.dtype)


def _swa(proj3, bias_a, sinks):
    bn, seq, _ = proj3.shape
    nblk = seq // QB
    kvw = A_KV_HEADS * HEAD_DIM
    qw = A_HEADS * HEAD_DIM
    sink_b = jnp.broadcast_to(sinks.astype(F32)[:, None, None], (A_HEADS, QB, LANES))
    prev = lambda c: (lambda b, n: (b, jnp.maximum(n - 1, 0), c))
    own = lambda c: (lambda b, n: (b, n, c))
    return pl.pallas_call(
        _swa_kernel,
        grid=(bn, nblk),
        in_specs=[
            pl.BlockSpec((1, QB, qw), own(EV_QA // qw)),
            pl.BlockSpec((1, QB, kvw), prev(EV_KA // kvw)),
            pl.BlockSpec((1, QB, kvw), own(EV_KA // kvw)),
            pl.BlockSpec((1, QB, kvw), prev(EV_VA // kvw)),
            pl.BlockSpec((1, QB, kvw), own(EV_VA // kvw)),
            pl.BlockSpec((A_HEADS, QB, 2 * QB), lambda b, n: (0, 0, 0)),
            pl.BlockSpec((A_HEADS, QB, LANES), lambda b, n: (0, 0, 0)),
        ],
        out_specs=pl.BlockSpec((1, QB, qw), lambda b, n: (b, n, 0)),
        out_shape=jax.ShapeDtypeStruct((bn, seq, qw), BF16),
        compiler_params=_cparams(("arbitrary", "arbitrary")),
        name="swa",
    )(proj3, proj3, proj3, proj3, proj3, bias_a, sink_b)


KCH = 2 * QB
DSA_MIN_DENOM = 2.0 ** -60
BF16_SLACK = 1.0 + 2.0 ** -7


def _dsa_kernel(qb_ref, qi_ref, kq_ref, ckv_ref, kk_ref, wuk_ref, wuv_ref, bias_ref, cb_ref,
                mb_ref, o_ref,
                qst_scr, wrow_scr, key_scr, mask_scr, qlat_scr, m_scr, l_scr, acc_scr, cut_scr,
                ckv_scr, kk_scr, *, topk):
    n = pl.program_id(1)
    nch = n // 2 + 1
    spad = key_scr.shape[0]

    @pl.when(n == 0)
    def _():
        ckv_scr[0:QB, :] = jnp.zeros((QB, ckv_scr.shape[1]), ckv_scr.dtype)
        kk_scr[0:QB, :] = jnp.zeros((QB, kk_scr.shape[1]), kk_scr.dtype)
        ckv_scr[QB:, :] = ckv_ref[0]
        kk_scr[QB:, :] = kk_ref[0]

    def chunk_start(e):
        return pl.multiple_of((n - 2 * e) * QB, QB)

    idx_scale = IDX_DIM ** -0.5 * IDX_HEADS ** -0.5
    w_t = jnp.transpose(kq_ref[0].astype(F32)) * idx_scale
    for h in range(IDX_HEADS):
        qst_scr[h * QB:(h + 1) * QB, :] = qi_ref[0, :, h * LANES:(h + 1) * LANES]
        wrow_scr[h] = jnp.broadcast_to(w_t[IDX_DIM + h:IDX_DIM + h + 1, :], (SUBLANES, LANES))
    q_all = qb_ref[0]
    for h in range(B_HEADS):
        ql = jnp.dot(q_all[:, h * HEAD_DIM:(h + 1) * HEAD_DIM], wuk_ref[h],
                     preferred_element_type=F32)
        ql = (ql * (HEAD_DIM ** -0.5 * LOG2E)).astype(BF16)
        qlat_scr[h * QB:(h + 1) * QB, :] = ql
        qf = ql.astype(F32)
        qn = jnp.sqrt(jnp.sum(qf * qf, axis=-1, keepdims=True))
        m_scr[h] = jnp.broadcast_to(qn, (QB, LANES)) * cb_ref[...] + mb_ref[h]

    key_row = lax.broadcasted_iota(I32, (KCH, LANES), 0)
    limit = (n * QB + CHUNK) + (lax.broadcasted_iota(I32, (1, LANES), 1) // CHUNK) * CHUNK

    def idx_chunk(e):
        ks = chunk_start(e)
        kj = kk_scr[pl.ds(ks, KCH), :]
        r = lax.dot_general(kj, qst_scr[...], (((1,), (1,)), ((), ())), preferred_element_type=F32)
        sc = jnp.maximum(r[:, 0:LANES], 0.0) * wrow_scr[0][0:1, :]
        for h in range(1, IDX_HEADS):
            sc = sc + jnp.maximum(r[:, h * LANES:(h + 1) * LANES], 0.0) * wrow_scr[h][0:1, :]
        bits = lax.bitcast_convert_type(sc, I32)
        key = bits ^ ((bits >> 31) & 0x7FFFFFFF)
        key = jnp.where(bits == INT_MIN, 0, key)
        kpos = ks - QB + key_row
        key = jnp.where((kpos >= 0) & (kpos < limit), key, INT_MIN)
        key_scr[pl.ds(ks, KCH), :] = key

    def idx_body(i, _):
        idx_chunk(2 * i)
        idx_chunk(jnp.minimum(2 * i + 1, nch - 1))
        return 0

    lax.fori_loop(0, (nch + 1) // 2, idx_body, 0)

    def count(pred):
        def hits(e):
            ks = chunk_start(e)
            hit = jnp.where(pred(key_scr[pl.ds(ks, KCH), :], ks + key_row), 1, 0)
            return jnp.sum(hit.reshape(KCH // SUBLANES, SUBLANES, LANES), axis=0)

        def pair(i, acc):
            return acc + (hits(2 * i) + hits(2 * i + 1))

        c = lax.fori_loop(0, nch // 2, pair, jnp.zeros((SUBLANES, LANES), I32))
        c = c + jnp.where(nch % 2 == 1, hits(nch - 1), 0)
        return jnp.sum(c, axis=0, keepdims=True)

    def bit_body(b, t):
        cand = t ^ jnp.left_shift(jnp.int32(1), 31 - b)
        tot = count(lambda k, pos: k >= cand)
        return jnp.where(tot >= topk, cand, t)

    thr = lax.fori_loop(0, 32, bit_body, jnp.full((1, LANES), INT_MIN, I32))
    thr = jnp.maximum(thr, INT_MIN + 1)

    n_gt = count(lambda k, pos: k > thr)
    n_eq = count(lambda k, pos: k == thr)
    need = topk - n_gt
    cut_scr[...] = jnp.full(cut_scr.shape, 2 * spad, I32)
    excess = jnp.max(jnp.where(n_eq > need, 1, 0))

    @pl.when(excess > 0)
    def _():
        nbits = int(spad).bit_length()

        def cut_body(b, c):
            cand = c | jnp.left_shift(jnp.int32(1), nbits - 1 - b)
            tot = count(lambda k, pos: (k == thr) & (pos < cand))
            return jnp.where(tot <= need, cand, c)

        c = lax.fori_loop(0, nbits, cut_body, jnp.zeros((1, LANES), I32))
        cut_scr[...] = jnp.broadcast_to(jnp.where(n_eq > need, c, 2 * spad), cut_scr.shape)

    cut = cut_scr[0:1, :]

    def mask_body(e, _):
        ks = chunk_start(e)
        k = key_scr[pl.ds(ks, KCH), :]
        sel = (k > thr) | ((k == thr) & (ks + key_row < cut))
        mask_scr[:, pl.ds(ks, KCH)] = jnp.transpose(jnp.where(sel, 0.0, NEG_BIG))
        return 0

    lax.fori_loop(0, nch, mask_body, 0)

    def masked_scores(e, near):
        ks = chunk_start(e)
        kv = ckv_scr[pl.ds(ks, KCH), :]
        addmask = mask_scr[:, pl.ds(ks, KCH)]
        s = lax.dot_general(qlat_scr[...], kv, (((1,), (1,)), ((), ())), preferred_element_type=F32)
        s = s.reshape(B_HEADS, QB, KCH) + addmask[None]
        if near:
            s = s + bias_ref[...]
        return s, kv

    def max_pass(e, near):
        s, _ = masked_scores(e, near)
        m_scr[...] = jnp.maximum(m_scr[...], jnp.maximum(s[:, :, :LANES], s[:, :, LANES:]))

    def sum_pass(e, near):
        s, kv = masked_scores(e, near)
        m = m_scr[...]
        p_lo = jnp.exp2(s[:, :, :LANES] - m)
        p_hi = jnp.exp2(s[:, :, LANES:] - m)
        l_scr[...] += p_lo + p_hi
        p = jnp.concatenate([p_lo, p_hi], axis=-1).astype(BF16).reshape(B_HEADS * QB, KCH)
        acc_scr[...] += jnp.dot(p, kv, preferred_element_type=F32)

    def sweep(fn):
        def far_pair(i, _):
            fn(2 * i + 1, False)
            fn(2 * i + 2, False)
            return 0
        lax.fori_loop(0, (nch - 1) // 2, far_pair, 0)

        @pl.when((nch - 1) % 2 == 1)
        def _():
            fn(nch - 1, False)

        fn(0, True)

    def accumulate():
        l_scr[...] = jnp.zeros(l_scr.shape, F32)
        acc_scr[...] = jnp.zeros(acc_scr.shape, F32)
        sweep(sum_pass)

    accumulate()
    smallest = jnp.min(jnp.sum(l_scr[...], axis=-1, keepdims=True))

    @pl.when(jnp.logical_not(smallest >= DSA_MIN_DENOM))
    def _():
        m_scr[...] = jnp.full(m_scr.shape, NEG_BIG, F32)
        sweep(max_pass)
        m_scr[...] = jnp.broadcast_to(jnp.max(m_scr[...], axis=-1, keepdims=True), m_scr.shape)
        accumulate()

    outs = []
    for h in range(B_HEADS):
        l = jnp.sum(l_scr[h], axis=-1, keepdims=True)
        o_lat = (acc_scr[h * QB:(h + 1) * QB, :] / l).astype(BF16)
        outs.append(jnp.dot(o_lat, wuv_ref[h], preferred_element_type=F32))
    o_ref[0] = jnp.concatenate(outs, axis=-1).astype(o_ref.dtype)


def _dsa(proj3, w_uk, w_uv, bias_b, kv_norm_g):
    bn, seq, _ = proj3.shape
    nblk = seq // QB
    spad = seq + QB
    qw = B_HEADS * HEAD_DIM
    iw = IDX_HEADS * LANES
    key_norm = jnp.max(jnp.abs(kv_norm_g.astype(F32))) * (KV_RANK ** 0.5 * BF16_SLACK)
    cb = jnp.broadcast_to(key_norm, (1, LANES))
    mb = jnp.broadcast_to(jnp.maximum(jnp.max(bias_b, axis=(1, 2)), 0.0)[:, None, None],
                          (B_HEADS, 1, LANES))
    return pl.pallas_call(
        functools.partial(_dsa_kernel, topk=min(TOPK_MAX, seq // 4)),
        grid=(bn, nblk),
        in_specs=[
            pl.BlockSpec((1, QB, qw), lambda b, n: (b, n, EV_QB // qw)),
            pl.BlockSpec((1, QB, iw), lambda b, n: (b, n, EV_QI // iw)),
            pl.BlockSpec((1, QB, LANES), lambda b, n: (b, n, EV_KW // LANES)),
            pl.BlockSpec((1, seq, KV_RANK), lambda b, n: (b, 0, EV_CL // KV_RANK)),
            pl.BlockSpec((1, seq, LANES), lambda b, n: (b, 0, EV_KW // LANES)),
            pl.BlockSpec(w_uk.shape, lambda b, n: (0, 0, 0)),
            pl.BlockSpec(w_uv.shape, lambda b, n: (0, 0, 0)),
            pl.BlockSpec((B_HEADS, QB, KCH), lambda b, n: (0, 0, 0)),
            pl.BlockSpec((1, LANES), lambda b, n: (0, 0)),
            pl.BlockSpec((B_HEADS, 1, LANES), lambda b, n: (0, 0, 0)),
        ],
        out_specs=pl.BlockSpec((1, QB, qw), lambda b, n: (b, n, 0)),
        out_shape=jax.ShapeDtypeStruct((bn, seq, qw), BF16),
        scratch_shapes=[
            pltpu.VMEM((IDX_HEADS * QB, LANES), BF16),
            pltpu.VMEM((IDX_HEADS, SUBLANES, LANES), F32),
            pltpu.VMEM((spad, LANES), I32),
            pltpu.VMEM((QB, spad), F32),
            pltpu.VMEM((B_HEADS * QB, KV_RANK), BF16),
            pltpu.VMEM((B_HEADS, QB, LANES), F32),
            pltpu.VMEM((B_HEADS, QB, LANES), F32),
            pltpu.VMEM((B_HEADS * QB, KV_RANK), F32),
            pltpu.VMEM((SUBLANES, LANES), I32),
            pltpu.VMEM((spad, KV_RANK), BF16),
            pltpu.VMEM((spad, LANES), BF16),
        ],
        compiler_params=_cparams(("arbitrary", "arbitrary")),
        name="dsa",
    )(proj3, proj3, proj3, proj3, proj3, w_uk, w_uv, bias_b, cb, mb)


SB_HG = 8


def _sb_kernel(q_ref, k_ref, v_ref, o_ref, acc_scr, carry_scr):
    n = pl.program_id(2)
    scale = C_HEAD_DIM ** -0.5
    rows = SB_HG * QB

    def sums_operand(width):
        r = lax.broadcasted_iota(I32, (width, width), 0)
        c = lax.broadcasted_iota(I32, (width, width), 1)
        suffix = jnp.where(r > c, 1.0, 0.0).astype(BF16)
        sums = jnp.concatenate([suffix, jnp.ones((width, QB), BF16)], axis=1)
        return jnp.concatenate([sums, sums], axis=0)

    def head_cols(h):
        return slice(h * C_HEAD_DIM, (h + 1) * C_HEAD_DIM)

    def chunk(ks, width, diagonal):
        z = jnp.concatenate(
            [lax.dot_general(q_ref[0, :, head_cols(h)], k_ref[0, pl.ds(ks, width), head_cols(h)],
                             (((1,), (1,)), ((), ())), preferred_element_type=F32)
             for h in range(SB_HG)], axis=0) * scale
        sp = jnp.maximum(z, 0.0) + jnp.log(1.0 + jnp.exp(-jnp.abs(z)))
        if diagonal:
            key = lax.broadcasted_iota(I32, (rows, width), 1) - (width - QB)
            earlier = key < (lax.broadcasted_iota(I32, (rows, width), 0) % QB)
            lk = jnp.where(earlier, -sp, 0.0)
        else:
            lk = -sp
        hi = lk.astype(BF16)
        lo = (lk - hi.astype(F32)).astype(BF16)
        inner = jnp.dot(jnp.concatenate([hi, lo], axis=1), sums_operand(width),
                        preferred_element_type=F32)
        carry = carry_scr[...]
        a = jnp.exp((z - sp) + (jnp.concatenate([carry] * (width // QB), axis=1) + inner[:, :width]))
        if diagonal:
            a = jnp.where(earlier, a, 0.0)
        a = a.astype(BF16)
        for h in range(SB_HG):
            r = slice(h * QB, (h + 1) * QB)
            acc_scr[r, :] += jnp.dot(a[r], v_ref[0, pl.ds(ks, width), head_cols(h)],
                                     preferred_element_type=F32)
        carry = carry + inner[:, width:]
        carry_scr[...] = carry
        return (jnp.max(carry) < SB_DONE).astype(I32)

    acc_scr[...] = jnp.zeros(acc_scr.shape, F32)
    carry_scr[...] = jnp.zeros(carry_scr.shape, F32)

    @pl.when(n % 2 == 0)
    def _():
        chunk(pl.multiple_of(n * QB, QB), QB, True)

    @pl.when(n % 2 == 1)
    def _():
        chunk(pl.multiple_of((n - 1) * QB, QB), 2 * QB, True)

    def cond(state):
        j, done = state
        return (j >= 1) & (done == 0)

    def body(state):
        j, _ = state
        return j - 2, chunk(pl.multiple_of((j - 1) * QB, QB), 2 * QB, False)

    lax.while_loop(cond, body, (n - 1 - n % 2, jnp.int32(0)))
    o_ref[0] = jnp.concatenate([acc_scr[h * QB:(h + 1) * QB, :] for h in range(SB_HG)],
                               axis=-1).astype(o_ref.dtype)


def _sb(qkv3):
    bn, seq, _ = qkv3.shape
    nblk = seq // QB
    gw = SB_HG * C_HEAD_DIM
    ngrp = C_HEADS // SB_HG
    return pl.pallas_call(
        _sb_kernel,
        grid=(bn, ngrp, nblk),
        in_specs=[
            pl.BlockSpec((1, QB, gw), lambda b, g, n: (b, n, g)),
            pl.BlockSpec((1, seq, gw), lambda b, g, n: (b, 0, ngrp + g)),
            pl.BlockSpec((1, seq, gw), lambda b, g, n: (b, 0, 2 * ngrp + g)),
        ],
        out_specs=pl.BlockSpec((1, QB, gw), lambda b, g, n: (b, n, g)),
        out_shape=jax.ShapeDtypeStruct((bn, seq, C_HEADS * C_HEAD_DIM), BF16),
        scratch_shapes=[pltpu.VMEM((SB_HG * QB, C_HEAD_DIM), F32),
                        pltpu.VMEM((SB_HG * QB, QB), F32)],
        compiler_params=_cparams(("arbitrary", "arbitrary", "arbitrary")),
        name="stick_breaking",
    )(qkv3, qkv3, qkv3)


def _t5_bucket(rel):
    half = NUM_BUCKETS // 2
    max_exact = half // 2
    n = jnp.abs(rel)
    nf = jnp.maximum(n, 1).astype(F32)
    large = max_exact + (jnp.log(nf / max_exact) / math.log(MAX_DISTANCE / max_exact)
                         * (half - max_exact)).astype(I32)
    large = jnp.minimum(large, half - 1)
    return jnp.where(rel > 0, half, 0) + jnp.where(n < max_exact, n, large)


def _band_bias(rel_bias):
    rel = (jnp.arange(2 * QB) - QB)[None, :] - jnp.arange(QB)[:, None]
    table = rel_bias.astype(F32)
    bucket = _t5_bucket(rel)[None]
    band = sum(jnp.where(bucket == k, table[k][:, None, None], 0.0) for k in range(NUM_BUCKETS))
    far = table[_t5_bucket(jnp.int32(-(QB + 1)))]
    return band, far


def _even_w_in(w_in):
    d = w_in.shape[0]
    sizes = (A_HEADS * HEAD_DIM, A_KV_HEADS * HEAD_DIM, A_KV_HEADS * HEAD_DIM,
             B_HEADS * HEAD_DIM, KV_RANK, IDX_HEADS * IDX_DIM, IDX_DIM, IDX_HEADS)
    offs = [int(o) for o in np.cumsum((0,) + sizes)]
    w = w_in.astype(BF16)
    moves = [(offs[0], offs[1], EV_QA), (offs[1], offs[2], EV_KA), (offs[2], offs[3], EV_VA),
             (offs[3], offs[4], EV_QB), (offs[4], offs[5], EV_CL), (offs[6], offs[8], EV_KW)]
    moves += [(offs[5] + h * IDX_DIM, offs[5] + (h + 1) * IDX_DIM, EV_QI + h * LANES)
              for h in range(IDX_HEADS)]
    out = jnp.zeros((d, EV_WIDTH), BF16)
    for lo, hi, dst in moves:
        out = lax.dynamic_update_slice(out, w[:, lo:hi], (0, dst))
    return out


def kernel(x, c, rel_bias, ada_w, ada_b, norm_mix_g, norm_ffn_g, ev_w_in, ev_kv_norm_g, ev_w_uk,
           ev_w_uv, ev_sinks, ev_w_out, od_w_in, od_w_out, ffn_w_up, ffn_conv_w, ffn_conv_b,
           ffn_w_down, final_g):
    bn, seq, d = x.shape
    depth = ada_w.shape[0]
    x2 = x.reshape(bn * seq, d)

    mod = _ada_mod(c, ada_w, ada_b)
    band, far = _band_bias(rel_bias)
    bias_a = band[:A_HEADS]
    bias_b = (band[A_HEADS:] - far[A_HEADS:, None, None]) * LOG2E

    for i in range(depth):
        sh1, sc1, g1, sh2, sc2, g2 = [m.reshape(bn, 1, d) for m in jnp.split(mod[i], 6, axis=-1)]
        g_mix = norm_mix_g[i].reshape(1, d)
        j = i // 2
        if i % 2 == 0:
            proj = _in_proj(x2, g_mix, sc1, sh1, _even_w_in(ev_w_in[j]),
                            ev_kv_norm_g[j].reshape(1, KV_RANK), seq, tm=1024, tn=1024, kv_col=EV_CL)
            proj3 = proj.reshape(bn, seq, EV_WIDTH)
            o_a = _swa(proj3, bias_a, ev_sinks[j])
            o_b = _dsa(proj3, ev_w_uk[j].astype(BF16), ev_w_uv[j].astype(BF16), bias_b, ev_kv_norm_g[j])
            w_out = ev_w_out[j].astype(BF16)
            wa = w_out[:A_HEADS * HEAD_DIM]
            wb = w_out[A_HEADS * HEAD_DIM:]
            x2 = _out_proj([o_a.reshape(bn * seq, -1), o_b.reshape(bn * seq, -1)], [wa, wb],
                           x2, g1, seq, tm=512)
        else:
            qkv = _in_proj(x2, g_mix, sc1, sh1, od_w_in[j].astype(BF16),
                           jnp.ones((1, KV_RANK), F32), seq, tm=1024, tn=1024)
            o_c = _sb(qkv.reshape(bn, seq, -1))
            x2 = _out_proj([o_c.reshape(bn * seq, -1)], [od_w_out[j].astype(BF16)], x2, g1, seq, tm=512)
        conv4 = jnp.concatenate([ffn_conv_w[i], ffn_conv_b[i][None]], axis=0)
        x2 = _conv_ffn(x2, norm_ffn_g[i].reshape(1, d), sc2, sh2, g2, ffn_w_up[i].astype(BF16), conv4,
                       ffn_w_down[i].astype(BF16), final_g.reshape(1, d), seq,
                       tm=512, tf=512, final=(i == depth - 1))
    return x2.reshape(bn, seq, d)
```

```python
import functools
import math

import jax
import jax.numpy as jnp
import numpy as np
from jax import lax
from jax.experimental import pallas as pl
from jax.experimental.pallas import tpu as pltpu

F32 = jnp.float32
BF16 = jnp.bfloat16
I32 = jnp.int32

CHUNK = 64
QB = 128
HEAD_DIM = 64
A_HEADS = 16
A_KV_HEADS = 4
A_GROUP = A_HEADS // A_KV_HEADS
WINDOW_CHUNKS = 2
B_HEADS = 16
KV_RANK = 256
IDX_HEADS = 8
IDX_DIM = 64
TOPK_MAX = 256
C_HEADS = 16
C_HEAD_DIM = 128
NUM_BUCKETS = 32
MAX_DISTANCE = 128
CONV_W = 3
EPS = 1e-6

LANES = 128
SUBLANES = 8
MXU_DIM = 256
VMEM_LIMIT = 56 * 1024 * 1024

NEG_BIG = -1e30
LOG2E = math.log2(math.e)
INT_MIN = -(2 ** 31)
SB_DONE = -88.0

EV_QA = 0
EV_QB = EV_QA + A_HEADS * HEAD_DIM
EV_QI = EV_QB + B_HEADS * HEAD_DIM
EV_KA = EV_QI + IDX_HEADS * LANES
EV_VA = EV_KA + A_KV_HEADS * HEAD_DIM
EV_CL = EV_VA + A_KV_HEADS * HEAD_DIM
EV_KW = EV_CL + KV_RANK
EV_WIDTH = 4096


def _cparams(sem):
    return pltpu.CompilerParams(dimension_semantics=sem, vmem_limit_bytes=VMEM_LIMIT)


NORM_RB = 16
NORM_UNROLL = 8


def _row_blocks(n_rows, fn):
    def body(r, _):
        fn(pl.ds(pl.multiple_of(r * NORM_RB, NORM_RB), NORM_RB))
        return 0
    lax.fori_loop(0, n_rows // NORM_RB, body, 0, unroll=NORM_UNROLL)


def _norm_mod_to(h_scr, x_ref, g_ref, sc_ref, sh_ref):
    def block(rows):
        x = x_ref[rows, :]
        ms = jnp.mean(x * x, axis=-1, keepdims=True)
        y = x * lax.rsqrt(ms + EPS)
        h_scr[rows, :] = ((y * g_ref[...]) * (1.0 + sc_ref[0]) + sh_ref[0]).astype(BF16)
    _row_blocks(x_ref.shape[0], block)


def _silu(x):
    return x / (1.0 + jnp.exp(-x))


def _ada_kernel(c_ref, w_ref, b_ref, o_ref):
    a = _silu(c_ref[...]).astype(BF16)
    o_ref[0] = jnp.dot(a, w_ref[0].astype(BF16), preferred_element_type=F32) + b_ref[0]


def _ada_mod(c, ada_w, ada_b):
    depth, d, n = ada_w.shape
    bn = c.shape[0]
    rows = -(-bn // SUBLANES) * SUBLANES
    c_pad = jnp.pad(c, ((0, rows - bn), (0, 0)))
    tn = 1024
    out = pl.pallas_call(
        _ada_kernel,
        grid=(depth, n // tn),
        in_specs=[
            pl.BlockSpec((rows, d), lambda l, j: (0, 0)),
            pl.BlockSpec((1, d, tn), lambda l, j: (l, 0, j)),
            pl.BlockSpec((1, 1, tn), lambda l, j: (l, 0, j)),
        ],
        out_specs=pl.BlockSpec((1, rows, tn), lambda l, j: (l, 0, j)),
        out_shape=jax.ShapeDtypeStruct((depth, rows, n), F32),
        compiler_params=_cparams(("arbitrary", "arbitrary")),
        name="ada_mod",
    )(c_pad, ada_w, ada_b.reshape(depth, 1, n))
    return out[:, :bn]


def _in_proj_kernel(x_ref, g_ref, sc_ref, sh_ref, w_ref, kvg_ref, o_ref, h_scr, *, kv_tile, kv_off):
    j = pl.program_id(1)

    @pl.when(j == 0)
    def _():
        _norm_mod_to(h_scr, x_ref, g_ref, sc_ref, sh_ref)

    def project():
        return jnp.dot(h_scr[...], w_ref[...], preferred_element_type=F32)

    if kv_tile is None:
        o_ref[...] = project().astype(o_ref.dtype)
    else:
        @pl.when(j != kv_tile)
        def _():
            o_ref[...] = project().astype(o_ref.dtype)

        @pl.when(j == kv_tile)
        def _():
            acc = project()
            lat = acc[:, kv_off:kv_off + KV_RANK]
            ms = jnp.mean(lat * lat, axis=-1, keepdims=True)
            lat = lat * lax.rsqrt(ms + EPS) * kvg_ref[...]
            o_ref[...] = acc.astype(o_ref.dtype)
            o_ref[:, kv_off:kv_off + KV_RANK] = lat.astype(o_ref.dtype)


def _in_proj(x2, g, sc, sh, w, kv_g, seq, *, tm, tn, kv_col=None):
    n_rows, d = x2.shape
    n_out = w.shape[1]
    tiles_per_batch = seq // tm
    if kv_col is None:
        kv_tile, kv_off = None, 0
    else:
        kv_tile, kv_off = kv_col // tn, kv_col % tn
    kern = functools.partial(_in_proj_kernel, kv_tile=kv_tile, kv_off=kv_off)
    return pl.pallas_call(
        kern,
        grid=(n_rows // tm, n_out // tn),
        in_specs=[
            pl.BlockSpec((tm, d), lambda i, j: (i, 0)),
            pl.BlockSpec((1, d), lambda i, j: (0, 0)),
            pl.BlockSpec((1, 1, d), lambda i, j: (i // tiles_per_batch, 0, 0)),
            pl.BlockSpec((1, 1, d), lambda i, j: (i // tiles_per_batch, 0, 0)),
            pl.BlockSpec((d, tn), lambda i, j: (0, j)),
            pl.BlockSpec((1, KV_RANK), lambda i, j: (0, 0)),
        ],
        out_specs=pl.BlockSpec((tm, tn), lambda i, j: (i, j)),
        out_shape=jax.ShapeDtypeStruct((n_rows, n_out), BF16),
        scratch_shapes=[pltpu.VMEM((tm, d), BF16)],
        compiler_params=_cparams(("arbitrary", "arbitrary")),
        name="in_proj",
    )(x2, g, sc, sh, w, kv_g)


def _out_proj_kernel(*refs, n_act):
    acts = refs[:n_act]
    ws = refs[n_act:2 * n_act]
    x_ref, gate_ref, o_ref = refs[2 * n_act:]
    y = jnp.dot(acts[0][...], ws[0][...], preferred_element_type=F32)
    for a_ref, w_ref in zip(acts[1:], ws[1:]):
        y = y + jnp.dot(a_ref[...], w_ref[...], preferred_element_type=F32)
    o_ref[...] = x_ref[...] + gate_ref[0] * y


def _out_proj(acts, ws, x2, gate, seq, *, tm):
    n_rows, d = x2.shape
    tiles_per_batch = seq // tm
    n_act = len(acts)
    in_specs = ([pl.BlockSpec((tm, a.shape[1]), lambda i: (i, 0)) for a in acts]
                + [pl.BlockSpec(w.shape, lambda i: (0, 0)) for w in ws]
                + [pl.BlockSpec((tm, d), lambda i: (i, 0)),
                   pl.BlockSpec((1, 1, d), lambda i: (i // tiles_per_batch, 0, 0))])
    return pl.pallas_call(
        functools.partial(_out_proj_kernel, n_act=n_act),
        grid=(n_rows // tm,),
        in_specs=in_specs,
        out_specs=pl.BlockSpec((tm, d), lambda i: (i, 0)),
        out_shape=jax.ShapeDtypeStruct((n_rows, d), F32),
        compiler_params=_cparams(("arbitrary",)),
        name="out_proj",
    )(*acts, *ws, x2, gate)


FFN_RB = 64


def _ffn_kernel(x_ref, g_ref, sc_ref, sh_ref, gate_ref, wg_ref, wv_ref, cg_ref, cv_ref, wd_ref,
                wdl_ref, fg_ref, o_ref, h_scr, ug_scr, uv_scr, a0_scr, a1_scr, carry_g, carry_v, *,
                nf, tiles_per_batch, final):
    i = pl.program_id(0)
    f = pl.program_id(1)
    tm = x_ref.shape[0]

    @pl.when(f == 0)
    def _():
        _norm_mod_to(h_scr, x_ref, g_ref, sc_ref, sh_ref)
        o_ref[...] = jnp.zeros(o_ref.shape, F32)
        a1_scr[...] = jnp.zeros(a1_scr.shape, BF16)

    batch_start = (i % tiles_per_batch) == 0

    @pl.when(batch_start)
    def _():
        ug_scr[0:SUBLANES, :] = jnp.zeros((SUBLANES, ug_scr.shape[1]), F32)
        uv_scr[0:SUBLANES, :] = jnp.zeros((SUBLANES, uv_scr.shape[1]), F32)

    @pl.when(jnp.logical_not(batch_start))
    def _():
        ug_scr[0:SUBLANES, :] = carry_g[f]
        uv_scr[0:SUBLANES, :] = carry_v[f]

    def conv(u_scr, cw, base):
        y = cw[2:3] * u_scr[base:base + FFN_RB, :] + cw[3:4]
        y = y + cw[1:2] * u_scr[base - 1:base - 1 + FFN_RB, :]
        return y + cw[0:1] * u_scr[base - 2:base - 2 + FFN_RB, :]

    def step(a_cur, a_prev):
        ug_scr[SUBLANES:, :] = jnp.dot(h_scr[...], wg_ref[...], preferred_element_type=F32)
        uv_scr[SUBLANES:, :] = jnp.dot(h_scr[...], wv_ref[...], preferred_element_type=F32)
        o_ref[...] += jnp.dot(a_prev[...], wd_ref[...], preferred_element_type=F32)
        carry_g[f] = ug_scr[tm:tm + SUBLANES, :]
        carry_v[f] = uv_scr[tm:tm + SUBLANES, :]
        cwg = cg_ref[...]
        cwv = cv_ref[...]
        for r in range(tm // FFN_RB):
            base = SUBLANES + r * FFN_RB
            act = _silu(conv(ug_scr, cwg, base)) * conv(uv_scr, cwv, base)
            a_cur[r * FFN_RB:(r + 1) * FFN_RB, :] = act.astype(BF16)

    @pl.when(f % 2 == 0)
    def _():
        step(a0_scr, a1_scr)

    @pl.when(f % 2 == 1)
    def _():
        step(a1_scr, a0_scr)

    @pl.when(f == nf - 1)
    def _():
        a_last = a0_scr if (nf - 1) % 2 == 0 else a1_scr
        o_ref[...] += jnp.dot(a_last[...], wdl_ref[...], preferred_element_type=F32)

        def residual(rows):
            xo = x_ref[rows, :] + gate_ref[0] * o_ref[rows, :]
            if final:
                ms = jnp.mean(xo * xo, axis=-1, keepdims=True)
                xo = xo * lax.rsqrt(ms + EPS) * fg_ref[...]
            o_ref[rows, :] = xo
        _row_blocks(tm, residual)


def _conv_ffn(x2, g, sc, sh, gate, w_up, conv4, w_down, final_g, seq, *, layer, tm, tf, final):
    n_rows, d = x2.shape
    d_ff = w_down.shape[1]
    nf = d_ff // tf
    tiles_per_batch = seq // tm
    kern = functools.partial(_ffn_kernel, nf=nf, tiles_per_batch=tiles_per_batch, final=final)
    return pl.pallas_call(
        kern,
        grid=(n_rows // tm, nf),
        in_specs=[
            pl.BlockSpec((tm, d), lambda i, f: (i, 0), pipeline_mode=pl.Buffered(1)),
            pl.BlockSpec((1, d), lambda i, f: (0, 0)),
            pl.BlockSpec((1, 1, d), lambda i, f: (i // tiles_per_batch, 0, 0)),
            pl.BlockSpec((1, 1, d), lambda i, f: (i // tiles_per_batch, 0, 0)),
            pl.BlockSpec((1, 1, d), lambda i, f: (i // tiles_per_batch, 0, 0)),
            pl.BlockSpec((None, d, tf), lambda i, f: (layer, 0, f)),
            pl.BlockSpec((None, d, tf), lambda i, f: (layer, 0, f + nf)),
            pl.BlockSpec((CONV_W + 1, tf), lambda i, f: (0, f)),
            pl.BlockSpec((CONV_W + 1, tf), lambda i, f: (0, f + nf)),
            pl.BlockSpec((None, tf, d), lambda i, f: (layer, jnp.maximum(f - 1, 0), 0)),
            pl.BlockSpec((None, tf, d), lambda i, f: (layer, nf - 1, 0)),
            pl.BlockSpec((1, d), lambda i, f: (0, 0)),
        ],
        out_specs=pl.BlockSpec((tm, d), lambda i, f: (i, 0)),
        out_shape=jax.ShapeDtypeStruct((n_rows, d), F32),
        scratch_shapes=[
            pltpu.VMEM((tm, d), BF16),
            pltpu.VMEM((tm + SUBLANES, tf), F32),
            pltpu.VMEM((tm + SUBLANES, tf), F32),
            pltpu.VMEM((tm, tf), BF16),
            pltpu.VMEM((tm, tf), BF16),
            pltpu.VMEM((nf, SUBLANES, tf), F32),
            pltpu.VMEM((nf, SUBLANES, tf), F32),
        ],
        compiler_params=_cparams(("arbitrary", "arbitrary")),
        name="conv_ffn",
    )(x2, g, sc, sh, gate, w_up, w_up, conv4, conv4, w_down, w_down, final_g)


def _swa_kernel(q_ref, kp_ref, ko_ref, vp_ref, vo_ref, bias_ref, sink_ref, o_ref):
    n = pl.program_id(1)
    row = lax.broadcasted_iota(I32, (QB, 2 * QB), 0)
    col = lax.broadcasted_iota(I32, (QB, 2 * QB), 1)
    qc = row // CHUNK
    kc = col // CHUNK - QB // CHUNK
    allowed = (kc <= qc) & (kc >= qc - WINDOW_CHUNKS) & ((col >= QB) | (n > 0))
    allowed_g = jnp.concatenate([allowed] * A_GROUP, axis=0)
    k_all = jnp.concatenate([kp_ref[0], ko_ref[0]], axis=0)
    v_all = jnp.concatenate([vp_ref[0], vo_ref[0]], axis=0)
    q_all = q_ref[0] * (HEAD_DIM ** -0.5)
    low = lax.broadcasted_iota(I32, (QB, LANES), 1) < HEAD_DIM
    zero = jnp.zeros((QB, LANES), q_all.dtype)
    outs = []
    for g in range(A_KV_HEADS):
        heads = range(g * A_GROUP, (g + 1) * A_GROUP)
        kg = k_all[:, g * HEAD_DIM:(g + 1) * HEAD_DIM]
        vg = v_all[:, g * HEAD_DIM:(g + 1) * HEAD_DIM]
        k2 = jnp.concatenate([kg, kg], axis=1)
        v2 = jnp.concatenate([vg, vg], axis=1)
        parts = []
        for c in range(A_GROUP // 2):
            col = (g * A_GROUP + 2 * c) * HEAD_DIM
            q2 = q_all[:, col:col + LANES]
            parts += [jnp.where(low, q2, zero), jnp.where(low, zero, q2)]
        qg = jnp.concatenate(parts, axis=0)
        s = lax.dot_general(qg, k2, (((1,), (1,)), ((), ())), preferred_element_type=F32)
        bias = jnp.concatenate([bias_ref[h] for h in heads], axis=0)
        s = jnp.where(allowed_g, s + bias, NEG_BIG)
        sink = jnp.concatenate([sink_ref[h] for h in heads], axis=0)[:, 0:1]
        m = jnp.maximum(jnp.max(s, axis=-1, keepdims=True), sink)
        p = jnp.exp(s - m)
        denom = jnp.sum(p, axis=-1, keepdims=True) + jnp.exp(sink - m)
        o = jnp.dot(p.astype(BF16), v2, preferred_element_type=F32) / denom
        for c in range(A_GROUP // 2):
            even = o[(2 * c) * QB:(2 * c + 1) * QB]
            odd = o[(2 * c + 1) * QB:(2 * c + 2) * QB]
            outs.append(jnp.where(low, even, odd))
    o_ref[0] = jnp.concatenate(outs, axis=-1).astype(o_ref.dtype)


def _swa(proj3, bias_a, sinks):
    bn, seq, _ = proj3.shape
    nblk = seq // QB
    kvw = A_KV_HEADS * HEAD_DIM
    qw = A_HEADS * HEAD_DIM
    sink_b = jnp.broadcast_to(sinks.astype(F32)[:, None, None], (A_HEADS, QB, LANES))
    prev = lambda c: (lambda b, n: (b, jnp.maximum(n - 1, 0), c))
    own = lambda c: (lambda b, n: (b, n, c))
    return pl.pallas_call(
        _swa_kernel,
        grid=(bn, nblk),
        in_specs=[
            pl.BlockSpec((1, QB, qw), own(EV_QA // qw)),
            pl.BlockSpec((1, QB, kvw), prev(EV_KA // kvw)),
            pl.BlockSpec((1, QB, kvw), own(EV_KA // kvw)),
            pl.BlockSpec((1, QB, kvw), prev(EV_VA // kvw)),
            pl.BlockSpec((1, QB, kvw), own(EV_VA // kvw)),
            pl.BlockSpec((A_HEADS, QB, 2 * QB), lambda b, n: (0, 0, 0)),
            pl.BlockSpec((A_HEADS, QB, LANES), lambda b, n: (0, 0, 0)),
        ],
        out_specs=pl.BlockSpec((1, QB, qw), lambda b, n: (b, n, 0)),
        out_shape=jax.ShapeDtypeStruct((bn, seq, qw), BF16),
        compiler_params=_cparams(("arbitrary", "arbitrary")),
        name="swa",
    )(proj3, proj3, proj3, proj3, proj3, bias_a, sink_b)


KCH = 2 * QB
DSA_MIN_DENOM = 2.0 ** -60
BF16_SLACK = 1.0 + 2.0 ** -7


def _dsa_kernel(qb_ref, qi_ref, kq_ref, ckv_ref, kk_ref, wuk_ref, wuv_ref, bias_ref, cb_ref,
                mb_ref, o_ref,
                qst_scr, wrow_scr, key_scr, mask_scr, qlat_scr, m_scr, l_scr, acc_scr, cut_scr,
                ckv_scr, kk_scr, *, topk):
    n = pl.program_id(1)
    nch = n // 2 + 1
    spad = key_scr.shape[0]

    @pl.when(n == 0)
    def _():
        ckv_scr[0:QB, :] = jnp.zeros((QB, ckv_scr.shape[1]), ckv_scr.dtype)
        kk_scr[0:QB, :] = jnp.zeros((QB, kk_scr.shape[1]), kk_scr.dtype)
        ckv_scr[QB:, :] = ckv_ref[0]
        kk_scr[QB:, :] = kk_ref[0]

    def chunk_start(e):
        return pl.multiple_of((n - 2 * e) * QB, QB)

    idx_scale = IDX_DIM ** -0.5 * IDX_HEADS ** -0.5
    w_t = jnp.transpose(kq_ref[0].astype(F32)) * idx_scale
    for h in range(IDX_HEADS):
        qst_scr[h * QB:(h + 1) * QB, :] = qi_ref[0, :, h * LANES:(h + 1) * LANES]
        wrow_scr[h] = jnp.broadcast_to(w_t[IDX_DIM + h:IDX_DIM + h + 1, :], (SUBLANES, LANES))
    q_all = qb_ref[0]
    for h in range(B_HEADS):
        ql = jnp.dot(q_all[:, h * HEAD_DIM:(h + 1) * HEAD_DIM], wuk_ref[h],
                     preferred_element_type=F32)
        ql = (ql * (HEAD_DIM ** -0.5 * LOG2E)).astype(BF16)
        qlat_scr[h * QB:(h + 1) * QB, :] = ql
        qf = ql.astype(F32)
        qn = jnp.sqrt(jnp.sum(qf * qf, axis=-1, keepdims=True))
        m_scr[h] = jnp.broadcast_to(qn, (QB, LANES)) * cb_ref[...] + mb_ref[h]

    key_row = lax.broadcasted_iota(I32, (KCH, LANES), 0)
    limit = (n * QB + CHUNK) + (lax.broadcasted_iota(I32, (1, LANES), 1) // CHUNK) * CHUNK

    def idx_chunk(e):
        ks = chunk_start(e)
        kj = kk_scr[pl.ds(ks, KCH), :]
        r = lax.dot_general(kj, qst_scr[...], (((1,), (1,)), ((), ())), preferred_element_type=F32)
        sc = jnp.maximum(r[:, 0:LANES], 0.0) * wrow_scr[0][0:1, :]
        for h in range(1, IDX_HEADS):
            sc = sc + jnp.maximum(r[:, h * LANES:(h + 1) * LANES], 0.0) * wrow_scr[h][0:1, :]
        bits = lax.bitcast_convert_type(sc, I32)
        key = bits ^ ((bits >> 31) & 0x7FFFFFFF)
        key = jnp.where(bits == INT_MIN, 0, key)
        kpos = ks - QB + key_row
        key = jnp.where((kpos >= 0) & (kpos < limit), key, INT_MIN)
        key_scr[pl.ds(ks, KCH), :] = key

    def idx_body(i, _):
        idx_chunk(2 * i)
        idx_chunk(jnp.minimum(2 * i + 1, nch - 1))
        return 0

    lax.fori_loop(0, (nch + 1) // 2, idx_body, 0)

    def count(pred):
        def hits(e):
            ks = chunk_start(e)
            hit = jnp.where(pred(key_scr[pl.ds(ks, KCH), :], ks + key_row), 1, 0)
            return jnp.sum(hit.reshape(KCH // SUBLANES, SUBLANES, LANES), axis=0)

        def pair(i, acc):
            return acc + (hits(2 * i) + hits(2 * i + 1))

        c = lax.fori_loop(0, nch // 2, pair, jnp.zeros((SUBLANES, LANES), I32))
        c = c + jnp.where(nch % 2 == 1, hits(nch - 1), 0)
        return jnp.sum(c, axis=0, keepdims=True)

    def bit_body(b, t):
        cand = t ^ jnp.left_shift(jnp.int32(1), 31 - b)
        tot = count(lambda k, pos: k >= cand)
        return jnp.where(tot >= topk, cand, t)

    thr = lax.fori_loop(0, 32, bit_body, jnp.full((1, LANES), INT_MIN, I32))
    thr = jnp.maximum(thr, INT_MIN + 1)

    n_gt = count(lambda k, pos: k > thr)
    n_eq = count(lambda k, pos: k == thr)
    need = topk - n_gt
    cut_scr[...] = jnp.full(cut_scr.shape, 2 * spad, I32)
    excess = jnp.max(jnp.where(n_eq > need, 1, 0))

    @pl.when(excess > 0)
    def _():
        nbits = int(spad).bit_length()

        def cut_body(b, c):
            cand = c | jnp.left_shift(jnp.int32(1), nbits - 1 - b)
            tot = count(lambda k, pos: (k == thr) & (pos < cand))
            return jnp.where(tot <= need, cand, c)

        c = lax.fori_loop(0, nbits, cut_body, jnp.zeros((1, LANES), I32))
        cut_scr[...] = jnp.broadcast_to(jnp.where(n_eq > need, c, 2 * spad), cut_scr.shape)

    cut = cut_scr[0:1, :]

    def mask_body(e, _):
        ks = chunk_start(e)
        k = key_scr[pl.ds(ks, KCH), :]
        sel = (k > thr) | ((k == thr) & (ks + key_row < cut))
        mask_scr[:, pl.ds(ks, KCH)] = jnp.transpose(jnp.where(sel, 0.0, NEG_BIG))
        return 0

    lax.fori_loop(0, nch, mask_body, 0)

    def masked_scores(e, near):
        ks = chunk_start(e)
        kv = ckv_scr[pl.ds(ks, KCH), :]
        addmask = mask_scr[:, pl.ds(ks, KCH)]
        s = lax.dot_general(qlat_scr[...], kv, (((1,), (1,)), ((), ())), preferred_element_type=F32)
        s = s.reshape(B_HEADS, QB, KCH) + addmask[None]
        if near:
            s = s + bias_ref[...]
        return s, kv

    def max_pass(e, near):
        s, _ = masked_scores(e, near)
        m_scr[...] = jnp.maximum(m_scr[...], jnp.maximum(s[:, :, :LANES], s[:, :, LANES:]))

    def sum_pass(e, near):
        s, kv = masked_scores(e, near)
        m = m_scr[...]
        p_lo = jnp.exp2(s[:, :, :LANES] - m)
        p_hi = jnp.exp2(s[:, :, LANES:] - m)
        l_scr[...] += p_lo + p_hi
        p = jnp.concatenate([p_lo, p_hi], axis=-1).astype(BF16).reshape(B_HEADS * QB, KCH)
        acc_scr[...] += jnp.dot(p, kv, preferred_element_type=F32)

    def sweep(fn):
        def far_pair(i, _):
            fn(2 * i + 1, False)
            fn(2 * i + 2, False)
            return 0
        lax.fori_loop(0, (nch - 1) // 2, far_pair, 0)

        @pl.when((nch - 1) % 2 == 1)
        def _():
            fn(nch - 1, False)

        fn(0, True)

    def accumulate():
        l_scr[...] = jnp.zeros(l_scr.shape, F32)
        acc_scr[...] = jnp.zeros(acc_scr.shape, F32)
        sweep(sum_pass)

    accumulate()
    smallest = jnp.min(jnp.sum(l_scr[...], axis=-1, keepdims=True))

    @pl.when(jnp.logical_not(smallest >= DSA_MIN_DENOM))
    def _():
        m_scr[...] = jnp.full(m_scr.shape, NEG_BIG, F32)
        sweep(max_pass)
        m_scr[...] = jnp.broadcast_to(jnp.max(m_scr[...], axis=-1, keepdims=True), m_scr.shape)
        accumulate()

    outs = []
    for h in range(B_HEADS):
        l = jnp.sum(l_scr[h], axis=-1, keepdims=True)
        o_lat = (acc_scr[h * QB:(h + 1) * QB, :] / l).astype(BF16)
        outs.append(jnp.dot(o_lat, wuv_ref[h], preferred_element_type=F32))
    o_ref[0] = jnp.concatenate(outs, axis=-1).astype(o_ref.dtype)


def _dsa(proj3, w_uk, w_uv, bias_b, kv_norm_g):
    bn, seq, _ = proj3.shape
    nblk = seq // QB
    spad = seq + QB
    qw = B_HEADS * HEAD_DIM
    iw = IDX_HEADS * LANES
    key_norm = jnp.max(jnp.abs(kv_norm_g.astype(F32))) * (KV_RANK ** 0.5 * BF16_SLACK)
    cb = jnp.broadcast_to(key_norm, (1, LANES))
    mb = jnp.broadcast_to(jnp.maximum(jnp.max(bias_b, axis=(1, 2)), 0.0)[:, None, None],
                          (B_HEADS, 1, LANES))
    return pl.pallas_call(
        functools.partial(_dsa_kernel, topk=min(TOPK_MAX, seq // 4)),
        grid=(bn, nblk),
        in_specs=[
            pl.BlockSpec((1, QB, qw), lambda b, n: (b, n, EV_QB // qw)),
            pl.BlockSpec((1, QB, iw), lambda b, n: (b, n, EV_QI // iw)),
            pl.BlockSpec((1, QB, LANES), lambda b, n: (b, n, EV_KW // LANES)),
            pl.BlockSpec((1, seq, KV_RANK), lambda b, n: (b, 0, EV_CL // KV_RANK)),
            pl.BlockSpec((1, seq, LANES), lambda b, n: (b, 0, EV_KW // LANES)),
            pl.BlockSpec(w_uk.shape, lambda b, n: (0, 0, 0)),
            pl.BlockSpec(w_uv.shape, lambda b, n: (0, 0, 0)),
            pl.BlockSpec((B_HEADS, QB, KCH), lambda b, n: (0, 0, 0)),
            pl.BlockSpec((1, LANES), lambda b, n: (0, 0)),
            pl.BlockSpec((B_HEADS, 1, LANES), lambda b, n: (0, 0, 0)),
        ],
        out_specs=pl.BlockSpec((1, QB, qw), lambda b, n: (b, n, 0)),
        out_shape=jax.ShapeDtypeStruct((bn, seq, qw), BF16),
        scratch_shapes=[
            pltpu.VMEM((IDX_HEADS * QB, LANES), BF16),
            pltpu.VMEM((IDX_HEADS, SUBLANES, LANES), F32),
            pltpu.VMEM((spad, LANES), I32),
            pltpu.VMEM((QB, spad), F32),
            pltpu.VMEM((B_HEADS * QB, KV_RANK), BF16),
            pltpu.VMEM((B_HEADS, QB, LANES), F32),
            pltpu.VMEM((B_HEADS, QB, LANES), F32),
            pltpu.VMEM((B_HEADS * QB, KV_RANK), F32),
            pltpu.VMEM((SUBLANES, LANES), I32),
            pltpu.VMEM((spad, KV_RANK), BF16),
            pltpu.VMEM((spad, LANES), BF16),
        ],
        compiler_params=_cparams(("arbitrary", "arbitrary")),
        name="dsa",
    )(proj3, proj3, proj3, proj3, proj3, w_uk, w_uv, bias_b, cb, mb)


SB_HG = 8


def _sb_kernel(q_ref, k_ref, v_ref, o_ref, acc_scr, carry_scr):
    n = pl.program_id(2)
    scale = C_HEAD_DIM ** -0.5
    rows = SB_HG * QB

    def sums_operand(width):
        r = lax.broadcasted_iota(I32, (width, width), 0)
        c = lax.broadcasted_iota(I32, (width, width), 1)
        suffix = jnp.where(r > c, 1.0, 0.0).astype(BF16)
        sums = jnp.concatenate([suffix, jnp.ones((width, QB), BF16)], axis=1)
        return jnp.concatenate([sums, sums], axis=0)

    def head_cols(h):
        return slice(h * C_HEAD_DIM, (h + 1) * C_HEAD_DIM)

    def chunk(ks, width, diagonal):
        z = jnp.concatenate(
            [lax.dot_general(q_ref[0, :, head_cols(h)], k_ref[0, pl.ds(ks, width), head_cols(h)],
                             (((1,), (1,)), ((), ())), preferred_element_type=F32)
             for h in range(SB_HG)], axis=0) * scale
        sp = jnp.maximum(z, 0.0) + jnp.log(1.0 + jnp.exp(-jnp.abs(z)))
        if diagonal:
            key = lax.broadcasted_iota(I32, (rows, width), 1) - (width - QB)
            earlier = key < (lax.broadcasted_iota(I32, (rows, width), 0) % QB)
            lk = jnp.where(earlier, -sp, 0.0)
        else:
            lk = -sp
        hi = lk.astype(BF16)
        lo = (lk - hi.astype(F32)).astype(BF16)
        inner = jnp.dot(jnp.concatenate([hi, lo], axis=1), sums_operand(width),
                        preferred_element_type=F32)
        carry = carry_scr[...]
        a = jnp.exp((z - sp) + (jnp.concatenate([carry] * (width // QB), axis=1) + inner[:, :width]))
        if diagonal:
            a = jnp.where(earlier, a, 0.0)
        a = a.astype(BF16)
        for h in range(SB_HG):
            r = slice(h * QB, (h + 1) * QB)
            acc_scr[r, :] += jnp.dot(a[r], v_ref[0, pl.ds(ks, width), head_cols(h)],
                                     preferred_element_type=F32)
        carry = carry + inner[:, width:]
        carry_scr[...] = carry
        return (jnp.max(carry) < SB_DONE).astype(I32)

    acc_scr[...] = jnp.zeros(acc_scr.shape, F32)
    carry_scr[...] = jnp.zeros(carry_scr.shape, F32)

    @pl.when(n % 2 == 0)
    def _():
        chunk(pl.multiple_of(n * QB, QB), QB, True)

    @pl.when(n % 2 == 1)
    def _():
        chunk(pl.multiple_of((n - 1) * QB, QB), 2 * QB, True)

    def cond(state):
        j, done = state
        return (j >= 1) & (done == 0)

    def body(state):
        j, _ = state
        return j - 2, chunk(pl.multiple_of((j - 1) * QB, QB), 2 * QB, False)

    lax.while_loop(cond, body, (n - 1 - n % 2, jnp.int32(0)))
    o_ref[0] = jnp.concatenate([acc_scr[h * QB:(h + 1) * QB, :] for h in range(SB_HG)],
                               axis=-1).astype(o_ref.dtype)


def _sb(qkv3):
    bn, seq, _ = qkv3.shape
    nblk = seq // QB
    gw = SB_HG * C_HEAD_DIM
    ngrp = C_HEADS // SB_HG
    return pl.pallas_call(
        _sb_kernel,
        grid=(bn, ngrp, nblk),
        in_specs=[
            pl.BlockSpec((1, QB, gw), lambda b, g, n: (b, n, g)),
            pl.BlockSpec((1, seq, gw), lambda b, g, n: (b, 0, ngrp + g)),
            pl.BlockSpec((1, seq, gw), lambda b, g, n: (b, 0, 2 * ngrp + g)),
        ],
        out_specs=pl.BlockSpec((1, QB, gw), lambda b, g, n: (b, n, g)),
        out_shape=jax.ShapeDtypeStruct((bn, seq, C_HEADS * C_HEAD_DIM), BF16),
        scratch_shapes=[pltpu.VMEM((SB_HG * QB, C_HEAD_DIM), F32),
                        pltpu.VMEM((SB_HG * QB, QB), F32)],
        compiler_params=_cparams(("arbitrary", "arbitrary", "arbitrary")),
        name="stick_breaking",
    )(qkv3, qkv3, qkv3)


def _t5_bucket(rel):
    half = NUM_BUCKETS // 2
    max_exact = half // 2
    n = jnp.abs(rel)
    nf = jnp.maximum(n, 1).astype(F32)
    large = max_exact + (jnp.log(nf / max_exact) / math.log(MAX_DISTANCE / max_exact)
                         * (half - max_exact)).astype(I32)
    large = jnp.minimum(large, half - 1)
    return jnp.where(rel > 0, half, 0) + jnp.where(n < max_exact, n, large)


def _band_bias(rel_bias):
    rel = (jnp.arange(2 * QB) - QB)[None, :] - jnp.arange(QB)[:, None]
    table = rel_bias.astype(F32)
    bucket = _t5_bucket(rel)[None]
    band = sum(jnp.where(bucket == k, table[k][:, None, None], 0.0) for k in range(NUM_BUCKETS))
    far = table[_t5_bucket(jnp.int32(-(QB + 1)))]
    return band, far


def _even_w_in(w_in):
    d = w_in.shape[0]
    sizes = (A_HEADS * HEAD_DIM, A_KV_HEADS * HEAD_DIM, A_KV_HEADS * HEAD_DIM,
             B_HEADS * HEAD_DIM, KV_RANK, IDX_HEADS * IDX_DIM, IDX_DIM, IDX_HEADS)
    offs = [int(o) for o in np.cumsum((0,) + sizes)]
    w = w_in.astype(BF16)
    moves = [(offs[0], offs[1], EV_QA), (offs[1], offs[2], EV_KA), (offs[2], offs[3], EV_VA),
             (offs[3], offs[4], EV_QB), (offs[4], offs[5], EV_CL), (offs[6], offs[8], EV_KW)]
    moves += [(offs[5] + h * IDX_DIM, offs[5] + (h + 1) * IDX_DIM, EV_QI + h * LANES)
              for h in range(IDX_HEADS)]
    out = jnp.zeros((d, EV_WIDTH), BF16)
    for lo, hi, dst in moves:
        out = lax.dynamic_update_slice(out, w[:, lo:hi], (0, dst))
    return out


def kernel(x, c, rel_bias, ada_w, ada_b, norm_mix_g, norm_ffn_g, ev_w_in, ev_kv_norm_g, ev_w_uk,
           ev_w_uv, ev_sinks, ev_w_out, od_w_in, od_w_out, ffn_w_up, ffn_conv_w, ffn_conv_b,
           ffn_w_down, final_g):
    bn, seq, d = x.shape
    depth = ada_w.shape[0]
    x2 = x.reshape(bn * seq, d)

    mod = _ada_mod(c, ada_w, ada_b)
    band, far = _band_bias(rel_bias)
    bias_a = band[:A_HEADS]
    bias_b = (band[A_HEADS:] - far[A_HEADS:, None, None]) * LOG2E
    w_up_all = ffn_w_up.astype(BF16)
    w_down_all = ffn_w_down.astype(BF16)

    for i in range(depth):
        sh1, sc1, g1, sh2, sc2, g2 = [m.reshape(bn, 1, d) for m in jnp.split(mod[i], 6, axis=-1)]
        g_mix = norm_mix_g[i].reshape(1, d)
        j = i // 2
        if i % 2 == 0:
            proj = _in_proj(x2, g_mix, sc1, sh1, _even_w_in(ev_w_in[j]),
                            ev_kv_norm_g[j].reshape(1, KV_RANK), seq, tm=1024, tn=1024, kv_col=EV_CL)
            proj3 = proj.reshape(bn, seq, EV_WIDTH)
            o_a = _swa(proj3, bias_a, ev_sinks[j])
            o_b = _dsa(proj3, ev_w_uk[j].astype(BF16), ev_w_uv[j].astype(BF16), bias_b, ev_kv_norm_g[j])
            w_out = ev_w_out[j].astype(BF16)
            wa = w_out[:A_HEADS * HEAD_DIM]
            wb = w_out[A_HEADS * HEAD_DIM:]
            x2 = _out_proj([o_a.reshape(bn * seq, -1), o_b.reshape(bn * seq, -1)], [wa, wb],
                           x2, g1, seq, tm=512)
        else:
            qkv = _in_proj(x2, g_mix, sc1, sh1, od_w_in[j].astype(BF16),
                           jnp.ones((1, KV_RANK), F32), seq, tm=1024, tn=1024)
            o_c = _sb(qkv.reshape(bn, seq, -1))
            x2 = _out_proj([o_c.reshape(bn * seq, -1)], [od_w_out[j].astype(BF16)], x2, g1, seq, tm=512)
        conv4 = jnp.concatenate([ffn_conv_w[i], ffn_conv_b[i][None]], axis=0)
        x2 = _conv_ffn(x2, norm_ffn_g[i].reshape(1, d), sc2, sh2, g2, w_up_all, conv4, w_down_all,
                       final_g.reshape(1, d), seq, layer=i, tm=1024, tf=512, final=(i == depth - 1))
    return x2.reshape(bn, seq, d)
```

```python
import functools
import math

import jax
import jax.numpy as jnp
import numpy as np
from jax import lax
from jax.experimental import pallas as pl
from jax.experimental.pallas import tpu as pltpu

F32 = jnp.float32
BF16 = jnp.bfloat16
I32 = jnp.int32

CHUNK = 64
QB = 128
HEAD_DIM = 64
A_HEADS = 16
A_KV_HEADS = 4
A_GROUP = A_HEADS // A_KV_HEADS
WINDOW_CHUNKS = 2
B_HEADS = 16
KV_RANK = 256
IDX_HEADS = 8
IDX_DIM = 64
TOPK_MAX = 256
C_HEADS = 16
C_HEAD_DIM = 128
NUM_BUCKETS = 32
MAX_DISTANCE = 128
CONV_W = 3
EPS = 1e-6

LANES = 128
SUBLANES = 8
MXU_DIM = 256
VMEM_LIMIT = 56 * 1024 * 1024

NEG_BIG = -1e30
LOG2E = math.log2(math.e)
INT_MIN = -(2 ** 31)
SB_DONE = -88.0

EV_QA = 0
EV_QB = EV_QA + A_HEADS * HEAD_DIM
EV_QI = EV_QB + B_HEADS * HEAD_DIM
EV_KA = EV_QI + IDX_HEADS * LANES
EV_VA = EV_KA + A_KV_HEADS * HEAD_DIM
EV_CL = EV_VA + A_KV_HEADS * HEAD_DIM
EV_KW = EV_CL + KV_RANK
EV_WIDTH = 4096


def _cparams(sem):
    return pltpu.CompilerParams(dimension_semantics=sem, vmem_limit_bytes=VMEM_LIMIT)


NORM_RB = 16
NORM_UNROLL = 8


def _row_blocks(n_rows, fn):
    def body(r, _):
        fn(pl.ds(pl.multiple_of(r * NORM_RB, NORM_RB), NORM_RB))
        return 0
    lax.fori_loop(0, n_rows // NORM_RB, body, 0, unroll=NORM_UNROLL)


def _norm_mod_to(h_scr, x_ref, g_ref, sc_ref, sh_ref):
    def block(rows):
        x = x_ref[rows, :]
        ms = jnp.mean(x * x, axis=-1, keepdims=True)
        y = x * lax.rsqrt(ms + EPS)
        h_scr[rows, :] = ((y * g_ref[...]) * (1.0 + sc_ref[0]) + sh_ref[0]).astype(BF16)
    _row_blocks(x_ref.shape[0], block)


def _silu(x):
    return x / (1.0 + jnp.exp(-x))


def _ada_kernel(c_ref, w_ref, b_ref, o_ref):
    a = _silu(c_ref[...]).astype(BF16)
    o_ref[0] = jnp.dot(a, w_ref[0].astype(BF16), preferred_element_type=F32) + b_ref[0]


def _ada_mod(c, ada_w, ada_b):
    depth, d, n = ada_w.shape
    bn = c.shape[0]
    rows = -(-bn // SUBLANES) * SUBLANES
    c_pad = jnp.pad(c, ((0, rows - bn), (0, 0)))
    tn = 1024
    out = pl.pallas_call(
        _ada_kernel,
        grid=(depth, n // tn),
        in_specs=[
            pl.BlockSpec((rows, d), lambda l, j: (0, 0)),
            pl.BlockSpec((1, d, tn), lambda l, j: (l, 0, j)),
            pl.BlockSpec((1, 1, tn), lambda l, j: (l, 0, j)),
        ],
        out_specs=pl.BlockSpec((1, rows, tn), lambda l, j: (l, 0, j)),
        out_shape=jax.ShapeDtypeStruct((depth, rows, n), F32),
        compiler_params=_cparams(("arbitrary", "arbitrary")),
        name="ada_mod",
    )(c_pad, ada_w, ada_b.reshape(depth, 1, n))
    return out[:, :bn]


def _in_proj_kernel(x_ref, g_ref, sc_ref, sh_ref, w_ref, kvg_ref, o_ref, h_scr, *, kv_tile, kv_off):
    j = pl.program_id(1)

    @pl.when(j == 0)
    def _():
        _norm_mod_to(h_scr, x_ref, g_ref, sc_ref, sh_ref)

    def project():
        return jnp.dot(h_scr[...], w_ref[...], preferred_element_type=F32)

    if kv_tile is None:
        o_ref[...] = project().astype(o_ref.dtype)
    else:
        @pl.when(j != kv_tile)
        def _():
            o_ref[...] = project().astype(o_ref.dtype)

        @pl.when(j == kv_tile)
        def _():
            acc = project()
            lat = acc[:, kv_off:kv_off + KV_RANK]
            ms = jnp.mean(lat * lat, axis=-1, keepdims=True)
            lat = lat * lax.rsqrt(ms + EPS) * kvg_ref[...]
            o_ref[...] = acc.astype(o_ref.dtype)
            o_ref[:, kv_off:kv_off + KV_RANK] = lat.astype(o_ref.dtype)


def _in_proj(x2, g, sc, sh, w, kv_g, seq, *, tm, tn, kv_col=None):
    n_rows, d = x2.shape
    n_out = w.shape[1]
    tiles_per_batch = seq // tm
    if kv_col is None:
        kv_tile, kv_off = None, 0
    else:
        kv_tile, kv_off = kv_col // tn, kv_col % tn
    kern = functools.partial(_in_proj_kernel, kv_tile=kv_tile, kv_off=kv_off)
    return pl.pallas_call(
        kern,
        grid=(n_rows // tm, n_out // tn),
        in_specs=[
            pl.BlockSpec((tm, d), lambda i, j: (i, 0)),
            pl.BlockSpec((1, d), lambda i, j: (0, 0)),
            pl.BlockSpec((1, 1, d), lambda i, j: (i // tiles_per_batch, 0, 0)),
            pl.BlockSpec((1, 1, d), lambda i, j: (i // tiles_per_batch, 0, 0)),
            pl.BlockSpec((d, tn), lambda i, j: (0, j)),
            pl.BlockSpec((1, KV_RANK), lambda i, j: (0, 0)),
        ],
        out_specs=pl.BlockSpec((tm, tn), lambda i, j: (i, j)),
        out_shape=jax.ShapeDtypeStruct((n_rows, n_out), BF16),
        scratch_shapes=[pltpu.VMEM((tm, d), BF16)],
        compiler_params=_cparams(("arbitrary", "arbitrary")),
        name="in_proj",
    )(x2, g, sc, sh, w, kv_g)


def _out_proj_kernel(*refs, n_act):
    acts = refs[:n_act]
    ws = refs[n_act:2 * n_act]
    x_ref, gate_ref, o_ref = refs[2 * n_act:]
    y = jnp.dot(acts[0][...], ws[0][...], preferred_element_type=F32)
    for a_ref, w_ref in zip(acts[1:], ws[1:]):
        y = y + jnp.dot(a_ref[...], w_ref[...], preferred_element_type=F32)
    o_ref[...] = x_ref[...] + gate_ref[0] * y


def _out_proj(acts, ws, x2, gate, seq, *, tm):
    n_rows, d = x2.shape
    tiles_per_batch = seq // tm
    n_act = len(acts)
    in_specs = ([pl.BlockSpec((tm, a.shape[1]), lambda i: (i, 0)) for a in acts]
                + [pl.BlockSpec(w.shape, lambda i: (0, 0)) for w in ws]
                + [pl.BlockSpec((tm, d), lambda i: (i, 0)),
                   pl.BlockSpec((1, 1, d), lambda i: (i // tiles_per_batch, 0, 0))])
    return pl.pallas_call(
        functools.partial(_out_proj_kernel, n_act=n_act),
        grid=(n_rows // tm,),
        in_specs=in_specs,
        out_specs=pl.BlockSpec((tm, d), lambda i: (i, 0)),
        out_shape=jax.ShapeDtypeStruct((n_rows, d), F32),
        compiler_params=_cparams(("arbitrary",)),
        name="out_proj",
    )(*acts, *ws, x2, gate)


FFN_RB = 64


def _ffn_kernel(x_ref, g_ref, sc_ref, sh_ref, gate_ref, wg_ref, wv_ref, cg_ref, cv_ref, wd_ref,
                wdl_ref, fg_ref, o_ref, h_scr, ug_scr, uv_scr, a0_scr, a1_scr, carry_g, carry_v, *,
                nf, tiles_per_batch, final):
    i = pl.program_id(0)
    f = pl.program_id(1)
    tm = x_ref.shape[0]

    @pl.when(f == 0)
    def _():
        _norm_mod_to(h_scr, x_ref, g_ref, sc_ref, sh_ref)
        o_ref[...] = jnp.zeros(o_ref.shape, F32)
        a1_scr[...] = jnp.zeros(a1_scr.shape, BF16)

    batch_start = (i % tiles_per_batch) == 0

    @pl.when(batch_start)
    def _():
        ug_scr[0:SUBLANES, :] = jnp.zeros((SUBLANES, ug_scr.shape[1]), F32)
        uv_scr[0:SUBLANES, :] = jnp.zeros((SUBLANES, uv_scr.shape[1]), F32)

    @pl.when(jnp.logical_not(batch_start))
    def _():
        ug_scr[0:SUBLANES, :] = carry_g[f]
        uv_scr[0:SUBLANES, :] = carry_v[f]

    def conv(u_scr, cw, base):
        y = cw[2:3] * u_scr[base:base + FFN_RB, :] + cw[3:4]
        y = y + cw[1:2] * u_scr[base - 1:base - 1 + FFN_RB, :]
        return y + cw[0:1] * u_scr[base - 2:base - 2 + FFN_RB, :]

    def step(a_cur, a_prev):
        ug_scr[SUBLANES:, :] = jnp.dot(h_scr[...], wg_ref[...], preferred_element_type=F32)
        uv_scr[SUBLANES:, :] = jnp.dot(h_scr[...], wv_ref[...], preferred_element_type=F32)
        o_ref[...] += jnp.dot(a_prev[...], wd_ref[...], preferred_element_type=F32)
        carry_g[f] = ug_scr[tm:tm + SUBLANES, :]
        carry_v[f] = uv_scr[tm:tm + SUBLANES, :]
        cwg = cg_ref[...]
        cwv = cv_ref[...]
        for r in range(tm // FFN_RB):
            base = SUBLANES + r * FFN_RB
            act = _silu(conv(ug_scr, cwg, base)) * conv(uv_scr, cwv, base)
            a_cur[r * FFN_RB:(r + 1) * FFN_RB, :] = act.astype(BF16)

    @pl.when(f % 2 == 0)
    def _():
        step(a0_scr, a1_scr)

    @pl.when(f % 2 == 1)
    def _():
        step(a1_scr, a0_scr)

    @pl.when(f == nf - 1)
    def _():
        a_last = a0_scr if (nf - 1) % 2 == 0 else a1_scr
        o_ref[...] += jnp.dot(a_last[...], wdl_ref[...], preferred_element_type=F32)

        def residual(rows):
            xo = x_ref[rows, :] + gate_ref[0] * o_ref[rows, :]
            if final:
                ms = jnp.mean(xo * xo, axis=-1, keepdims=True)
                xo = xo * lax.rsqrt(ms + EPS) * fg_ref[...]
            o_ref[rows, :] = xo
        _row_blocks(tm, residual)


def _conv_ffn(x2, g, sc, sh, gate, w_up, conv4, w_down, final_g, seq, *, layer, tm, tf, final):
    n_rows, d = x2.shape
    d_ff = w_down.shape[1]
    nf = d_ff // tf
    tiles_per_batch = seq // tm
    kern = functools.partial(_ffn_kernel, nf=nf, tiles_per_batch=tiles_per_batch, final=final)
    return pl.pallas_call(
        kern,
        grid=(n_rows // tm, nf),
        in_specs=[
            pl.BlockSpec((tm, d), lambda i, f: (i, 0), pipeline_mode=pl.Buffered(1)),
            pl.BlockSpec((1, d), lambda i, f: (0, 0)),
            pl.BlockSpec((1, 1, d), lambda i, f: (i // tiles_per_batch, 0, 0)),
            pl.BlockSpec((1, 1, d), lambda i, f: (i // tiles_per_batch, 0, 0)),
            pl.BlockSpec((1, 1, d), lambda i, f: (i // tiles_per_batch, 0, 0)),
            pl.BlockSpec((None, d, tf), lambda i, f: (layer, 0, f)),
            pl.BlockSpec((None, d, tf), lambda i, f: (layer, 0, f + nf)),
            pl.BlockSpec((CONV_W + 1, tf), lambda i, f: (0, f)),
            pl.BlockSpec((CONV_W + 1, tf), lambda i, f: (0, f + nf)),
            pl.BlockSpec((None, tf, d), lambda i, f: (layer, jnp.maximum(f - 1, 0), 0)),
            pl.BlockSpec((None, tf, d), lambda i, f: (layer, nf - 1, 0)),
            pl.BlockSpec((1, d), lambda i, f: (0, 0)),
        ],
        out_specs=pl.BlockSpec((tm, d), lambda i, f: (i, 0)),
        out_shape=jax.ShapeDtypeStruct((n_rows, d), F32),
        scratch_shapes=[
            pltpu.VMEM((tm, d), BF16),
            pltpu.VMEM((tm + SUBLANES, tf), F32),
            pltpu.VMEM((tm + SUBLANES, tf), F32),
            pltpu.VMEM((tm, tf), BF16),
            pltpu.VMEM((tm, tf), BF16),
            pltpu.VMEM((nf, SUBLANES, tf), F32),
            pltpu.VMEM((nf, SUBLANES, tf), F32),
        ],
        compiler_params=_cparams(("arbitrary", "arbitrary")),
        name="conv_ffn",
    )(x2, g, sc, sh, gate, w_up, w_up, conv4, conv4, w_down, w_down, final_g)


def _swa_kernel(q_ref, kp_ref, ko_ref, vp_ref, vo_ref, bias_ref, sink_ref, o_ref, s_scr, p_scr,
                den_scr):
    n = pl.program_id(1)
    row = lax.broadcasted_iota(I32, (QB, 2 * QB), 0)
    col = lax.broadcasted_iota(I32, (QB, 2 * QB), 1)
    qc = row // CHUNK
    kc = col // CHUNK - QB // CHUNK
    allowed = (kc <= qc) & (kc >= qc - WINDOW_CHUNKS) & ((col >= QB) | (n > 0))
    addmask = jnp.where(allowed, 0.0, NEG_BIG)
    k_all = jnp.concatenate([kp_ref[0], ko_ref[0]], axis=0)
    v_all = jnp.concatenate([vp_ref[0], vo_ref[0]], axis=0)
    q_all = q_ref[0] * (HEAD_DIM ** -0.5)
    low = lax.broadcasted_iota(I32, (QB, LANES), 1) < HEAD_DIM
    zero = jnp.zeros((QB, LANES), q_all.dtype)
    grp = A_GROUP * QB

    for g in range(A_KV_HEADS):
        kg = k_all[:, g * HEAD_DIM:(g + 1) * HEAD_DIM]
        k2 = jnp.concatenate([kg, kg], axis=1)
        parts = []
        for c in range(A_GROUP // 2):
            col0 = (g * A_GROUP + 2 * c) * HEAD_DIM
            q2 = q_all[:, col0:col0 + LANES]
            parts += [jnp.where(low, q2, zero), jnp.where(low, zero, q2)]
        s_scr[g * grp:(g + 1) * grp, :] = lax.dot_general(
            jnp.concatenate(parts, axis=0), k2, (((1,), (1,)), ((), ())), preferred_element_type=F32)

    s = s_scr[...].reshape(A_HEADS, QB, 2 * QB) + bias_ref[...] + addmask[None]
    sink = sink_ref[...][:, :, 0:1]
    m = jnp.maximum(jnp.max(s, axis=-1, keepdims=True), sink)
    p = jnp.exp(s - m).astype(BF16).reshape(A_HEADS * QB, 2 * QB)
    p_scr[...] = p
    total = jnp.dot(p, jnp.ones((2 * QB, LANES), BF16), preferred_element_type=F32)
    sink_term = jnp.broadcast_to(jnp.exp(sink - m), (A_HEADS, QB, LANES))
    den_scr[...] = total + sink_term.reshape(A_HEADS * QB, LANES)

    outs = []
    for g in range(A_KV_HEADS):
        vg = v_all[:, g * HEAD_DIM:(g + 1) * HEAD_DIM]
        v2 = jnp.concatenate([vg, vg], axis=1)
        rows = slice(g * grp, (g + 1) * grp)
        o = jnp.dot(p_scr[rows, :], v2, preferred_element_type=F32) / den_scr[rows, :]
        for c in range(A_GROUP // 2):
            even = o[(2 * c) * QB:(2 * c + 1) * QB]
            odd = o[(2 * c + 1) * QB:(2 * c + 2) * QB]
            outs.append(jnp.where(low, even, odd))
    o_ref[0] = jnp.concatenate(outs, axis=-1).astype(o_ref.dtype)


def _swa(proj3, bias_a, sinks):
    bn, seq, _ = proj3.shape
    nblk = seq // QB
    kvw = A_KV_HEADS * HEAD_DIM
    qw = A_HEADS * HEAD_DIM
    sink_b = jnp.broadcast_to(sinks.astype(F32)[:, None, None], (A_HEADS, QB, LANES))
    prev = lambda c: (lambda b, n: (b, jnp.maximum(n - 1, 0), c))
    own = lambda c: (lambda b, n: (b, n, c))
    return pl.pallas_call(
        _swa_kernel,
        grid=(bn, nblk),
        in_specs=[
            pl.BlockSpec((1, QB, qw), own(EV_QA // qw)),
            pl.BlockSpec((1, QB, kvw), prev(EV_KA // kvw)),
            pl.BlockSpec((1, QB, kvw), own(EV_KA // kvw)),
            pl.BlockSpec((1, QB, kvw), prev(EV_VA // kvw)),
            pl.BlockSpec((1, QB, kvw), own(EV_VA // kvw)),
            pl.BlockSpec((A_HEADS, QB, 2 * QB), lambda b, n: (0, 0, 0)),
            pl.BlockSpec((A_HEADS, QB, LANES), lambda b, n: (0, 0, 0)),
        ],
        out_specs=pl.BlockSpec((1, QB, qw), lambda b, n: (b, n, 0)),
        out_shape=jax.ShapeDtypeStruct((bn, seq, qw), BF16),
        scratch_shapes=[
            pltpu.VMEM((A_HEADS * QB, 2 * QB), F32),
            pltpu.VMEM((A_HEADS * QB, 2 * QB), BF16),
            pltpu.VMEM((A_HEADS * QB, LANES), F32),
        ],
        compiler_params=_cparams(("arbitrary", "arbitrary")),
        name="swa",
    )(proj3, proj3, proj3, proj3, proj3, bias_a, sink_b)


KCH = 2 * QB
DSA_MIN_DENOM = 2.0 ** -60
BF16_SLACK = 1.0 + 2.0 ** -7


def _dsa_kernel(qb_ref, qi_ref, kq_ref, ckv_ref, kk_ref, wuk_ref, wuv_ref, bias_ref, cb_ref,
                mb_ref, o_ref,
                qst_scr, wrow_scr, key_scr, mask_scr, qlat_scr, m_scr, l_scr, acc_scr, cut_scr,
                ckv_scr, kk_scr, *, topk):
    n = pl.program_id(1)
    nch = n // 2 + 1
    spad = key_scr.shape[0]

    @pl.when(n == 0)
    def _():
        ckv_scr[0:QB, :] = jnp.zeros((QB, ckv_scr.shape[1]), ckv_scr.dtype)
        kk_scr[0:QB, :] = jnp.zeros((QB, kk_scr.shape[1]), kk_scr.dtype)
        ckv_scr[QB:, :] = ckv_ref[0]
        kk_scr[QB:, :] = kk_ref[0]

    def chunk_start(e):
        return pl.multiple_of((n - 2 * e) * QB, QB)

    idx_scale = IDX_DIM ** -0.5 * IDX_HEADS ** -0.5
    w_t = jnp.transpose(kq_ref[0].astype(F32)) * idx_scale
    for h in range(IDX_HEADS):
        qst_scr[h * QB:(h + 1) * QB, :] = qi_ref[0, :, h * LANES:(h + 1) * LANES]
        wrow_scr[h] = jnp.broadcast_to(w_t[IDX_DIM + h:IDX_DIM + h + 1, :], (SUBLANES, LANES))
    q_all = qb_ref[0]
    for h in range(B_HEADS):
        ql = jnp.dot(q_all[:, h * HEAD_DIM:(h + 1) * HEAD_DIM], wuk_ref[h],
                     preferred_element_type=F32)
        ql = (ql * (HEAD_DIM ** -0.5 * LOG2E)).astype(BF16)
        qlat_scr[h * QB:(h + 1) * QB, :] = ql
        qf = ql.astype(F32)
        qn = jnp.sqrt(jnp.sum(qf * qf, axis=-1, keepdims=True))
        m_scr[h] = jnp.broadcast_to(qn, (QB, LANES)) * cb_ref[...] + mb_ref[h]

    key_row = lax.broadcasted_iota(I32, (KCH, LANES), 0)
    limit = (n * QB + CHUNK) + (lax.broadcasted_iota(I32, (1, LANES), 1) // CHUNK) * CHUNK

    def idx_chunk(e):
        ks = chunk_start(e)
        kj = kk_scr[pl.ds(ks, KCH), :]
        r = lax.dot_general(kj, qst_scr[...], (((1,), (1,)), ((), ())), preferred_element_type=F32)
        sc = jnp.maximum(r[:, 0:LANES], 0.0) * wrow_scr[0][0:1, :]
        for h in range(1, IDX_HEADS):
            sc = sc + jnp.maximum(r[:, h * LANES:(h + 1) * LANES], 0.0) * wrow_scr[h][0:1, :]
        bits = lax.bitcast_convert_type(sc, I32)
        key = bits ^ ((bits >> 31) & 0x7FFFFFFF)
        key = jnp.where(bits == INT_MIN, 0, key)
        kpos = ks - QB + key_row
        key = jnp.where((kpos >= 0) & (kpos < limit), key, INT_MIN)
        key_scr[pl.ds(ks, KCH), :] = key

    def idx_body(i, _):
        idx_chunk(2 * i)
        idx_chunk(jnp.minimum(2 * i + 1, nch - 1))
        return 0

    lax.fori_loop(0, (nch + 1) // 2, idx_body, 0)

    def count(pred):
        def hits(e):
            ks = chunk_start(e)
            hit = jnp.where(pred(key_scr[pl.ds(ks, KCH), :], ks + key_row), 1, 0)
            return jnp.sum(hit.reshape(KCH // SUBLANES, SUBLANES, LANES), axis=0)

        def pair(i, acc):
            return acc + (hits(2 * i) + hits(2 * i + 1))

        c = lax.fori_loop(0, nch // 2, pair, jnp.zeros((SUBLANES, LANES), I32))
        c = c + jnp.where(nch % 2 == 1, hits(nch - 1), 0)
        return jnp.sum(c, axis=0, keepdims=True)

    def bit_body(b, t):
        cand = t ^ jnp.left_shift(jnp.int32(1), 31 - b)
        tot = count(lambda k, pos: k >= cand)
        return jnp.where(tot >= topk, cand, t)

    thr = lax.fori_loop(0, 32, bit_body, jnp.full((1, LANES), INT_MIN, I32))
    thr = jnp.maximum(thr, INT_MIN + 1)

    n_gt = count(lambda k, pos: k > thr)
    n_eq = count(lambda k, pos: k == thr)
    need = topk - n_gt
    cut_scr[...] = jnp.full(cut_scr.shape, 2 * spad, I32)
    excess = jnp.max(jnp.where(n_eq > need, 1, 0))

    @pl.when(excess > 0)
    def _():
        nbits = int(spad).bit_length()

        def cut_body(b, c):
            cand = c | jnp.left_shift(jnp.int32(1), nbits - 1 - b)
            tot = count(lambda k, pos: (k == thr) & (pos < cand))
            return jnp.where(tot <= need, cand, c)

        c = lax.fori_loop(0, nbits, cut_body, jnp.zeros((1, LANES), I32))
        cut_scr[...] = jnp.broadcast_to(jnp.where(n_eq > need, c, 2 * spad), cut_scr.shape)

    cut = cut_scr[0:1, :]

    def mask_body(e, _):
        ks = chunk_start(e)
        k = key_scr[pl.ds(ks, KCH), :]
        sel = (k > thr) | ((k == thr) & (ks + key_row < cut))
        mask_scr[:, pl.ds(ks, KCH)] = jnp.transpose(jnp.where(sel, 0.0, NEG_BIG))
        return 0

    lax.fori_loop(0, nch, mask_body, 0)

    def masked_scores(e, near):
        ks = chunk_start(e)
        kv = ckv_scr[pl.ds(ks, KCH), :]
        addmask = mask_scr[:, pl.ds(ks, KCH)]
        s = lax.dot_general(qlat_scr[...], kv, (((1,), (1,)), ((), ())), preferred_element_type=F32)
        s = s.reshape(B_HEADS, QB, KCH) + addmask[None]
        if near:
            s = s + bias_ref[...]
        return s, kv

    def max_pass(e, near):
        s, _ = masked_scores(e, near)
        m_scr[...] = jnp.maximum(m_scr[...], jnp.maximum(s[:, :, :LANES], s[:, :, LANES:]))

    def sum_pass(e, near):
        s, kv = masked_scores(e, near)
        m = m_scr[...]
        p_lo = jnp.exp2(s[:, :, :LANES] - m)
        p_hi = jnp.exp2(s[:, :, LANES:] - m)
        l_scr[...] += p_lo + p_hi
        p = jnp.concatenate([p_lo, p_hi], axis=-1).astype(BF16).reshape(B_HEADS * QB, KCH)
        acc_scr[...] += jnp.dot(p, kv, preferred_element_type=F32)

    def sweep(fn):
        def far_pair(i, _):
            fn(2 * i + 1, False)
            fn(2 * i + 2, False)
            return 0
        lax.fori_loop(0, (nch - 1) // 2, far_pair, 0)

        @pl.when((nch - 1) % 2 == 1)
        def _():
            fn(nch - 1, False)

        fn(0, True)

    def accumulate():
        l_scr[...] = jnp.zeros(l_scr.shape, F32)
        acc_scr[...] = jnp.zeros(acc_scr.shape, F32)
        sweep(sum_pass)

    accumulate()
    smallest = jnp.min(jnp.sum(l_scr[...], axis=-1, keepdims=True))

    @pl.when(jnp.logical_not(smallest >= DSA_MIN_DENOM))
    def _():
        m_scr[...] = jnp.full(m_scr.shape, NEG_BIG, F32)
        sweep(max_pass)
        m_scr[...] = jnp.broadcast_to(jnp.max(m_scr[...], axis=-1, keepdims=True), m_scr.shape)
        accumulate()

    outs = []
    for h in range(B_HEADS):
        l = jnp.sum(l_scr[h], axis=-1, keepdims=True)
        o_lat = (acc_scr[h * QB:(h + 1) * QB, :] / l).astype(BF16)
        outs.append(jnp.dot(o_lat, wuv_ref[h], preferred_element_type=F32))
    o_ref[0] = jnp.concatenate(outs, axis=-1).astype(o_ref.dtype)


def _dsa(proj3, w_uk, w_uv, bias_b, kv_norm_g):
    bn, seq, _ = proj3.shape
    nblk = seq // QB
    spad = seq + QB
    qw = B_HEADS * HEAD_DIM
    iw = IDX_HEADS * LANES
    key_norm = jnp.max(jnp.abs(kv_norm_g.astype(F32))) * (KV_RANK ** 0.5 * BF16_SLACK)
    cb = jnp.broadcast_to(key_norm, (1, LANES))
    mb = jnp.broadcast_to(jnp.maximum(jnp.max(bias_b, axis=(1, 2)), 0.0)[:, None, None],
                          (B_HEADS, 1, LANES))
    return pl.pallas_call(
        functools.partial(_dsa_kernel, topk=min(TOPK_MAX, seq // 4)),
        grid=(bn, nblk),
        in_specs=[
            pl.BlockSpec((1, QB, qw), lambda b, n: (b, n, EV_QB // qw)),
            pl.BlockSpec((1, QB, iw), lambda b, n: (b, n, EV_QI // iw)),
            pl.BlockSpec((1, QB, LANES), lambda b, n: (b, n, EV_KW // LANES)),
            pl.BlockSpec((1, seq, KV_RANK), lambda b, n: (b, 0, EV_CL // KV_RANK)),
            pl.BlockSpec((1, seq, LANES), lambda b, n: (b, 0, EV_KW // LANES)),
            pl.BlockSpec(w_uk.shape, lambda b, n: (0, 0, 0)),
            pl.BlockSpec(w_uv.shape, lambda b, n: (0, 0, 0)),
            pl.BlockSpec((B_HEADS, QB, KCH), lambda b, n: (0, 0, 0)),
            pl.BlockSpec((1, LANES), lambda b, n: (0, 0)),
            pl.BlockSpec((B_HEADS, 1, LANES), lambda b, n: (0, 0, 0)),
        ],
        out_specs=pl.BlockSpec((1, QB, qw), lambda b, n: (b, n, 0)),
        out_shape=jax.ShapeDtypeStruct((bn, seq, qw), BF16),
        scratch_shapes=[
            pltpu.VMEM((IDX_HEADS * QB, LANES), BF16),
            pltpu.VMEM((IDX_HEADS, SUBLANES, LANES), F32),
            pltpu.VMEM((spad, LANES), I32),
            pltpu.VMEM((QB, spad), F32),
            pltpu.VMEM((B_HEADS * QB, KV_RANK), BF16),
            pltpu.VMEM((B_HEADS, QB, LANES), F32),
            pltpu.VMEM((B_HEADS, QB, LANES), F32),
            pltpu.VMEM((B_HEADS * QB, KV_RANK), F32),
            pltpu.VMEM((SUBLANES, LANES), I32),
            pltpu.VMEM((spad, KV_RANK), BF16),
            pltpu.VMEM((spad, LANES), BF16),
        ],
        compiler_params=_cparams(("arbitrary", "arbitrary")),
        name="dsa",
    )(proj3, proj3, proj3, proj3, proj3, w_uk, w_uv, bias_b, cb, mb)


SB_HG = 8


def _sb_kernel(q_ref, k_ref, v_ref, o_ref, acc_scr, carry_scr):
    n = pl.program_id(2)
    scale = C_HEAD_DIM ** -0.5
    rows = SB_HG * QB

    def sums_operand(width):
        r = lax.broadcasted_iota(I32, (width, width), 0)
        c = lax.broadcasted_iota(I32, (width, width), 1)
        suffix = jnp.where(r > c, 1.0, 0.0).astype(BF16)
        sums = jnp.concatenate([suffix, jnp.ones((width, QB), BF16)], axis=1)
        return jnp.concatenate([sums, sums], axis=0)

    def head_cols(h):
        return slice(h * C_HEAD_DIM, (h + 1) * C_HEAD_DIM)

    def chunk(ks, width, diagonal):
        z = jnp.concatenate(
            [lax.dot_general(q_ref[0, :, head_cols(h)], k_ref[0, pl.ds(ks, width), head_cols(h)],
                             (((1,), (1,)), ((), ())), preferred_element_type=F32)
             for h in range(SB_HG)], axis=0) * scale
        sp = jnp.maximum(z, 0.0) + jnp.log(1.0 + jnp.exp(-jnp.abs(z)))
        if diagonal:
            key = lax.broadcasted_iota(I32, (rows, width), 1) - (width - QB)
            earlier = key < (lax.broadcasted_iota(I32, (rows, width), 0) % QB)
            lk = jnp.where(earlier, -sp, 0.0)
        else:
            lk = -sp
        hi = lk.astype(BF16)
        lo = (lk - hi.astype(F32)).astype(BF16)
        inner = jnp.dot(jnp.concatenate([hi, lo], axis=1), sums_operand(width),
                        preferred_element_type=F32)
        carry = carry_scr[...]
        a = jnp.exp((z - sp) + (jnp.concatenate([carry] * (width // QB), axis=1) + inner[:, :width]))
        if diagonal:
            a = jnp.where(earlier, a, 0.0)
        a = a.astype(BF16)
        for h in range(SB_HG):
            r = slice(h * QB, (h + 1) * QB)
            acc_scr[r, :] += jnp.dot(a[r], v_ref[0, pl.ds(ks, width), head_cols(h)],
                                     preferred_element_type=F32)
        carry = carry + inner[:, width:]
        carry_scr[...] = carry
        return (jnp.max(carry) < SB_DONE).astype(I32)

    acc_scr[...] = jnp.zeros(acc_scr.shape, F32)
    carry_scr[...] = jnp.zeros(carry_scr.shape, F32)

    @pl.when(n % 2 == 0)
    def _():
        chunk(pl.multiple_of(n * QB, QB), QB, True)

    @pl.when(n % 2 == 1)
    def _():
        chunk(pl.multiple_of((n - 1) * QB, QB), 2 * QB, True)

    def cond(state):
        j, done = state
        return (j >= 1) & (done == 0)

    def body(state):
        j, _ = state
        return j - 2, chunk(pl.multiple_of((j - 1) * QB, QB), 2 * QB, False)

    lax.while_loop(cond, body, (n - 1 - n % 2, jnp.int32(0)))
    o_ref[0] = jnp.concatenate([acc_scr[h * QB:(h + 1) * QB, :] for h in range(SB_HG)],
                               axis=-1).astype(o_ref.dtype)


def _sb(qkv3):
    bn, seq, _ = qkv3.shape
    nblk = seq // QB
    gw = SB_HG * C_HEAD_DIM
    ngrp = C_HEADS // SB_HG
    return pl.pallas_call(
        _sb_kernel,
        grid=(bn, ngrp, nblk),
        in_specs=[
            pl.BlockSpec((1, QB, gw), lambda b, g, n: (b, n, g)),
            pl.BlockSpec((1, seq, gw), lambda b, g, n: (b, 0, ngrp + g)),
            pl.BlockSpec((1, seq, gw), lambda b, g, n: (b, 0, 2 * ngrp + g)),
        ],
        out_specs=pl.BlockSpec((1, QB, gw), lambda b, g, n: (b, n, g)),
        out_shape=jax.ShapeDtypeStruct((bn, seq, C_HEADS * C_HEAD_DIM), BF16),
        scratch_shapes=[pltpu.VMEM((SB_HG * QB, C_HEAD_DIM), F32),
                        pltpu.VMEM((SB_HG * QB, QB), F32)],
        compiler_params=_cparams(("arbitrary", "arbitrary", "arbitrary")),
        name="stick_breaking",
    )(qkv3, qkv3, qkv3)


def _t5_bucket(rel):
    half = NUM_BUCKETS // 2
    max_exact = half // 2
    n = jnp.abs(rel)
    nf = jnp.maximum(n, 1).astype(F32)
    large = max_exact + (jnp.log(nf / max_exact) / math.log(MAX_DISTANCE / max_exact)
                         * (half - max_exact)).astype(I32)
    large = jnp.minimum(large, half - 1)
    return jnp.where(rel > 0, half, 0) + jnp.where(n < max_exact, n, large)


def _band_bias(rel_bias):
    rel = (jnp.arange(2 * QB) - QB)[None, :] - jnp.arange(QB)[:, None]
    table = rel_bias.astype(F32)
    bucket = _t5_bucket(rel)[None]
    band = sum(jnp.where(bucket == k, table[k][:, None, None], 0.0) for k in range(NUM_BUCKETS))
    far = table[_t5_bucket(jnp.int32(-(QB + 1)))]
    return band, far


def _even_w_in(w_in):
    d = w_in.shape[0]
    sizes = (A_HEADS * HEAD_DIM, A_KV_HEADS * HEAD_DIM, A_KV_HEADS * HEAD_DIM,
             B_HEADS * HEAD_DIM, KV_RANK, IDX_HEADS * IDX_DIM, IDX_DIM, IDX_HEADS)
    offs = [int(o) for o in np.cumsum((0,) + sizes)]
    w = w_in.astype(BF16)
    moves = [(offs[0], offs[1], EV_QA), (offs[1], offs[2], EV_KA), (offs[2], offs[3], EV_VA),
             (offs[3], offs[4], EV_QB), (offs[4], offs[5], EV_CL), (offs[6], offs[8], EV_KW)]
    moves += [(offs[5] + h * IDX_DIM, offs[5] + (h + 1) * IDX_DIM, EV_QI + h * LANES)
              for h in range(IDX_HEADS)]
    out = jnp.zeros((d, EV_WIDTH), BF16)
    for lo, hi, dst in moves:
        out = lax.dynamic_update_slice(out, w[:, lo:hi], (0, dst))
    return out


def kernel(x, c, rel_bias, ada_w, ada_b, norm_mix_g, norm_ffn_g, ev_w_in, ev_kv_norm_g, ev_w_uk,
           ev_w_uv, ev_sinks, ev_w_out, od_w_in, od_w_out, ffn_w_up, ffn_conv_w, ffn_conv_b,
           ffn_w_down, final_g):
    bn, seq, d = x.shape
    depth = ada_w.shape[0]
    x2 = x.reshape(bn * seq, d)

    mod = _ada_mod(c, ada_w, ada_b)
    band, far = _band_bias(rel_bias)
    bias_a = band[:A_HEADS]
    bias_b = (band[A_HEADS:] - far[A_HEADS:, None, None]) * LOG2E
    w_up_all = ffn_w_up.astype(BF16)
    w_down_all = ffn_w_down.astype(BF16)

    for i in range(depth):
        sh1, sc1, g1, sh2, sc2, g2 = [m.reshape(bn, 1, d) for m in jnp.split(mod[i], 6, axis=-1)]
        g_mix = norm_mix_g[i].reshape(1, d)
        j = i // 2
        if i % 2 == 0:
            proj = _in_proj(x2, g_mix, sc1, sh1, _even_w_in(ev_w_in[j]),
                            ev_kv_norm_g[j].reshape(1, KV_RANK), seq, tm=1024, tn=1024, kv_col=EV_CL)
            proj3 = proj.reshape(bn, seq, EV_WIDTH)
            o_a = _swa(proj3, bias_a, ev_sinks[j])
            o_b = _dsa(proj3, ev_w_uk[j].astype(BF16), ev_w_uv[j].astype(BF16), bias_b, ev_kv_norm_g[j])
            w_out = ev_w_out[j].astype(BF16)
            wa = w_out[:A_HEADS * HEAD_DIM]
            wb = w_out[A_HEADS * HEAD_DIM:]
            x2 = _out_proj([o_a.reshape(bn * seq, -1), o_b.reshape(bn * seq, -1)], [wa, wb],
                           x2, g1, seq, tm=512)
        else:
            qkv = _in_proj(x2, g_mix, sc1, sh1, od_w_in[j].astype(BF16),
                           jnp.ones((1, KV_RANK), F32), seq, tm=1024, tn=1024)
            o_c = _sb(qkv.reshape(bn, seq, -1))
            x2 = _out_proj([o_c.reshape(bn * seq, -1)], [od_w_out[j].astype(BF16)], x2, g1, seq, tm=512)
        conv4 = jnp.concatenate([ffn_conv_w[i], ffn_conv_b[i][None]], axis=0)
        x2 = _conv_ffn(x2, norm_ffn_g[i].reshape(1, d), sc2, sh2, g2, w_up_all, conv4, w_down_all,
                       final_g.reshape(1, d), seq, layer=i, tm=1024, tf=512, final=(i == depth - 1))
    return x2.reshape(bn, seq, d)
```

```python
import functools
import math

import jax
import jax.numpy as jnp
import numpy as np
from jax import lax
from jax.experimental import pallas as pl
from jax.experimental.pallas import tpu as pltpu

F32 = jnp.float32
BF16 = jnp.bfloat16
I32 = jnp.int32

CHUNK = 64
QB = 128
HEAD_DIM = 64
A_HEADS = 16
A_KV_HEADS = 4
A_GROUP = A_HEADS // A_KV_HEADS
WINDOW_CHUNKS = 2
B_HEADS = 16
KV_RANK = 256
IDX_HEADS = 8
IDX_DIM = 64
TOPK_MAX = 256
C_HEADS = 16
C_HEAD_DIM = 128
NUM_BUCKETS = 32
MAX_DISTANCE = 128
CONV_W = 3
EPS = 1e-6

LANES = 128
SUBLANES = 8
MXU_DIM = 256
VMEM_LIMIT = 56 * 1024 * 1024

ADA_TN = 1024
IN_PROJ_TM, IN_PROJ_TN = 1024, 2048
OUT_PROJ_TM = 512
FFN_TM, FFN_TF = 1024, 512

NEG_BIG = -1e30
LOG2E = math.log2(math.e)
INT_MIN = -(2 ** 31)
SB_DONE = -88.0

EV_QA = 0
EV_QB = EV_QA + A_HEADS * HEAD_DIM
EV_QI = EV_QB + B_HEADS * HEAD_DIM
EV_KA = EV_QI + IDX_HEADS * LANES
EV_VA = EV_KA + A_KV_HEADS * HEAD_DIM
EV_CL = EV_VA + A_KV_HEADS * HEAD_DIM
EV_KW = EV_CL + KV_RANK
EV_WIDTH = 4096


def _cparams(sem):
    return pltpu.CompilerParams(dimension_semantics=sem, vmem_limit_bytes=VMEM_LIMIT)


NORM_RB = 16
NORM_UNROLL = 8


def _row_blocks(n_rows, fn):
    def body(r, _):
        fn(pl.ds(pl.multiple_of(r * NORM_RB, NORM_RB), NORM_RB))
        return 0
    lax.fori_loop(0, n_rows // NORM_RB, body, 0, unroll=NORM_UNROLL)


def _norm_mod_to(h_scr, x_ref, g_ref, sc_ref, sh_ref):
    def block(rows):
        x = x_ref[rows, :]
        ms = jnp.mean(x * x, axis=-1, keepdims=True)
        y = x * lax.rsqrt(ms + EPS)
        h_scr[rows, :] = ((y * g_ref[...]) * (1.0 + sc_ref[0]) + sh_ref[0]).astype(BF16)
    _row_blocks(x_ref.shape[0], block)


def _silu(x):
    return x / (1.0 + jnp.exp(-x))


def _ada_kernel(c_ref, w_ref, b_ref, o_ref):
    a = _silu(c_ref[...]).astype(BF16)
    o_ref[0] = jnp.dot(a, w_ref[0].astype(BF16), preferred_element_type=F32) + b_ref[0]


def _ada_mod(c, ada_w, ada_b):
    depth, d, n = ada_w.shape
    bn = c.shape[0]
    rows = -(-bn // SUBLANES) * SUBLANES
    c_pad = jnp.pad(c, ((0, rows - bn), (0, 0)))
    tn = ADA_TN
    out = pl.pallas_call(
        _ada_kernel,
        grid=(depth, n // tn),
        in_specs=[
            pl.BlockSpec((rows, d), lambda l, j: (0, 0)),
            pl.BlockSpec((1, d, tn), lambda l, j: (l, 0, j)),
            pl.BlockSpec((1, 1, tn), lambda l, j: (l, 0, j)),
        ],
        out_specs=pl.BlockSpec((1, rows, tn), lambda l, j: (l, 0, j)),
        out_shape=jax.ShapeDtypeStruct((depth, rows, n), F32),
        compiler_params=_cparams(("arbitrary", "arbitrary")),
        name="ada_mod",
    )(c_pad, ada_w, ada_b.reshape(depth, 1, n))
    return out[:, :bn]


def _in_proj_kernel(x_ref, g_ref, sc_ref, sh_ref, w_ref, kvg_ref, o_ref, h_scr, *, kv_tile, kv_off):
    j = pl.program_id(1)

    @pl.when(j == 0)
    def _():
        _norm_mod_to(h_scr, x_ref, g_ref, sc_ref, sh_ref)

    def project():
        return jnp.dot(h_scr[...], w_ref[...], preferred_element_type=F32)

    if kv_tile is None:
        o_ref[...] = project().astype(o_ref.dtype)
    else:
        @pl.when(j != kv_tile)
        def _():
            o_ref[...] = project().astype(o_ref.dtype)

        @pl.when(j == kv_tile)
        def _():
            acc = project()
            lat = acc[:, kv_off:kv_off + KV_RANK]
            ms = jnp.mean(lat * lat, axis=-1, keepdims=True)
            lat = lat * lax.rsqrt(ms + EPS) * kvg_ref[...]
            o_ref[...] = acc.astype(o_ref.dtype)
            o_ref[:, kv_off:kv_off + KV_RANK] = lat.astype(o_ref.dtype)


def _in_proj(x2, g, sc, sh, w, kv_g, seq, *, tm, tn, kv_col=None):
    n_rows, d = x2.shape
    n_out = w.shape[1]
    tiles_per_batch = seq // tm
    if kv_col is None:
        kv_tile, kv_off = None, 0
    else:
        kv_tile, kv_off = kv_col // tn, kv_col % tn
    kern = functools.partial(_in_proj_kernel, kv_tile=kv_tile, kv_off=kv_off)
    return pl.pallas_call(
        kern,
        grid=(n_rows // tm, n_out // tn),
        in_specs=[
            pl.BlockSpec((tm, d), lambda i, j: (i, 0), pipeline_mode=pl.Buffered(1)),
            pl.BlockSpec((1, d), lambda i, j: (0, 0)),
            pl.BlockSpec((1, 1, d), lambda i, j: (i // tiles_per_batch, 0, 0)),
            pl.BlockSpec((1, 1, d), lambda i, j: (i // tiles_per_batch, 0, 0)),
            pl.BlockSpec((d, tn), lambda i, j: (0, j)),
            pl.BlockSpec((1, KV_RANK), lambda i, j: (0, 0)),
        ],
        out_specs=pl.BlockSpec((tm, tn), lambda i, j: (i, j)),
        out_shape=jax.ShapeDtypeStruct((n_rows, n_out), BF16),
        scratch_shapes=[pltpu.VMEM((tm, d), BF16)],
        compiler_params=_cparams(("arbitrary", "arbitrary")),
        name="in_proj",
    )(x2, g, sc, sh, w, kv_g)


def _out_proj_kernel(*refs, n_act):
    acts = refs[:n_act]
    ws = refs[n_act:2 * n_act]
    x_ref, gate_ref, o_ref = refs[2 * n_act:]
    y = jnp.dot(acts[0][...], ws[0][...], preferred_element_type=F32)
    for a_ref, w_ref in zip(acts[1:], ws[1:]):
        y = y + jnp.dot(a_ref[...], w_ref[...], preferred_element_type=F32)
    o_ref[...] = x_ref[...] + gate_ref[0] * y


def _out_proj(acts, ws, x2, gate, seq, *, tm):
    n_rows, d = x2.shape
    tiles_per_batch = seq // tm
    n_act = len(acts)
    in_specs = ([pl.BlockSpec((tm, a.shape[1]), lambda i: (i, 0)) for a in acts]
                + [pl.BlockSpec(w.shape, lambda i: (0, 0)) for w in ws]
                + [pl.BlockSpec((tm, d), lambda i: (i, 0)),
                   pl.BlockSpec((1, 1, d), lambda i: (i // tiles_per_batch, 0, 0))])
    return pl.pallas_call(
        functools.partial(_out_proj_kernel, n_act=n_act),
        grid=(n_rows // tm,),
        in_specs=in_specs,
        out_specs=pl.BlockSpec((tm, d), lambda i: (i, 0)),
        out_shape=jax.ShapeDtypeStruct((n_rows, d), F32),
        compiler_params=_cparams(("arbitrary",)),
        name="out_proj",
    )(*acts, *ws, x2, gate)


FFN_RB = 64


def _ffn_kernel(x_ref, g_ref, sc_ref, sh_ref, gate_ref, wg_ref, wv_ref, cg_ref, cv_ref, wd_ref,
                wdl_ref, fg_ref, o_ref, h_scr, ug_scr, uv_scr, a0_scr, a1_scr, carry_g, carry_v, *,
                nf, tiles_per_batch, final):
    i = pl.program_id(0)
    f = pl.program_id(1)
    tm = x_ref.shape[0]

    @pl.when(f == 0)
    def _():
        _norm_mod_to(h_scr, x_ref, g_ref, sc_ref, sh_ref)
        o_ref[...] = jnp.zeros(o_ref.shape, F32)
        a1_scr[...] = jnp.zeros(a1_scr.shape, BF16)

    batch_start = (i % tiles_per_batch) == 0

    @pl.when(batch_start)
    def _():
        ug_scr[0:SUBLANES, :] = jnp.zeros((SUBLANES, ug_scr.shape[1]), F32)
        uv_scr[0:SUBLANES, :] = jnp.zeros((SUBLANES, uv_scr.shape[1]), F32)

    @pl.when(jnp.logical_not(batch_start))
    def _():
        ug_scr[0:SUBLANES, :] = carry_g[f]
        uv_scr[0:SUBLANES, :] = carry_v[f]

    def conv(u_scr, cw, base):
        y = cw[2:3] * u_scr[base:base + FFN_RB, :] + cw[3:4]
        y = y + cw[1:2] * u_scr[base - 1:base - 1 + FFN_RB, :]
        return y + cw[0:1] * u_scr[base - 2:base - 2 + FFN_RB, :]

    def step(a_cur, a_prev):
        ug_scr[SUBLANES:, :] = jnp.dot(h_scr[...], wg_ref[...], preferred_element_type=F32)
        uv_scr[SUBLANES:, :] = jnp.dot(h_scr[...], wv_ref[...], preferred_element_type=F32)
        o_ref[...] += jnp.dot(a_prev[...], wd_ref[...], preferred_element_type=F32)
        carry_g[f] = ug_scr[tm:tm + SUBLANES, :]
        carry_v[f] = uv_scr[tm:tm + SUBLANES, :]
        cwg = cg_ref[...]
        cwv = cv_ref[...]
        for r in range(tm // FFN_RB):
            base = SUBLANES + r * FFN_RB
            act = _silu(conv(ug_scr, cwg, base)) * conv(uv_scr, cwv, base)
            a_cur[r * FFN_RB:(r + 1) * FFN_RB, :] = act.astype(BF16)

    @pl.when(f % 2 == 0)
    def _():
        step(a0_scr, a1_scr)

    @pl.when(f % 2 == 1)
    def _():
        step(a1_scr, a0_scr)

    @pl.when(f == nf - 1)
    def _():
        a_last = a0_scr if (nf - 1) % 2 == 0 else a1_scr
        o_ref[...] += jnp.dot(a_last[...], wdl_ref[...], preferred_element_type=F32)

        def residual(rows):
            xo = x_ref[rows, :] + gate_ref[0] * o_ref[rows, :]
            if final:
                ms = jnp.mean(xo * xo, axis=-1, keepdims=True)
                xo = xo * lax.rsqrt(ms + EPS) * fg_ref[...]
            o_ref[rows, :] = xo
        _row_blocks(tm, residual)


def _conv_ffn(x2, g, sc, sh, gate, w_up, conv4, w_down, final_g, seq, *, layer, tm, tf, final):
    n_rows, d = x2.shape
    d_ff = w_down.shape[1]
    nf = d_ff // tf
    tiles_per_batch = seq // tm
    kern = functools.partial(_ffn_kernel, nf=nf, tiles_per_batch=tiles_per_batch, final=final)
    return pl.pallas_call(
        kern,
        grid=(n_rows // tm, nf),
        in_specs=[
            pl.BlockSpec((tm, d), lambda i, f: (i, 0), pipeline_mode=pl.Buffered(1)),
            pl.BlockSpec((1, d), lambda i, f: (0, 0)),
            pl.BlockSpec((1, 1, d), lambda i, f: (i // tiles_per_batch, 0, 0)),
            pl.BlockSpec((1, 1, d), lambda i, f: (i // tiles_per_batch, 0, 0)),
            pl.BlockSpec((1, 1, d), lambda i, f: (i // tiles_per_batch, 0, 0)),
            pl.BlockSpec((None, d, tf), lambda i, f: (layer, 0, f)),
            pl.BlockSpec((None, d, tf), lambda i, f: (layer, 0, f + nf)),
            pl.BlockSpec((CONV_W + 1, tf), lambda i, f: (0, f)),
            pl.BlockSpec((CONV_W + 1, tf), lambda i, f: (0, f + nf)),
            pl.BlockSpec((None, tf, d), lambda i, f: (layer, jnp.maximum(f - 1, 0), 0)),
            pl.BlockSpec((None, tf, d), lambda i, f: (layer, nf - 1, 0)),
            pl.BlockSpec((1, d), lambda i, f: (0, 0)),
        ],
        out_specs=pl.BlockSpec((tm, d), lambda i, f: (i, 0)),
        out_shape=jax.ShapeDtypeStruct((n_rows, d), F32),
        scratch_shapes=[
            pltpu.VMEM((tm, d), BF16),
            pltpu.VMEM((tm + SUBLANES, tf), F32),
            pltpu.VMEM((tm + SUBLANES, tf), F32),
            pltpu.VMEM((tm, tf), BF16),
            pltpu.VMEM((tm, tf), BF16),
            pltpu.VMEM((nf, SUBLANES, tf), F32),
            pltpu.VMEM((nf, SUBLANES, tf), F32),
        ],
        compiler_params=_cparams(("arbitrary", "arbitrary")),
        name="conv_ffn",
    )(x2, g, sc, sh, gate, w_up, w_up, conv4, conv4, w_down, w_down, final_g)


def _swa_kernel(q_ref, kp_ref, ko_ref, vp_ref, vo_ref, bias_ref, sink_ref, o_ref, s_scr, p_scr,
                den_scr):
    n = pl.program_id(1)
    row = lax.broadcasted_iota(I32, (QB, 2 * QB), 0)
    col = lax.broadcasted_iota(I32, (QB, 2 * QB), 1)
    qc = row // CHUNK
    kc = col // CHUNK - QB // CHUNK
    allowed = (kc <= qc) & (kc >= qc - WINDOW_CHUNKS) & ((col >= QB) | (n > 0))
    addmask = jnp.where(allowed, 0.0, NEG_BIG)
    k_all = jnp.concatenate([kp_ref[0], ko_ref[0]], axis=0)
    v_all = jnp.concatenate([vp_ref[0], vo_ref[0]], axis=0)
    q_all = q_ref[0] * (HEAD_DIM ** -0.5)
    low = lax.broadcasted_iota(I32, (QB, LANES), 1) < HEAD_DIM
    zero = jnp.zeros((QB, LANES), q_all.dtype)
    grp = A_GROUP * QB

    for g in range(A_KV_HEADS):
        kg = k_all[:, g * HEAD_DIM:(g + 1) * HEAD_DIM]
        k2 = jnp.concatenate([kg, kg], axis=1)
        parts = []
        for c in range(A_GROUP // 2):
            col0 = (g * A_GROUP + 2 * c) * HEAD_DIM
            q2 = q_all[:, col0:col0 + LANES]
            parts += [jnp.where(low, q2, zero), jnp.where(low, zero, q2)]
        s_scr[g * grp:(g + 1) * grp, :] = lax.dot_general(
            jnp.concatenate(parts, axis=0), k2, (((1,), (1,)), ((), ())), preferred_element_type=F32)

    s = s_scr[...].reshape(A_HEADS, QB, 2 * QB) + bias_ref[...] + addmask[None]
    sink = sink_ref[...][:, :, 0:1]
    m = jnp.maximum(jnp.max(s, axis=-1, keepdims=True), sink)
    p = jnp.exp(s - m).astype(BF16).reshape(A_HEADS * QB, 2 * QB)
    p_scr[...] = p
    total = jnp.dot(p, jnp.ones((2 * QB, LANES), BF16), preferred_element_type=F32)
    sink_term = jnp.broadcast_to(jnp.exp(sink - m), (A_HEADS, QB, LANES))
    den_scr[...] = total + sink_term.reshape(A_HEADS * QB, LANES)

    outs = []
    for g in range(A_KV_HEADS):
        vg = v_all[:, g * HEAD_DIM:(g + 1) * HEAD_DIM]
        v2 = jnp.concatenate([vg, vg], axis=1)
        rows = slice(g * grp, (g + 1) * grp)
        o = jnp.dot(p_scr[rows, :], v2, preferred_element_type=F32) / den_scr[rows, :]
        for c in range(A_GROUP // 2):
            even = o[(2 * c) * QB:(2 * c + 1) * QB]
            odd = o[(2 * c + 1) * QB:(2 * c + 2) * QB]
            outs.append(jnp.where(low, even, odd))
    o_ref[0] = jnp.concatenate(outs, axis=-1).astype(o_ref.dtype)


def _swa(proj3, bias_a, sinks):
    bn, seq, _ = proj3.shape
    nblk = seq // QB
    kvw = A_KV_HEADS * HEAD_DIM
    qw = A_HEADS * HEAD_DIM
    sink_b = jnp.broadcast_to(sinks.astype(F32)[:, None, None], (A_HEADS, QB, LANES))
    prev = lambda c: (lambda b, n: (b, jnp.maximum(n - 1, 0), c))
    own = lambda c: (lambda b, n: (b, n, c))
    return pl.pallas_call(
        _swa_kernel,
        grid=(bn, nblk),
        in_specs=[
            pl.BlockSpec((1, QB, qw), own(EV_QA // qw)),
            pl.BlockSpec((1, QB, kvw), prev(EV_KA // kvw)),
            pl.BlockSpec((1, QB, kvw), own(EV_KA // kvw)),
            pl.BlockSpec((1, QB, kvw), prev(EV_VA // kvw)),
            pl.BlockSpec((1, QB, kvw), own(EV_VA // kvw)),
            pl.BlockSpec((A_HEADS, QB, 2 * QB), lambda b, n: (0, 0, 0)),
            pl.BlockSpec((A_HEADS, QB, LANES), lambda b, n: (0, 0, 0)),
        ],
        out_specs=pl.BlockSpec((1, QB, qw), lambda b, n: (b, n, 0)),
        out_shape=jax.ShapeDtypeStruct((bn, seq, qw), BF16),
        scratch_shapes=[
            pltpu.VMEM((A_HEADS * QB, 2 * QB), F32),
            pltpu.VMEM((A_HEADS * QB, 2 * QB), BF16),
            pltpu.VMEM((A_HEADS * QB, LANES), F32),
        ],
        compiler_params=_cparams(("arbitrary", "arbitrary")),
        name="swa",
    )(proj3, proj3, proj3, proj3, proj3, bias_a, sink_b)


KCH = MXU_DIM
DSA_MIN_DENOM = 2.0 ** -60
BF16_SLACK = 1.0 + 2.0 ** -7


def _dsa_kernel(qb_ref, qi_ref, kq_ref, ckv_ref, kk_ref, wuk_ref, wuv_ref, bias_ref, cb_ref,
                mb_ref, o_ref,
                qst_scr, wrow_scr, key_scr, mask_scr, qlat_scr, m_scr, l_scr, acc_scr, cut_scr,
                ckv_scr, kk_scr, *, topk):
    n = pl.program_id(1)
    nch = n // 2 + 1
    spad = key_scr.shape[0]

    @pl.when(n == 0)
    def _():
        ckv_scr[0:QB, :] = jnp.zeros((QB, ckv_scr.shape[1]), ckv_scr.dtype)
        kk_scr[0:QB, :] = jnp.zeros((QB, kk_scr.shape[1]), kk_scr.dtype)
        ckv_scr[QB:, :] = ckv_ref[0]
        kk_scr[QB:, :] = kk_ref[0]

    def chunk_start(e):
        return pl.multiple_of((n - 2 * e) * QB, QB)

    idx_scale = IDX_DIM ** -0.5 * IDX_HEADS ** -0.5
    w_t = jnp.transpose(kq_ref[0].astype(F32)) * idx_scale
    for h in range(IDX_HEADS):
        qst_scr[h * QB:(h + 1) * QB, :] = qi_ref[0, :, h * LANES:(h + 1) * LANES]
        wrow_scr[h] = jnp.broadcast_to(w_t[IDX_DIM + h:IDX_DIM + h + 1, :], (SUBLANES, LANES))
    q_all = qb_ref[0]
    for h in range(B_HEADS):
        ql = jnp.dot(q_all[:, h * HEAD_DIM:(h + 1) * HEAD_DIM], wuk_ref[h],
                     preferred_element_type=F32)
        ql = (ql * (HEAD_DIM ** -0.5 * LOG2E)).astype(BF16)
        qlat_scr[h * QB:(h + 1) * QB, :] = ql
        qf = ql.astype(F32)
        qn = jnp.sqrt(jnp.sum(qf * qf, axis=-1, keepdims=True))
        m_scr[h] = jnp.broadcast_to(qn, (QB, LANES)) * cb_ref[...] + mb_ref[h]

    key_row = lax.broadcasted_iota(I32, (KCH, LANES), 0)
    limit = (n * QB + CHUNK) + (lax.broadcasted_iota(I32, (1, LANES), 1) // CHUNK) * CHUNK

    def idx_chunk(e):
        ks = chunk_start(e)
        kj = kk_scr[pl.ds(ks, KCH), :]
        r = lax.dot_general(kj, qst_scr[...], (((1,), (1,)), ((), ())), preferred_element_type=F32)
        sc = jnp.maximum(r[:, 0:LANES], 0.0) * wrow_scr[0][0:1, :]
        for h in range(1, IDX_HEADS):
            sc = sc + jnp.maximum(r[:, h * LANES:(h + 1) * LANES], 0.0) * wrow_scr[h][0:1, :]
        bits = lax.bitcast_convert_type(sc, I32)
        key = bits ^ ((bits >> 31) & 0x7FFFFFFF)
        key = jnp.where(bits == INT_MIN, 0, key)
        kpos = ks - QB + key_row
        key = jnp.where((kpos >= 0) & (kpos < limit), key, INT_MIN)
        key_scr[pl.ds(ks, KCH), :] = key

    def idx_body(i, _):
        idx_chunk(2 * i)
        idx_chunk(jnp.minimum(2 * i + 1, nch - 1))
        return 0

    lax.fori_loop(0, (nch + 1) // 2, idx_body, 0)

    def count(pred):
        def hits(e):
            ks = chunk_start(e)
            hit = jnp.where(pred(key_scr[pl.ds(ks, KCH), :], ks + key_row), 1, 0)
            return jnp.sum(hit.reshape(KCH // SUBLANES, SUBLANES, LANES), axis=0)

        def pair(i, acc):
            return acc + (hits(2 * i) + hits(2 * i + 1))

        c = lax.fori_loop(0, nch // 2, pair, jnp.zeros((SUBLANES, LANES), I32))
        c = c + jnp.where(nch % 2 == 1, hits(nch - 1), 0)
        return jnp.sum(c, axis=0, keepdims=True)

    def bit_body(b, t):
        cand = t ^ jnp.left_shift(jnp.int32(1), 31 - b)
        tot = count(lambda k, pos: k >= cand)
        return jnp.where(tot >= topk, cand, t)

    thr = lax.fori_loop(0, 32, bit_body, jnp.full((1, LANES), INT_MIN, I32))
    thr = jnp.maximum(thr, INT_MIN + 1)

    n_gt = count(lambda k, pos: k > thr)
    n_eq = count(lambda k, pos: k == thr)
    need = topk - n_gt
    cut_scr[...] = jnp.full(cut_scr.shape, 2 * spad, I32)
    excess = jnp.max(jnp.where(n_eq > need, 1, 0))

    @pl.when(excess > 0)
    def _():
        nbits = int(spad).bit_length()

        def cut_body(b, c):
            cand = c | jnp.left_shift(jnp.int32(1), nbits - 1 - b)
            tot = count(lambda k, pos: (k == thr) & (pos < cand))
            return jnp.where(tot <= need, cand, c)

        c = lax.fori_loop(0, nbits, cut_body, jnp.zeros((1, LANES), I32))
        cut_scr[...] = jnp.broadcast_to(jnp.where(n_eq > need, c, 2 * spad), cut_scr.shape)

    cut = cut_scr[0:1, :]

    def mask_body(e, _):
        ks = chunk_start(e)
        k = key_scr[pl.ds(ks, KCH), :]
        sel = (k > thr) | ((k == thr) & (ks + key_row < cut))
        mask_scr[:, pl.ds(ks, KCH)] = jnp.transpose(jnp.where(sel, 0.0, NEG_BIG))
        return 0

    lax.fori_loop(0, nch, mask_body, 0)

    def masked_scores(e, near):
        ks = chunk_start(e)
        kv = ckv_scr[pl.ds(ks, KCH), :]
        addmask = mask_scr[:, pl.ds(ks, KCH)]
        s = lax.dot_general(qlat_scr[...], kv, (((1,), (1,)), ((), ())), preferred_element_type=F32)
        s = s.reshape(B_HEADS, QB, KCH) + addmask[None]
        if near:
            s = s + bias_ref[...]
        return s, kv

    def max_pass(e, near):
        s, _ = masked_scores(e, near)
        m_scr[...] = jnp.maximum(m_scr[...], jnp.maximum(s[:, :, :LANES], s[:, :, LANES:]))

    def sum_pass(e, near):
        s, kv = masked_scores(e, near)
        m = m_scr[...]
        p_lo = jnp.exp2(s[:, :, :LANES] - m)
        p_hi = jnp.exp2(s[:, :, LANES:] - m)
        l_scr[...] += p_lo + p_hi
        p = jnp.concatenate([p_lo, p_hi], axis=-1).astype(BF16).reshape(B_HEADS * QB, KCH)
        acc_scr[...] += jnp.dot(p, kv, preferred_element_type=F32)

    def sweep(fn):
        def far_pair(i, _):
            fn(2 * i + 1, False)
            fn(2 * i + 2, False)
            return 0
        lax.fori_loop(0, (nch - 1) // 2, far_pair, 0)

        @pl.when((nch - 1) % 2 == 1)
        def _():
            fn(nch - 1, False)

        fn(0, True)

    def accumulate():
        l_scr[...] = jnp.zeros(l_scr.shape, F32)
        acc_scr[...] = jnp.zeros(acc_scr.shape, F32)
        sweep(sum_pass)

    accumulate()
    smallest = jnp.min(jnp.sum(l_scr[...], axis=-1, keepdims=True))

    @pl.when(jnp.logical_not(smallest >= DSA_MIN_DENOM))
    def _():
        m_scr[...] = jnp.full(m_scr.shape, NEG_BIG, F32)
        sweep(max_pass)
        m_scr[...] = jnp.broadcast_to(jnp.max(m_scr[...], axis=-1, keepdims=True), m_scr.shape)
        accumulate()

    outs = []
    for h in range(B_HEADS):
        l = jnp.sum(l_scr[h], axis=-1, keepdims=True)
        o_lat = (acc_scr[h * QB:(h + 1) * QB, :] / l).astype(BF16)
        outs.append(jnp.dot(o_lat, wuv_ref[h], preferred_element_type=F32))
    o_ref[0] = jnp.concatenate(outs, axis=-1).astype(o_ref.dtype)


def _dsa(proj3, w_uk, w_uv, bias_b, kv_norm_g):
    bn, seq, _ = proj3.shape
    nblk = seq // QB
    spad = seq + QB
    qw = B_HEADS * HEAD_DIM
    iw = IDX_HEADS * LANES
    key_norm = jnp.max(jnp.abs(kv_norm_g.astype(F32))) * (KV_RANK ** 0.5 * BF16_SLACK)
    cb = jnp.broadcast_to(key_norm, (1, LANES))
    mb = jnp.broadcast_to(jnp.maximum(jnp.max(bias_b, axis=(1, 2)), 0.0)[:, None, None],
                          (B_HEADS, 1, LANES))
    return pl.pallas_call(
        functools.partial(_dsa_kernel, topk=min(TOPK_MAX, seq // 4)),
        grid=(bn, nblk),
        in_specs=[
            pl.BlockSpec((1, QB, qw), lambda b, n: (b, n, EV_QB // qw)),
            pl.BlockSpec((1, QB, iw), lambda b, n: (b, n, EV_QI // iw)),
            pl.BlockSpec((1, QB, LANES), lambda b, n: (b, n, EV_KW // LANES)),
            pl.BlockSpec((1, seq, KV_RANK), lambda b, n: (b, 0, EV_CL // KV_RANK)),
            pl.BlockSpec((1, seq, LANES), lambda b, n: (b, 0, EV_KW // LANES)),
            pl.BlockSpec(w_uk.shape, lambda b, n: (0, 0, 0)),
            pl.BlockSpec(w_uv.shape, lambda b, n: (0, 0, 0)),
            pl.BlockSpec((B_HEADS, QB, KCH), lambda b, n: (0, 0, 0)),
            pl.BlockSpec((1, LANES), lambda b, n: (0, 0)),
            pl.BlockSpec((B_HEADS, 1, LANES), lambda b, n: (0, 0, 0)),
        ],
        out_specs=pl.BlockSpec((1, QB, qw), lambda b, n: (b, n, 0)),
        out_shape=jax.ShapeDtypeStruct((bn, seq, qw), BF16),
        scratch_shapes=[
            pltpu.VMEM((IDX_HEADS * QB, LANES), BF16),
            pltpu.VMEM((IDX_HEADS, SUBLANES, LANES), F32),
            pltpu.VMEM((spad, LANES), I32),
            pltpu.VMEM((QB, spad), F32),
            pltpu.VMEM((B_HEADS * QB, KV_RANK), BF16),
            pltpu.VMEM((B_HEADS, QB, LANES), F32),
            pltpu.VMEM((B_HEADS, QB, LANES), F32),
            pltpu.VMEM((B_HEADS * QB, KV_RANK), F32),
            pltpu.VMEM((SUBLANES, LANES), I32),
            pltpu.VMEM((spad, KV_RANK), BF16),
            pltpu.VMEM((spad, LANES), BF16),
        ],
        compiler_params=_cparams(("arbitrary", "arbitrary")),
        name="dsa",
    )(proj3, proj3, proj3, proj3, proj3, w_uk, w_uv, bias_b, cb, mb)


SB_HG = 8


def _sb_kernel(q_ref, k_ref, v_ref, o_ref, acc_scr, carry_scr):
    n = pl.program_id(2)
    scale = C_HEAD_DIM ** -0.5
    rows = SB_HG * QB

    def sums_operand(width):
        r = lax.broadcasted_iota(I32, (width, width), 0)
        c = lax.broadcasted_iota(I32, (width, width), 1)
        suffix = jnp.where(r > c, 1.0, 0.0).astype(BF16)
        sums = jnp.concatenate([suffix, jnp.ones((width, QB), BF16)], axis=1)
        return jnp.concatenate([sums, sums], axis=0)

    def head_cols(h):
        return slice(h * C_HEAD_DIM, (h + 1) * C_HEAD_DIM)

    def chunk(ks, width, diagonal):
        z = jnp.concatenate(
            [lax.dot_general(q_ref[0, :, head_cols(h)], k_ref[0, pl.ds(ks, width), head_cols(h)],
                             (((1,), (1,)), ((), ())), preferred_element_type=F32)
             for h in range(SB_HG)], axis=0) * scale
        sp = jnp.maximum(z, 0.0) + jnp.log(1.0 + jnp.exp(-jnp.abs(z)))
        if diagonal:
            key = lax.broadcasted_iota(I32, (rows, width), 1) - (width - QB)
            earlier = key < (lax.broadcasted_iota(I32, (rows, width), 0) % QB)
            lk = jnp.where(earlier, -sp, 0.0)
        else:
            lk = -sp
        hi = lk.astype(BF16)
        lo = (lk - hi.astype(F32)).astype(BF16)
        inner = jnp.dot(jnp.concatenate([hi, lo], axis=1), sums_operand(width),
                        preferred_element_type=F32)
        carry = carry_scr[...]
        a = jnp.exp((z - sp) + (jnp.concatenate([carry] * (width // QB), axis=1) + inner[:, :width]))
        if diagonal:
            a = jnp.where(earlier, a, 0.0)
        a = a.astype(BF16)
        for h in range(SB_HG):
            r = slice(h * QB, (h + 1) * QB)
            acc_scr[r, :] += jnp.dot(a[r], v_ref[0, pl.ds(ks, width), head_cols(h)],
                                     preferred_element_type=F32)
        carry = carry + inner[:, width:]
        carry_scr[...] = carry
        return (jnp.max(carry) < SB_DONE).astype(I32)

    acc_scr[...] = jnp.zeros(acc_scr.shape, F32)
    carry_scr[...] = jnp.zeros(carry_scr.shape, F32)

    @pl.when(n % 2 == 0)
    def _():
        chunk(pl.multiple_of(n * QB, QB), QB, True)

    @pl.when(n % 2 == 1)
    def _():
        chunk(pl.multiple_of((n - 1) * QB, QB), 2 * QB, True)

    def cond(state):
        j, done = state
        return (j >= 1) & (done == 0)

    def body(state):
        j, _ = state
        return j - 2, chunk(pl.multiple_of((j - 1) * QB, QB), 2 * QB, False)

    lax.while_loop(cond, body, (n - 1 - n % 2, jnp.int32(0)))
    o_ref[0] = jnp.concatenate([acc_scr[h * QB:(h + 1) * QB, :] for h in range(SB_HG)],
                               axis=-1).astype(o_ref.dtype)


def _sb(qkv3):
    bn, seq, _ = qkv3.shape
    nblk = seq // QB
    gw = SB_HG * C_HEAD_DIM
    ngrp = C_HEADS // SB_HG
    return pl.pallas_call(
        _sb_kernel,
        grid=(bn, ngrp, nblk),
        in_specs=[
            pl.BlockSpec((1, QB, gw), lambda b, g, n: (b, n, g)),
            pl.BlockSpec((1, seq, gw), lambda b, g, n: (b, 0, ngrp + g)),
            pl.BlockSpec((1, seq, gw), lambda b, g, n: (b, 0, 2 * ngrp + g)),
        ],
        out_specs=pl.BlockSpec((1, QB, gw), lambda b, g, n: (b, n, g)),
        out_shape=jax.ShapeDtypeStruct((bn, seq, C_HEADS * C_HEAD_DIM), BF16),
        scratch_shapes=[pltpu.VMEM((SB_HG * QB, C_HEAD_DIM), F32),
                        pltpu.VMEM((SB_HG * QB, QB), F32)],
        compiler_params=_cparams(("arbitrary", "arbitrary", "arbitrary")),
        name="stick_breaking",
    )(qkv3, qkv3, qkv3)


def _t5_bucket(rel):
    half = NUM_BUCKETS // 2
    max_exact = half // 2
    n = jnp.abs(rel)
    nf = jnp.maximum(n, 1).astype(F32)
    large = max_exact + (jnp.log(nf / max_exact) / math.log(MAX_DISTANCE / max_exact)
                         * (half - max_exact)).astype(I32)
    large = jnp.minimum(large, half - 1)
    return jnp.where(rel > 0, half, 0) + jnp.where(n < max_exact, n, large)


def _band_bias(rel_bias):
    rel = (jnp.arange(2 * QB) - QB)[None, :] - jnp.arange(QB)[:, None]
    table = rel_bias.astype(F32)
    bucket = _t5_bucket(rel)[None]
    band = sum(jnp.where(bucket == k, table[k][:, None, None], 0.0) for k in range(NUM_BUCKETS))
    far = table[_t5_bucket(jnp.int32(-(QB + 1)))]
    return band, far


def _even_w_in(w_in):
    d = w_in.shape[0]
    sizes = (A_HEADS * HEAD_DIM, A_KV_HEADS * HEAD_DIM, A_KV_HEADS * HEAD_DIM,
             B_HEADS * HEAD_DIM, KV_RANK, IDX_HEADS * IDX_DIM, IDX_DIM, IDX_HEADS)
    offs = [int(o) for o in np.cumsum((0,) + sizes)]
    w = w_in.astype(BF16)
    moves = [(offs[0], offs[1], EV_QA), (offs[1], offs[2], EV_KA), (offs[2], offs[3], EV_VA),
             (offs[3], offs[4], EV_QB), (offs[4], offs[5], EV_CL), (offs[6], offs[8], EV_KW)]
    moves += [(offs[5] + h * IDX_DIM, offs[5] + (h + 1) * IDX_DIM, EV_QI + h * LANES)
              for h in range(IDX_HEADS)]
    out = jnp.zeros((d, EV_WIDTH), BF16)
    for lo, hi, dst in moves:
        out = lax.dynamic_update_slice(out, w[:, lo:hi], (0, dst))
    return out


def kernel(x, c, rel_bias, ada_w, ada_b, norm_mix_g, norm_ffn_g, ev_w_in, ev_kv_norm_g, ev_w_uk,
           ev_w_uv, ev_sinks, ev_w_out, od_w_in, od_w_out, ffn_w_up, ffn_conv_w, ffn_conv_b,
           ffn_w_down, final_g):
    bn, seq, d = x.shape
    depth = ada_w.shape[0]
    x2 = x.reshape(bn * seq, d)

    mod = _ada_mod(c, ada_w, ada_b)
    band, far = _band_bias(rel_bias)
    bias_a = band[:A_HEADS]
    bias_b = (band[A_HEADS:] - far[A_HEADS:, None, None]) * LOG2E
    w_up_all = ffn_w_up.astype(BF16)
    w_down_all = ffn_w_down.astype(BF16)

    for i in range(depth):
        sh1, sc1, g1, sh2, sc2, g2 = [m.reshape(bn, 1, d) for m in jnp.split(mod[i], 6, axis=-1)]
        g_mix = norm_mix_g[i].reshape(1, d)
        j = i // 2
        if i % 2 == 0:
            proj = _in_proj(x2, g_mix, sc1, sh1, _even_w_in(ev_w_in[j]),
                            ev_kv_norm_g[j].reshape(1, KV_RANK), seq, tm=IN_PROJ_TM, tn=IN_PROJ_TN,
                            kv_col=EV_CL)
            proj3 = proj.reshape(bn, seq, EV_WIDTH)
            o_a = _swa(proj3, bias_a, ev_sinks[j])
            o_b = _dsa(proj3, ev_w_uk[j].astype(BF16), ev_w_uv[j].astype(BF16), bias_b, ev_kv_norm_g[j])
            w_out = ev_w_out[j].astype(BF16)
            wa = w_out[:A_HEADS * HEAD_DIM]
            wb = w_out[A_HEADS * HEAD_DIM:]
            x2 = _out_proj([o_a.reshape(bn * seq, -1), o_b.reshape(bn * seq, -1)], [wa, wb],
                           x2, g1, seq, tm=OUT_PROJ_TM)
        else:
            qkv = _in_proj(x2, g_mix, sc1, sh1, od_w_in[j].astype(BF16),
                           jnp.ones((1, KV_RANK), F32), seq, tm=IN_PROJ_TM, tn=IN_PROJ_TN)
            o_c = _sb(qkv.reshape(bn, seq, -1))
            x2 = _out_proj([o_c.reshape(bn * seq, -1)], [od_w_out[j].astype(BF16)], x2, g1, seq,
                           tm=OUT_PROJ_TM)
        conv4 = jnp.concatenate([ffn_conv_w[i], ffn_conv_b[i][None]], axis=0)
        x2 = _conv_ffn(x2, norm_ffn_g[i].reshape(1, d), sc2, sh2, g2, w_up_all, conv4, w_down_all,
                       final_g.reshape(1, d), seq, layer=i, tm=FFN_TM, tf=FFN_TF,
                       final=(i == depth - 1))
    return x2.reshape(bn, seq, d)
```

```python
import functools
import math

import jax
import jax.numpy as jnp
import numpy as np
from jax import lax
from jax.experimental import pallas as pl
from jax.experimental.pallas import tpu as pltpu

F32 = jnp.float32
BF16 = jnp.bfloat16
I32 = jnp.int32

CHUNK = 64
QB = 128
HEAD_DIM = 64
A_HEADS = 16
A_KV_HEADS = 4
A_GROUP = A_HEADS // A_KV_HEADS
WINDOW_CHUNKS = 2
B_HEADS = 16
KV_RANK = 256
IDX_HEADS = 8
IDX_DIM = 64
TOPK_MAX = 256
C_HEADS = 16
C_HEAD_DIM = 128
NUM_BUCKETS = 32
MAX_DISTANCE = 128
CONV_W = 3
EPS = 1e-6

LANES = 128
SUBLANES = 8
MXU_DIM = 256
VMEM_LIMIT = 56 * 1024 * 1024
FFN_VMEM_LIMIT = 61 * 1024 * 1024

ADA_TN = 1024
IN_PROJ_TM, IN_PROJ_TN = 1024, 1024
OUT_PROJ_TM = 512
FFN_TM, FFN_TF = 1024, 512

NEG_BIG = -1e30
LOG2E = math.log2(math.e)
INT_MIN = -(2 ** 31)
SB_DONE = -88.0

EV_QA = 0
EV_QB = EV_QA + A_HEADS * HEAD_DIM
EV_QI = EV_QB + B_HEADS * HEAD_DIM
EV_KA = EV_QI + IDX_HEADS * LANES
EV_VA = EV_KA + A_KV_HEADS * HEAD_DIM
EV_CL = EV_VA + A_KV_HEADS * HEAD_DIM
EV_KW = EV_CL + KV_RANK
EV_WIDTH = 4096


def _cparams(sem, vmem_limit=VMEM_LIMIT):
    return pltpu.CompilerParams(dimension_semantics=sem, vmem_limit_bytes=vmem_limit)


NORM_RB = 16
NORM_UNROLL = 8


def _row_blocks(n_rows, fn):
    def body(r, _):
        fn(pl.ds(pl.multiple_of(r * NORM_RB, NORM_RB), NORM_RB))
        return 0
    lax.fori_loop(0, n_rows // NORM_RB, body, 0, unroll=NORM_UNROLL)


def _norm_mod_to(h_scr, x_ref, g_ref, sc_ref, sh_ref):
    def block(rows):
        x = x_ref[rows, :]
        ms = jnp.mean(x * x, axis=-1, keepdims=True)
        y = x * lax.rsqrt(ms + EPS)
        h_scr[rows, :] = ((y * g_ref[...]) * (1.0 + sc_ref[0]) + sh_ref[0]).astype(BF16)
    _row_blocks(x_ref.shape[0], block)


def _silu(x):
    return x / (1.0 + jnp.exp(-x))


def _ada_kernel(c_ref, w_ref, b_ref, o_ref):
    a = _silu(c_ref[...]).astype(BF16)
    o_ref[0] = jnp.dot(a, w_ref[0].astype(BF16), preferred_element_type=F32) + b_ref[0]


def _ada_mod(c, ada_w, ada_b):
    depth, d, n = ada_w.shape
    bn = c.shape[0]
    rows = -(-bn // SUBLANES) * SUBLANES
    c_pad = jnp.pad(c, ((0, rows - bn), (0, 0)))
    tn = ADA_TN
    out = pl.pallas_call(
        _ada_kernel,
        grid=(depth, n // tn),
        in_specs=[
            pl.BlockSpec((rows, d), lambda l, j: (0, 0)),
            pl.BlockSpec((1, d, tn), lambda l, j: (l, 0, j)),
            pl.BlockSpec((1, 1, tn), lambda l, j: (l, 0, j)),
        ],
        out_specs=pl.BlockSpec((1, rows, tn), lambda l, j: (l, 0, j)),
        out_shape=jax.ShapeDtypeStruct((depth, rows, n), F32),
        compiler_params=_cparams(("arbitrary", "arbitrary")),
        name="ada_mod",
    )(c_pad, ada_w, ada_b.reshape(depth, 1, n))
    return out[:, :bn]


def _in_proj_kernel(x_ref, g_ref, sc_ref, sh_ref, w_ref, kvg_ref, o_ref, h_scr, *, kv_tile, kv_off):
    j = pl.program_id(1)

    @pl.when(j == 0)
    def _():
        _norm_mod_to(h_scr, x_ref, g_ref, sc_ref, sh_ref)

    def project():
        return jnp.dot(h_scr[...], w_ref[...], preferred_element_type=F32)

    if kv_tile is None:
        o_ref[...] = project().astype(o_ref.dtype)
    else:
        @pl.when(j != kv_tile)
        def _():
            o_ref[...] = project().astype(o_ref.dtype)

        @pl.when(j == kv_tile)
        def _():
            acc = project()
            lat = acc[:, kv_off:kv_off + KV_RANK]
            ms = jnp.mean(lat * lat, axis=-1, keepdims=True)
            lat = lat * lax.rsqrt(ms + EPS) * kvg_ref[...]
            o_ref[...] = acc.astype(o_ref.dtype)
            o_ref[:, kv_off:kv_off + KV_RANK] = lat.astype(o_ref.dtype)


def _in_proj(x2, g, sc, sh, w, kv_g, seq, *, tm, tn, kv_col=None):
    n_rows, d = x2.shape
    n_out = w.shape[1]
    tiles_per_batch = seq // tm
    if kv_col is None:
        kv_tile, kv_off = None, 0
    else:
        kv_tile, kv_off = kv_col // tn, kv_col % tn
    kern = functools.partial(_in_proj_kernel, kv_tile=kv_tile, kv_off=kv_off)
    return pl.pallas_call(
        kern,
        grid=(n_rows // tm, n_out // tn),
        in_specs=[
            pl.BlockSpec((tm, d), lambda i, j: (i, 0)),
            pl.BlockSpec((1, d), lambda i, j: (0, 0)),
            pl.BlockSpec((1, 1, d), lambda i, j: (i // tiles_per_batch, 0, 0)),
            pl.BlockSpec((1, 1, d), lambda i, j: (i // tiles_per_batch, 0, 0)),
            pl.BlockSpec((d, tn), lambda i, j: (0, j)),
            pl.BlockSpec((1, KV_RANK), lambda i, j: (0, 0)),
        ],
        out_specs=pl.BlockSpec((tm, tn), lambda i, j: (i, j)),
        out_shape=jax.ShapeDtypeStruct((n_rows, n_out), BF16),
        scratch_shapes=[pltpu.VMEM((tm, d), BF16)],
        compiler_params=_cparams(("arbitrary", "arbitrary")),
        name="in_proj",
    )(x2, g, sc, sh, w, kv_g)


def _out_proj_kernel(*refs, n_act):
    acts = refs[:n_act]
    ws = refs[n_act:2 * n_act]
    x_ref, gate_ref, o_ref = refs[2 * n_act:]
    y = jnp.dot(acts[0][...], ws[0][...], preferred_element_type=F32)
    for a_ref, w_ref in zip(acts[1:], ws[1:]):
        y = y + jnp.dot(a_ref[...], w_ref[...], preferred_element_type=F32)
    o_ref[...] = x_ref[...] + gate_ref[0] * y


def _out_proj(acts, ws, x2, gate, seq, *, tm):
    n_rows, d = x2.shape
    tiles_per_batch = seq // tm
    n_act = len(acts)
    in_specs = ([pl.BlockSpec((tm, a.shape[1]), lambda i: (i, 0)) for a in acts]
                + [pl.BlockSpec(w.shape, lambda i: (0, 0)) for w in ws]
                + [pl.BlockSpec((tm, d), lambda i: (i, 0)),
                   pl.BlockSpec((1, 1, d), lambda i: (i // tiles_per_batch, 0, 0))])
    return pl.pallas_call(
        functools.partial(_out_proj_kernel, n_act=n_act),
        grid=(n_rows // tm,),
        in_specs=in_specs,
        out_specs=pl.BlockSpec((tm, d), lambda i: (i, 0)),
        out_shape=jax.ShapeDtypeStruct((n_rows, d), F32),
        compiler_params=_cparams(("arbitrary",)),
        name="out_proj",
    )(*acts, *ws, x2, gate)


FFN_RB = 64


def _ffn_kernel(x_ref, g_ref, sc_ref, sh_ref, gate_ref, wg_ref, wv_ref, cg_ref, cv_ref, wd_ref,
                wdl_ref, fg_ref, o_ref, h_scr, ug_scr, uv_scr, a0_scr, a1_scr, carry_g, carry_v, *,
                nf, tiles_per_batch, final):
    i = pl.program_id(0)
    f = pl.program_id(1)
    tm = x_ref.shape[0]

    @pl.when(f == 0)
    def _():
        _norm_mod_to(h_scr, x_ref, g_ref, sc_ref, sh_ref)
        o_ref[...] = jnp.zeros(o_ref.shape, F32)
        a1_scr[...] = jnp.zeros(a1_scr.shape, BF16)

    batch_start = (i % tiles_per_batch) == 0

    @pl.when(batch_start)
    def _():
        ug_scr[0:SUBLANES, :] = jnp.zeros((SUBLANES, ug_scr.shape[1]), F32)
        uv_scr[0:SUBLANES, :] = jnp.zeros((SUBLANES, uv_scr.shape[1]), F32)

    @pl.when(jnp.logical_not(batch_start))
    def _():
        ug_scr[0:SUBLANES, :] = carry_g[f]
        uv_scr[0:SUBLANES, :] = carry_v[f]

    def conv(u_scr, cw, base):
        y = cw[2:3] * u_scr[base:base + FFN_RB, :] + cw[3:4]
        y = y + cw[1:2] * u_scr[base - 1:base - 1 + FFN_RB, :]
        return y + cw[0:1] * u_scr[base - 2:base - 2 + FFN_RB, :]

    def step(a_cur, a_prev):
        ug_scr[SUBLANES:, :] = jnp.dot(h_scr[...], wg_ref[...], preferred_element_type=F32)
        uv_scr[SUBLANES:, :] = jnp.dot(h_scr[...], wv_ref[...], preferred_element_type=F32)
        o_ref[...] += jnp.dot(a_prev[...], wd_ref[...], preferred_element_type=F32)
        carry_g[f] = ug_scr[tm:tm + SUBLANES, :]
        carry_v[f] = uv_scr[tm:tm + SUBLANES, :]
        cwg = cg_ref[...]
        cwv = cv_ref[...]
        for r in range(tm // FFN_RB):
            base = SUBLANES + r * FFN_RB
            act = _silu(conv(ug_scr, cwg, base)) * conv(uv_scr, cwv, base)
            a_cur[r * FFN_RB:(r + 1) * FFN_RB, :] = act.astype(BF16)

    @pl.when(f % 2 == 0)
    def _():
        step(a0_scr, a1_scr)

    @pl.when(f % 2 == 1)
    def _():
        step(a1_scr, a0_scr)

    @pl.when(f == nf - 1)
    def _():
        a_last = a0_scr if (nf - 1) % 2 == 0 else a1_scr
        o_ref[...] += jnp.dot(a_last[...], wdl_ref[...], preferred_element_type=F32)

        def residual(rows):
            xo = x_ref[rows, :] + gate_ref[0] * o_ref[rows, :]
            if final:
                ms = jnp.mean(xo * xo, axis=-1, keepdims=True)
                xo = xo * lax.rsqrt(ms + EPS) * fg_ref[...]
            o_ref[rows, :] = xo
        _row_blocks(tm, residual)


def _conv_ffn(x2, g, sc, sh, gate, w_up, conv4, w_down, final_g, seq, *, layer, tm, tf, final):
    n_rows, d = x2.shape
    d_ff = w_down.shape[1]
    nf = d_ff // tf
    tiles_per_batch = seq // tm
    kern = functools.partial(_ffn_kernel, nf=nf, tiles_per_batch=tiles_per_batch, final=final)
    return pl.pallas_call(
        kern,
        grid=(n_rows // tm, nf),
        in_specs=[
            pl.BlockSpec((tm, d), lambda i, f: (i, 0)),
            pl.BlockSpec((1, d), lambda i, f: (0, 0)),
            pl.BlockSpec((1, 1, d), lambda i, f: (i // tiles_per_batch, 0, 0)),
            pl.BlockSpec((1, 1, d), lambda i, f: (i // tiles_per_batch, 0, 0)),
            pl.BlockSpec((1, 1, d), lambda i, f: (i // tiles_per_batch, 0, 0)),
            pl.BlockSpec((None, d, tf), lambda i, f: (layer, 0, f)),
            pl.BlockSpec((None, d, tf), lambda i, f: (layer, 0, f + nf)),
            pl.BlockSpec((CONV_W + 1, tf), lambda i, f: (0, f)),
            pl.BlockSpec((CONV_W + 1, tf), lambda i, f: (0, f + nf)),
            pl.BlockSpec((None, tf, d), lambda i, f: (layer, jnp.maximum(f - 1, 0), 0)),
            pl.BlockSpec((None, tf, d), lambda i, f: (layer, nf - 1, 0)),
            pl.BlockSpec((1, d), lambda i, f: (0, 0)),
        ],
        out_specs=pl.BlockSpec((tm, d), lambda i, f: (i, 0)),
        out_shape=jax.ShapeDtypeStruct((n_rows, d), F32),
        scratch_shapes=[
            pltpu.VMEM((tm, d), BF16),
            pltpu.VMEM((tm + SUBLANES, tf), F32),
            pltpu.VMEM((tm + SUBLANES, tf), F32),
            pltpu.VMEM((tm, tf), BF16),
            pltpu.VMEM((tm, tf), BF16),
            pltpu.VMEM((nf, SUBLANES, tf), F32),
            pltpu.VMEM((nf, SUBLANES, tf), F32),
        ],
        compiler_params=_cparams(("arbitrary", "arbitrary"), FFN_VMEM_LIMIT),
        name="conv_ffn",
    )(x2, g, sc, sh, gate, w_up, w_up, conv4, conv4, w_down, w_down, final_g)


def _swa_kernel(q_ref, kp_ref, ko_ref, vp_ref, vo_ref, bias_ref, sink_ref, o_ref, s_scr, p_scr,
                den_scr):
    n = pl.program_id(1)
    row = lax.broadcasted_iota(I32, (QB, 2 * QB), 0)
    col = lax.broadcasted_iota(I32, (QB, 2 * QB), 1)
    qc = row // CHUNK
    kc = col // CHUNK - QB // CHUNK
    allowed = (kc <= qc) & (kc >= qc - WINDOW_CHUNKS) & ((col >= QB) | (n > 0))
    addmask = jnp.where(allowed, 0.0, NEG_BIG)
    k_all = jnp.concatenate([kp_ref[0], ko_ref[0]], axis=0)
    v_all = jnp.concatenate([vp_ref[0], vo_ref[0]], axis=0)
    q_all = q_ref[0] * (HEAD_DIM ** -0.5)
    low = lax.broadcasted_iota(I32, (QB, LANES), 1) < HEAD_DIM
    zero = jnp.zeros((QB, LANES), q_all.dtype)
    grp = A_GROUP * QB

    for g in range(A_KV_HEADS):
        kg = k_all[:, g * HEAD_DIM:(g + 1) * HEAD_DIM]
        k2 = jnp.concatenate([kg, kg], axis=1)
        parts = []
        for c in range(A_GROUP // 2):
            col0 = (g * A_GROUP + 2 * c) * HEAD_DIM
            q2 = q_all[:, col0:col0 + LANES]
            parts += [jnp.where(low, q2, zero), jnp.where(low, zero, q2)]
        s_scr[g * grp:(g + 1) * grp, :] = lax.dot_general(
            jnp.concatenate(parts, axis=0), k2, (((1,), (1,)), ((), ())), preferred_element_type=F32)

    s = s_scr[...].reshape(A_HEADS, QB, 2 * QB) + bias_ref[...] + addmask[None]
    sink = sink_ref[...][:, :, 0:1]
    m = jnp.maximum(jnp.max(s, axis=-1, keepdims=True), sink)
    p = jnp.exp(s - m).astype(BF16).reshape(A_HEADS * QB, 2 * QB)
    p_scr[...] = p
    total = jnp.dot(p, jnp.ones((2 * QB, LANES), BF16), preferred_element_type=F32)
    sink_term = jnp.broadcast_to(jnp.exp(sink - m), (A_HEADS, QB, LANES))
    den_scr[...] = total + sink_term.reshape(A_HEADS * QB, LANES)

    outs = []
    for g in range(A_KV_HEADS):
        vg = v_all[:, g * HEAD_DIM:(g + 1) * HEAD_DIM]
        v2 = jnp.concatenate([vg, vg], axis=1)
        rows = slice(g * grp, (g + 1) * grp)
        o = jnp.dot(p_scr[rows, :], v2, preferred_element_type=F32) / den_scr[rows, :]
        for c in range(A_GROUP // 2):
            even = o[(2 * c) * QB:(2 * c + 1) * QB]
            odd = o[(2 * c + 1) * QB:(2 * c + 2) * QB]
            outs.append(jnp.where(low, even, odd))
    o_ref[0] = jnp.concatenate(outs, axis=-1).astype(o_ref.dtype)


def _swa(proj3, bias_a, sinks):
    bn, seq, _ = proj3.shape
    nblk = seq // QB
    kvw = A_KV_HEADS * HEAD_DIM
    qw = A_HEADS * HEAD_DIM
    sink_b = jnp.broadcast_to(sinks.astype(F32)[:, None, None], (A_HEADS, QB, LANES))
    prev = lambda c: (lambda b, n: (b, jnp.maximum(n - 1, 0), c))
    own = lambda c: (lambda b, n: (b, n, c))
    return pl.pallas_call(
        _swa_kernel,
        grid=(bn, nblk),
        in_specs=[
            pl.BlockSpec((1, QB, qw), own(EV_QA // qw)),
            pl.BlockSpec((1, QB, kvw), prev(EV_KA // kvw)),
            pl.BlockSpec((1, QB, kvw), own(EV_KA // kvw)),
            pl.BlockSpec((1, QB, kvw), prev(EV_VA // kvw)),
            pl.BlockSpec((1, QB, kvw), own(EV_VA // kvw)),
            pl.BlockSpec((A_HEADS, QB, 2 * QB), lambda b, n: (0, 0, 0)),
            pl.BlockSpec((A_HEADS, QB, LANES), lambda b, n: (0, 0, 0)),
        ],
        out_specs=pl.BlockSpec((1, QB, qw), lambda b, n: (b, n, 0)),
        out_shape=jax.ShapeDtypeStruct((bn, seq, qw), BF16),
        scratch_shapes=[
            pltpu.VMEM((A_HEADS * QB, 2 * QB), F32),
            pltpu.VMEM((A_HEADS * QB, 2 * QB), BF16),
            pltpu.VMEM((A_HEADS * QB, LANES), F32),
        ],
        compiler_params=_cparams(("arbitrary", "arbitrary")),
        name="swa",
    )(proj3, proj3, proj3, proj3, proj3, bias_a, sink_b)


KCH = MXU_DIM
DSA_MIN_DENOM = 2.0 ** -60
BF16_SLACK = 1.0 + 2.0 ** -7


def _dsa_kernel(qb_ref, qi_ref, kq_ref, ckv_ref, kk_ref, wuk_ref, wuv_ref, bias_ref, cb_ref,
                mb_ref, o_ref,
                qst_scr, wrow_scr, key_scr, mask_scr, qlat_scr, m_scr, l_scr, acc_scr, cut_scr,
                ckv_scr, kk_scr, *, topk):
    n = pl.program_id(1)
    nch = n // 2 + 1
    spad = key_scr.shape[0]

    @pl.when(n == 0)
    def _():
        ckv_scr[0:QB, :] = jnp.zeros((QB, ckv_scr.shape[1]), ckv_scr.dtype)
        kk_scr[0:QB, :] = jnp.zeros((QB, kk_scr.shape[1]), kk_scr.dtype)
        ckv_scr[QB:, :] = ckv_ref[0]
        kk_scr[QB:, :] = kk_ref[0]

    def chunk_start(e):
        return pl.multiple_of((n - 2 * e) * QB, QB)

    idx_scale = IDX_DIM ** -0.5 * IDX_HEADS ** -0.5
    w_t = jnp.transpose(kq_ref[0].astype(F32)) * idx_scale
    for h in range(IDX_HEADS):
        qst_scr[h * QB:(h + 1) * QB, :] = qi_ref[0, :, h * LANES:(h + 1) * LANES]
        wrow_scr[h] = jnp.broadcast_to(w_t[IDX_DIM + h:IDX_DIM + h + 1, :], (SUBLANES, LANES))
    q_all = qb_ref[0]
    for h in range(B_HEADS):
        ql = jnp.dot(q_all[:, h * HEAD_DIM:(h + 1) * HEAD_DIM], wuk_ref[h],
                     preferred_element_type=F32)
        ql = (ql * (HEAD_DIM ** -0.5 * LOG2E)).astype(BF16)
        qlat_scr[h * QB:(h + 1) * QB, :] = ql
        qf = ql.astype(F32)
        qn = jnp.sqrt(jnp.sum(qf * qf, axis=-1, keepdims=True))
        m_scr[h] = jnp.broadcast_to(qn, (QB, LANES)) * cb_ref[...] + mb_ref[h]

    key_row = lax.broadcasted_iota(I32, (KCH, LANES), 0)
    limit = (n * QB + CHUNK) + (lax.broadcasted_iota(I32, (1, LANES), 1) // CHUNK) * CHUNK

    def idx_chunk(e):
        ks = chunk_start(e)
        kj = kk_scr[pl.ds(ks, KCH), :]
        r = lax.dot_general(kj, qst_scr[...], (((1,), (1,)), ((), ())), preferred_element_type=F32)
        sc = jnp.maximum(r[:, 0:LANES], 0.0) * wrow_scr[0][0:1, :]
        for h in range(1, IDX_HEADS):
            sc = sc + jnp.maximum(r[:, h * LANES:(h + 1) * LANES], 0.0) * wrow_scr[h][0:1, :]
        bits = lax.bitcast_convert_type(sc, I32)
        key = bits ^ ((bits >> 31) & 0x7FFFFFFF)
        key = jnp.where(bits == INT_MIN, 0, key)
        kpos = ks - QB + key_row
        key = jnp.where((kpos >= 0) & (kpos < limit), key, INT_MIN)
        key_scr[pl.ds(ks, KCH), :] = key

    def idx_body(i, _):
        idx_chunk(2 * i)
        idx_chunk(jnp.minimum(2 * i + 1, nch - 1))
        return 0

    lax.fori_loop(0, (nch + 1) // 2, idx_body, 0)

    def count(pred):
        def hits(e):
            ks = chunk_start(e)
            hit = jnp.where(pred(key_scr[pl.ds(ks, KCH), :], ks + key_row), 1, 0)
            return jnp.sum(hit.reshape(KCH // SUBLANES, SUBLANES, LANES), axis=0)

        def pair(i, acc):
            return acc + (hits(2 * i) + hits(2 * i + 1))

        c = lax.fori_loop(0, nch // 2, pair, jnp.zeros((SUBLANES, LANES), I32))
        c = c + jnp.where(nch % 2 == 1, hits(nch - 1), 0)
        return jnp.sum(c, axis=0, keepdims=True)

    def bit_body(b, t):
        cand = t ^ jnp.left_shift(jnp.int32(1), 31 - b)
        tot = count(lambda k, pos: k >= cand)
        return jnp.where(tot >= topk, cand, t)

    thr = lax.fori_loop(0, 32, bit_body, jnp.full((1, LANES), INT_MIN, I32))
    thr = jnp.maximum(thr, INT_MIN + 1)

    n_gt = count(lambda k, pos: k > thr)
    n_eq = count(lambda k, pos: k == thr)
    need = topk - n_gt
    cut_scr[...] = jnp.full(cut_scr.shape, 2 * spad, I32)
    excess = jnp.max(jnp.where(n_eq > need, 1, 0))

    @pl.when(excess > 0)
    def _():
        nbits = int(spad).bit_length()

        def cut_body(b, c):
            cand = c | jnp.left_shift(jnp.int32(1), nbits - 1 - b)
            tot = count(lambda k, pos: (k == thr) & (pos < cand))
            return jnp.where(tot <= need, cand, c)

        c = lax.fori_loop(0, nbits, cut_body, jnp.zeros((1, LANES), I32))
        cut_scr[...] = jnp.broadcast_to(jnp.where(n_eq > need, c, 2 * spad), cut_scr.shape)

    cut = cut_scr[0:1, :]

    def mask_body(e, _):
        ks = chunk_start(e)
        k = key_scr[pl.ds(ks, KCH), :]
        sel = (k > thr) | ((k == thr) & (ks + key_row < cut))
        mask_scr[:, pl.ds(ks, KCH)] = jnp.transpose(jnp.where(sel, 0.0, NEG_BIG))
        return 0

    lax.fori_loop(0, nch, mask_body, 0)

    def masked_scores(e, near):
        ks = chunk_start(e)
        kv = ckv_scr[pl.ds(ks, KCH), :]
        addmask = mask_scr[:, pl.ds(ks, KCH)]
        s = lax.dot_general(qlat_scr[...], kv, (((1,), (1,)), ((), ())), preferred_element_type=F32)
        s = s.reshape(B_HEADS, QB, KCH) + addmask[None]
        if near:
            s = s + bias_ref[...]
        return s, kv

    def max_pass(e, near):
        s, _ = masked_scores(e, near)
        m_scr[...] = jnp.maximum(m_scr[...], jnp.maximum(s[:, :, :LANES], s[:, :, LANES:]))

    def sum_pass(e, near):
        s, kv = masked_scores(e, near)
        m = m_scr[...]
        p_lo = jnp.exp2(s[:, :, :LANES] - m)
        p_hi = jnp.exp2(s[:, :, LANES:] - m)
        l_scr[...] += p_lo + p_hi
        p = jnp.concatenate([p_lo, p_hi], axis=-1).astype(BF16).reshape(B_HEADS * QB, KCH)
        acc_scr[...] += jnp.dot(p, kv, preferred_element_type=F32)

    def sweep(fn):
        def far_pair(i, _):
            fn(2 * i + 1, False)
            fn(2 * i + 2, False)
            return 0
        lax.fori_loop(0, (nch - 1) // 2, far_pair, 0)

        @pl.when((nch - 1) % 2 == 1)
        def _():
            fn(nch - 1, False)

        fn(0, True)

    def accumulate():
        l_scr[...] = jnp.zeros(l_scr.shape, F32)
        acc_scr[...] = jnp.zeros(acc_scr.shape, F32)
        sweep(sum_pass)

    accumulate()
    smallest = jnp.min(jnp.sum(l_scr[...], axis=-1, keepdims=True))

    @pl.when(jnp.logical_not(smallest >= DSA_MIN_DENOM))
    def _():
        m_scr[...] = jnp.full(m_scr.shape, NEG_BIG, F32)
        sweep(max_pass)
        m_scr[...] = jnp.broadcast_to(jnp.max(m_scr[...], axis=-1, keepdims=True), m_scr.shape)
        accumulate()

    outs = []
    for h in range(B_HEADS):
        l = jnp.sum(l_scr[h], axis=-1, keepdims=True)
        o_lat = (acc_scr[h * QB:(h + 1) * QB, :] / l).astype(BF16)
        outs.append(jnp.dot(o_lat, wuv_ref[h], preferred_element_type=F32))
    o_ref[0] = jnp.concatenate(outs, axis=-1).astype(o_ref.dtype)


def _dsa(proj3, w_uk, w_uv, bias_b, kv_norm_g):
    bn, seq, _ = proj3.shape
    nblk = seq // QB
    spad = seq + QB
    qw = B_HEADS * HEAD_DIM
    iw = IDX_HEADS * LANES
    key_norm = jnp.max(jnp.abs(kv_norm_g.astype(F32))) * (KV_RANK ** 0.5 * BF16_SLACK)
    cb = jnp.broadcast_to(key_norm, (1, LANES))
    mb = jnp.broadcast_to(jnp.maximum(jnp.max(bias_b, axis=(1, 2)), 0.0)[:, None, None],
                          (B_HEADS, 1, LANES))
    return pl.pallas_call(
        functools.partial(_dsa_kernel, topk=min(TOPK_MAX, seq // 4)),
        grid=(bn, nblk),
        in_specs=[
            pl.BlockSpec((1, QB, qw), lambda b, n: (b, n, EV_QB // qw)),
            pl.BlockSpec((1, QB, iw), lambda b, n: (b, n, EV_QI // iw)),
            pl.BlockSpec((1, QB, LANES), lambda b, n: (b, n, EV_KW // LANES)),
            pl.BlockSpec((1, seq, KV_RANK), lambda b, n: (b, 0, EV_CL // KV_RANK)),
            pl.BlockSpec((1, seq, LANES), lambda b, n: (b, 0, EV_KW // LANES)),
            pl.BlockSpec(w_uk.shape, lambda b, n: (0, 0, 0)),
            pl.BlockSpec(w_uv.shape, lambda b, n: (0, 0, 0)),
            pl.BlockSpec((B_HEADS, QB, KCH), lambda b, n: (0, 0, 0)),
            pl.BlockSpec((1, LANES), lambda b, n: (0, 0)),
            pl.BlockSpec((B_HEADS, 1, LANES), lambda b, n: (0, 0, 0)),
        ],
        out_specs=pl.BlockSpec((1, QB, qw), lambda b, n: (b, n, 0)),
        out_shape=jax.ShapeDtypeStruct((bn, seq, qw), BF16),
        scratch_shapes=[
            pltpu.VMEM((IDX_HEADS * QB, LANES), BF16),
            pltpu.VMEM((IDX_HEADS, SUBLANES, LANES), F32),
            pltpu.VMEM((spad, LANES), I32),
            pltpu.VMEM((QB, spad), F32),
            pltpu.VMEM((B_HEADS * QB, KV_RANK), BF16),
            pltpu.VMEM((B_HEADS, QB, LANES), F32),
            pltpu.VMEM((B_HEADS, QB, LANES), F32),
            pltpu.VMEM((B_HEADS * QB, KV_RANK), F32),
            pltpu.VMEM((SUBLANES, LANES), I32),
            pltpu.VMEM((spad, KV_RANK), BF16),
            pltpu.VMEM((spad, LANES), BF16),
        ],
        compiler_params=_cparams(("arbitrary", "arbitrary")),
        name="dsa",
    )(proj3, proj3, proj3, proj3, proj3, w_uk, w_uv, bias_b, cb, mb)


SB_HG = 8


def _sb_kernel(q_ref, k_ref, v_ref, o_ref, acc_scr, carry_scr):
    n = pl.program_id(2)
    scale = C_HEAD_DIM ** -0.5
    rows = SB_HG * QB

    def sums_operand(width):
        r = lax.broadcasted_iota(I32, (width, width), 0)
        c = lax.broadcasted_iota(I32, (width, width), 1)
        suffix = jnp.where(r > c, 1.0, 0.0).astype(BF16)
        sums = jnp.concatenate([suffix, jnp.ones((width, QB), BF16)], axis=1)
        return jnp.concatenate([sums, sums], axis=0)

    def head_cols(h):
        return slice(h * C_HEAD_DIM, (h + 1) * C_HEAD_DIM)

    def chunk(ks, width, diagonal):
        z = jnp.concatenate(
            [lax.dot_general(q_ref[0, :, head_cols(h)], k_ref[0, pl.ds(ks, width), head_cols(h)],
                             (((1,), (1,)), ((), ())), preferred_element_type=F32)
             for h in range(SB_HG)], axis=0) * scale
        sp = jnp.maximum(z, 0.0) + jnp.log(1.0 + jnp.exp(-jnp.abs(z)))
        if diagonal:
            key = lax.broadcasted_iota(I32, (rows, width), 1) - (width - QB)
            earlier = key < (lax.broadcasted_iota(I32, (rows, width), 0) % QB)
            lk = jnp.where(earlier, -sp, 0.0)
        else:
            lk = -sp
        hi = lk.astype(BF16)
        lo = (lk - hi.astype(F32)).astype(BF16)
        inner = jnp.dot(jnp.concatenate([hi, lo], axis=1), sums_operand(width),
                        preferred_element_type=F32)
        carry = carry_scr[...]
        a = jnp.exp((z - sp) + (jnp.concatenate([carry] * (width // QB), axis=1) + inner[:, :width]))
        if diagonal:
            a = jnp.where(earlier, a, 0.0)
        a = a.astype(BF16)
        for h in range(SB_HG):
            r = slice(h * QB, (h + 1) * QB)
            acc_scr[r, :] += jnp.dot(a[r], v_ref[0, pl.ds(ks, width), head_cols(h)],
                                     preferred_element_type=F32)
        carry = carry + inner[:, width:]
        carry_scr[...] = carry
        return (jnp.max(carry) < SB_DONE).astype(I32)

    acc_scr[...] = jnp.zeros(acc_scr.shape, F32)
    carry_scr[...] = jnp.zeros(carry_scr.shape, F32)

    @pl.when(n % 2 == 0)
    def _():
        chunk(pl.multiple_of(n * QB, QB), QB, True)

    @pl.when(n % 2 == 1)
    def _():
        chunk(pl.multiple_of((n - 1) * QB, QB), 2 * QB, True)

    def cond(state):
        j, done = state
        return (j >= 1) & (done == 0)

    def body(state):
        j, _ = state
        return j - 2, chunk(pl.multiple_of((j - 1) * QB, QB), 2 * QB, False)

    lax.while_loop(cond, body, (n - 1 - n % 2, jnp.int32(0)))
    o_ref[0] = jnp.concatenate([acc_scr[h * QB:(h + 1) * QB, :] for h in range(SB_HG)],
                               axis=-1).astype(o_ref.dtype)


def _sb(qkv3):
    bn, seq, _ = qkv3.shape
    nblk = seq // QB
    gw = SB_HG * C_HEAD_DIM
    ngrp = C_HEADS // SB_HG
    return pl.pallas_call(
        _sb_kernel,
        grid=(bn, ngrp, nblk),
        in_specs=[
            pl.BlockSpec((1, QB, gw), lambda b, g, n: (b, n, g)),
            pl.BlockSpec((1, seq, gw), lambda b, g, n: (b, 0, ngrp + g)),
            pl.BlockSpec((1, seq, gw), lambda b, g, n: (b, 0, 2 * ngrp + g)),
        ],
        out_specs=pl.BlockSpec((1, QB, gw), lambda b, g, n: (b, n, g)),
        out_shape=jax.ShapeDtypeStruct((bn, seq, C_HEADS * C_HEAD_DIM), BF16),
        scratch_shapes=[pltpu.VMEM((SB_HG * QB, C_HEAD_DIM), F32),
                        pltpu.VMEM((SB_HG * QB, QB), F32)],
        compiler_params=_cparams(("arbitrary", "arbitrary", "arbitrary")),
        name="stick_breaking",
    )(qkv3, qkv3, qkv3)


def _t5_bucket(rel):
    half = NUM_BUCKETS // 2
    max_exact = half // 2
    n = jnp.abs(rel)
    nf = jnp.maximum(n, 1).astype(F32)
    large = max_exact + (jnp.log(nf / max_exact) / math.log(MAX_DISTANCE / max_exact)
                         * (half - max_exact)).astype(I32)
    large = jnp.minimum(large, half - 1)
    return jnp.where(rel > 0, half, 0) + jnp.where(n < max_exact, n, large)


def _band_bias(rel_bias):
    rel = (jnp.arange(2 * QB) - QB)[None, :] - jnp.arange(QB)[:, None]
    table = rel_bias.astype(F32)
    bucket = _t5_bucket(rel)[None]
    band = sum(jnp.where(bucket == k, table[k][:, None, None], 0.0) for k in range(NUM_BUCKETS))
    far = table[_t5_bucket(jnp.int32(-(QB + 1)))]
    return band, far


def _even_w_in(w_in):
    d = w_in.shape[0]
    sizes = (A_HEADS * HEAD_DIM, A_KV_HEADS * HEAD_DIM, A_KV_HEADS * HEAD_DIM,
             B_HEADS * HEAD_DIM, KV_RANK, IDX_HEADS * IDX_DIM, IDX_DIM, IDX_HEADS)
    offs = [int(o) for o in np.cumsum((0,) + sizes)]
    w = w_in.astype(BF16)
    moves = [(offs[0], offs[1], EV_QA), (offs[1], offs[2], EV_KA), (offs[2], offs[3], EV_VA),
             (offs[3], offs[4], EV_QB), (offs[4], offs[5], EV_CL), (offs[6], offs[8], EV_KW)]
    moves += [(offs[5] + h * IDX_DIM, offs[5] + (h + 1) * IDX_DIM, EV_QI + h * LANES)
              for h in range(IDX_HEADS)]
    out = jnp.zeros((d, EV_WIDTH), BF16)
    for lo, hi, dst in moves:
        out = lax.dynamic_update_slice(out, w[:, lo:hi], (0, dst))
    return out


def kernel(x, c, rel_bias, ada_w, ada_b, norm_mix_g, norm_ffn_g, ev_w_in, ev_kv_norm_g, ev_w_uk,
           ev_w_uv, ev_sinks, ev_w_out, od_w_in, od_w_out, ffn_w_up, ffn_conv_w, ffn_conv_b,
           ffn_w_down, final_g):
    bn, seq, d = x.shape
    depth = ada_w.shape[0]
    x2 = x.reshape(bn * seq, d)

    mod = _ada_mod(c, ada_w, ada_b)
    band, far = _band_bias(rel_bias)
    bias_a = band[:A_HEADS]
    bias_b = (band[A_HEADS:] - far[A_HEADS:, None, None]) * LOG2E
    w_up_all = ffn_w_up.astype(BF16)
    w_down_all = ffn_w_down.astype(BF16)

    for i in range(depth):
        sh1, sc1, g1, sh2, sc2, g2 = [m.reshape(bn, 1, d) for m in jnp.split(mod[i], 6, axis=-1)]
        g_mix = norm_mix_g[i].reshape(1, d)
        j = i // 2
        if i % 2 == 0:
            proj = _in_proj(x2, g_mix, sc1, sh1, _even_w_in(ev_w_in[j]),
                            ev_kv_norm_g[j].reshape(1, KV_RANK), seq, tm=IN_PROJ_TM, tn=IN_PROJ_TN,
                            kv_col=EV_CL)
            proj3 = proj.reshape(bn, seq, EV_WIDTH)
            o_a = _swa(proj3, bias_a, ev_sinks[j])
            o_b = _dsa(proj3, ev_w_uk[j].astype(BF16), ev_w_uv[j].astype(BF16), bias_b, ev_kv_norm_g[j])
            w_out = ev_w_out[j].astype(BF16)
            wa = w_out[:A_HEADS * HEAD_DIM]
            wb = w_out[A_HEADS * HEAD_DIM:]
            x2 = _out_proj([o_a.reshape(bn * seq, -1), o_b.reshape(bn * seq, -1)], [wa, wb],
                           x2, g1, seq, tm=OUT_PROJ_TM)
        else:
            qkv = _in_proj(x2, g_mix, sc1, sh1, od_w_in[j].astype(BF16),
                           jnp.ones((1, KV_RANK), F32), seq, tm=IN_PROJ_TM, tn=IN_PROJ_TN)
            o_c = _sb(qkv.reshape(bn, seq, -1))
            x2 = _out_proj([o_c.reshape(bn * seq, -1)], [od_w_out[j].astype(BF16)], x2, g1, seq,
                           tm=OUT_PROJ_TM)
        conv4 = jnp.concatenate([ffn_conv_w[i], ffn_conv_b[i][None]], axis=0)
        x2 = _conv_ffn(x2, norm_ffn_g[i].reshape(1, d), sc2, sh2, g2, w_up_all, conv4, w_down_all,
                       final_g.reshape(1, d), seq, layer=i, tm=FFN_TM, tf=FFN_TF,
                       final=(i == depth - 1))
    return x2.reshape(bn, seq, d)
```

```python
import functools
import math

import jax
import jax.numpy as jnp
import numpy as np
from jax import lax
from jax.experimental import pallas as pl
from jax.experimental.pallas import tpu as pltpu

F32 = jnp.float32
BF16 = jnp.bfloat16
I32 = jnp.int32

CHUNK = 64
QB = 128
HEAD_DIM = 64
A_HEADS = 16
A_KV_HEADS = 4
A_GROUP = A_HEADS // A_KV_HEADS
WINDOW_CHUNKS = 2
B_HEADS = 16
KV_RANK = 256
IDX_HEADS = 8
IDX_DIM = 64
TOPK_MAX = 256
C_HEADS = 16
C_HEAD_DIM = 128
NUM_BUCKETS = 32
MAX_DISTANCE = 128
CONV_W = 3
EPS = 1e-6

LANES = 128
SUBLANES = 8
MXU_DIM = 256
VMEM_LIMIT = 56 * 1024 * 1024
FFN_VMEM_LIMIT = 61 * 1024 * 1024

ADA_TN = 1024
IN_PROJ_TM, IN_PROJ_TN = 1024, 1024
OUT_PROJ_TM = 512
FFN_TM, FFN_TF = 1024, 512

NEG_BIG = -1e30
LOG2E = math.log2(math.e)
INT_MIN = -(2 ** 31)
SB_DONE = -88.0

EV_QA = 0
EV_QB = EV_QA + A_HEADS * HEAD_DIM
EV_QI = EV_QB + B_HEADS * HEAD_DIM
EV_KA = EV_QI + IDX_HEADS * LANES
EV_VA = EV_KA + A_KV_HEADS * HEAD_DIM
EV_CL = EV_VA + A_KV_HEADS * HEAD_DIM
EV_KW = EV_CL + KV_RANK
EV_WIDTH = 4096


def _cparams(sem, vmem_limit=VMEM_LIMIT):
    return pltpu.CompilerParams(dimension_semantics=sem, vmem_limit_bytes=vmem_limit)


NORM_RB = 16
NORM_UNROLL = 8


def _row_blocks(n_rows, fn):
    def body(r, _):
        fn(pl.ds(pl.multiple_of(r * NORM_RB, NORM_RB), NORM_RB))
        return 0
    lax.fori_loop(0, n_rows // NORM_RB, body, 0, unroll=NORM_UNROLL)


def _norm_mod_to(h_scr, x_ref, g_ref, sc_ref, sh_ref):
    def block(rows):
        x = x_ref[rows, :]
        ms = jnp.mean(x * x, axis=-1, keepdims=True)
        y = x * lax.rsqrt(ms + EPS)
        h_scr[rows, :] = ((y * g_ref[...]) * (1.0 + sc_ref[0]) + sh_ref[0]).astype(BF16)
    _row_blocks(x_ref.shape[0], block)


def _silu(x):
    return x / (1.0 + jnp.exp(-x))


def _ada_kernel(c_ref, w_ref, b_ref, o_ref):
    a = _silu(c_ref[...]).astype(BF16)
    o_ref[0] = jnp.dot(a, w_ref[0].astype(BF16), preferred_element_type=F32) + b_ref[0]


def _ada_mod(c, ada_w, ada_b):
    depth, d, n = ada_w.shape
    bn = c.shape[0]
    rows = -(-bn // SUBLANES) * SUBLANES
    c_pad = jnp.pad(c, ((0, rows - bn), (0, 0)))
    tn = ADA_TN
    out = pl.pallas_call(
        _ada_kernel,
        grid=(depth, n // tn),
        in_specs=[
            pl.BlockSpec((rows, d), lambda l, j: (0, 0)),
            pl.BlockSpec((1, d, tn), lambda l, j: (l, 0, j)),
            pl.BlockSpec((1, 1, tn), lambda l, j: (l, 0, j)),
        ],
        out_specs=pl.BlockSpec((1, rows, tn), lambda l, j: (l, 0, j)),
        out_shape=jax.ShapeDtypeStruct((depth, rows, n), F32),
        compiler_params=_cparams(("arbitrary", "arbitrary")),
        name="ada_mod",
    )(c_pad, ada_w, ada_b.reshape(depth, 1, n))
    return out[:, :bn]


def _in_proj_kernel(x_ref, g_ref, sc_ref, sh_ref, w_ref, kvg_ref, o_ref, h_scr, *, kv_tile, kv_off):
    j = pl.program_id(1)

    @pl.when(j == 0)
    def _():
        _norm_mod_to(h_scr, x_ref, g_ref, sc_ref, sh_ref)

    def project():
        return jnp.dot(h_scr[...], w_ref[...], preferred_element_type=F32)

    if kv_tile is None:
        o_ref[...] = project().astype(o_ref.dtype)
    else:
        @pl.when(j != kv_tile)
        def _():
            o_ref[...] = project().astype(o_ref.dtype)

        @pl.when(j == kv_tile)
        def _():
            acc = project()
            lat = acc[:, kv_off:kv_off + KV_RANK]
            ms = jnp.mean(lat * lat, axis=-1, keepdims=True)
            lat = lat * lax.rsqrt(ms + EPS) * kvg_ref[...]
            o_ref[...] = acc.astype(o_ref.dtype)
            o_ref[:, kv_off:kv_off + KV_RANK] = lat.astype(o_ref.dtype)


def _in_proj(x2, g, sc, sh, w, kv_g, seq, *, tm, tn, kv_col=None):
    n_rows, d = x2.shape
    n_out = w.shape[1]
    tiles_per_batch = seq // tm
    if kv_col is None:
        kv_tile, kv_off = None, 0
    else:
        kv_tile, kv_off = kv_col // tn, kv_col % tn
    kern = functools.partial(_in_proj_kernel, kv_tile=kv_tile, kv_off=kv_off)
    return pl.pallas_call(
        kern,
        grid=(n_rows // tm, n_out // tn),
        in_specs=[
            pl.BlockSpec((tm, d), lambda i, j: (i, 0)),
            pl.BlockSpec((1, d), lambda i, j: (0, 0)),
            pl.BlockSpec((1, 1, d), lambda i, j: (i // tiles_per_batch, 0, 0)),
            pl.BlockSpec((1, 1, d), lambda i, j: (i // tiles_per_batch, 0, 0)),
            pl.BlockSpec((d, tn), lambda i, j: (0, j)),
            pl.BlockSpec((1, KV_RANK), lambda i, j: (0, 0)),
        ],
        out_specs=pl.BlockSpec((tm, tn), lambda i, j: (i, j)),
        out_shape=jax.ShapeDtypeStruct((n_rows, n_out), BF16),
        scratch_shapes=[pltpu.VMEM((tm, d), BF16)],
        compiler_params=_cparams(("arbitrary", "arbitrary")),
        name="in_proj",
    )(x2, g, sc, sh, w, kv_g)


def _out_proj_kernel(*refs, n_act):
    acts = refs[:n_act]
    ws = refs[n_act:2 * n_act]
    x_ref, gate_ref, o_ref = refs[2 * n_act:]
    y = jnp.dot(acts[0][...], ws[0][...], preferred_element_type=F32)
    for a_ref, w_ref in zip(acts[1:], ws[1:]):
        y = y + jnp.dot(a_ref[...], w_ref[...], preferred_element_type=F32)
    o_ref[...] = x_ref[...] + gate_ref[0] * y


def _out_proj(acts, ws, x2, gate, seq, *, tm):
    n_rows, d = x2.shape
    tiles_per_batch = seq // tm
    n_act = len(acts)
    in_specs = ([pl.BlockSpec((tm, a.shape[1]), lambda i: (i, 0)) for a in acts]
                + [pl.BlockSpec(w.shape, lambda i: (0, 0)) for w in ws]
                + [pl.BlockSpec((tm, d), lambda i: (i, 0)),
                   pl.BlockSpec((1, 1, d), lambda i: (i // tiles_per_batch, 0, 0))])
    return pl.pallas_call(
        functools.partial(_out_proj_kernel, n_act=n_act),
        grid=(n_rows // tm,),
        in_specs=in_specs,
        out_specs=pl.BlockSpec((tm, d), lambda i: (i, 0)),
        out_shape=jax.ShapeDtypeStruct((n_rows, d), F32),
        compiler_params=_cparams(("arbitrary",)),
        name="out_proj",
    )(*acts, *ws, x2, gate)


FFN_RB = 64


def _ffn_kernel(x_ref, g_ref, sc_ref, sh_ref, gate_ref, wg_ref, wv_ref, cg_ref, cv_ref, wd_ref,
                wdl_ref, fg_ref, o_ref, h_scr, ug_scr, uv_scr, a0_scr, a1_scr, carry_g, carry_v, *,
                nf, tiles_per_batch, final):
    i = pl.program_id(0)
    f = pl.program_id(1)
    tm = x_ref.shape[0]

    @pl.when(f == 0)
    def _():
        _norm_mod_to(h_scr, x_ref, g_ref, sc_ref, sh_ref)
        o_ref[...] = jnp.zeros(o_ref.shape, F32)

    batch_start = (i % tiles_per_batch) == 0

    @pl.when(batch_start)
    def _():
        ug_scr[0:SUBLANES, :] = jnp.zeros((SUBLANES, ug_scr.shape[1]), F32)
        uv_scr[0:SUBLANES, :] = jnp.zeros((SUBLANES, uv_scr.shape[1]), F32)

    @pl.when(jnp.logical_not(batch_start))
    def _():
        ug_scr[0:SUBLANES, :] = carry_g[f]
        uv_scr[0:SUBLANES, :] = carry_v[f]

    def conv(u_scr, cw, base):
        y = cw[2:3] * u_scr[base:base + FFN_RB, :] + cw[3:4]
        y = y + cw[1:2] * u_scr[base - 1:base - 1 + FFN_RB, :]
        return y + cw[0:1] * u_scr[base - 2:base - 2 + FFN_RB, :]

    def step(a_cur, a_prev):
        ug_scr[SUBLANES:, :] = jnp.dot(h_scr[...], wg_ref[...], preferred_element_type=F32)
        uv_scr[SUBLANES:, :] = jnp.dot(h_scr[...], wv_ref[...], preferred_element_type=F32)
        if a_prev is not None:
            o_ref[...] += jnp.dot(a_prev[...], wd_ref[...], preferred_element_type=F32)
        carry_g[f] = ug_scr[tm:tm + SUBLANES, :]
        carry_v[f] = uv_scr[tm:tm + SUBLANES, :]
        cwg = cg_ref[...]
        cwv = cv_ref[...]
        for r in range(tm // FFN_RB):
            base = SUBLANES + r * FFN_RB
            act = _silu(conv(ug_scr, cwg, base)) * conv(uv_scr, cwv, base)
            a_cur[r * FFN_RB:(r + 1) * FFN_RB, :] = act.astype(BF16)

    @pl.when(f == 0)
    def _():
        step(a0_scr, None)

    @pl.when((f % 2 == 0) & (f > 0))
    def _():
        step(a0_scr, a1_scr)

    @pl.when(f % 2 == 1)
    def _():
        step(a1_scr, a0_scr)

    @pl.when(f == nf - 1)
    def _():
        a_last = a0_scr if (nf - 1) % 2 == 0 else a1_scr
        o_ref[...] += jnp.dot(a_last[...], wdl_ref[...], preferred_element_type=F32)

        def residual(rows):
            xo = x_ref[rows, :] + gate_ref[0] * o_ref[rows, :]
            if final:
                ms = jnp.mean(xo * xo, axis=-1, keepdims=True)
                xo = xo * lax.rsqrt(ms + EPS) * fg_ref[...]
            o_ref[rows, :] = xo
        _row_blocks(tm, residual)


def _conv_ffn(x2, g, sc, sh, gate, w_up, conv4, w_down, final_g, seq, *, layer, tm, tf, final):
    n_rows, d = x2.shape
    d_ff = w_down.shape[1]
    nf = d_ff // tf
    tiles_per_batch = seq // tm
    kern = functools.partial(_ffn_kernel, nf=nf, tiles_per_batch=tiles_per_batch, final=final)
    return pl.pallas_call(
        kern,
        grid=(n_rows // tm, nf),
        in_specs=[
            pl.BlockSpec((tm, d), lambda i, f: (i, 0)),
            pl.BlockSpec((1, d), lambda i, f: (0, 0)),
            pl.BlockSpec((1, 1, d), lambda i, f: (i // tiles_per_batch, 0, 0)),
            pl.BlockSpec((1, 1, d), lambda i, f: (i // tiles_per_batch, 0, 0)),
            pl.BlockSpec((1, 1, d), lambda i, f: (i // tiles_per_batch, 0, 0)),
            pl.BlockSpec((None, d, tf), lambda i, f: (layer, 0, f)),
            pl.BlockSpec((None, d, tf), lambda i, f: (layer, 0, f + nf)),
            pl.BlockSpec((CONV_W + 1, tf), lambda i, f: (0, f)),
            pl.BlockSpec((CONV_W + 1, tf), lambda i, f: (0, f + nf)),
            pl.BlockSpec((None, tf, d), lambda i, f: (layer, jnp.maximum(f - 1, 0), 0)),
            pl.BlockSpec((None, tf, d), lambda i, f: (layer, nf - 1, 0)),
            pl.BlockSpec((1, d), lambda i, f: (0, 0)),
        ],
        out_specs=pl.BlockSpec((tm, d), lambda i, f: (i, 0)),
        out_shape=jax.ShapeDtypeStruct((n_rows, d), F32),
        scratch_shapes=[
            pltpu.VMEM((tm, d), BF16),
            pltpu.VMEM((tm + SUBLANES, tf), F32),
            pltpu.VMEM((tm + SUBLANES, tf), F32),
            pltpu.VMEM((tm, tf), BF16),
            pltpu.VMEM((tm, tf), BF16),
            pltpu.VMEM((nf, SUBLANES, tf), F32),
            pltpu.VMEM((nf, SUBLANES, tf), F32),
        ],
        compiler_params=_cparams(("arbitrary", "arbitrary"), FFN_VMEM_LIMIT),
        name="conv_ffn",
    )(x2, g, sc, sh, gate, w_up, w_up, conv4, conv4, w_down, w_down, final_g)


def _swa_kernel(q_ref, kp_ref, ko_ref, vp_ref, vo_ref, bias_ref, sink_ref, o_ref, s_scr, p_scr,
                den_scr):
    n = pl.program_id(1)
    row = lax.broadcasted_iota(I32, (QB, 2 * QB), 0)
    col = lax.broadcasted_iota(I32, (QB, 2 * QB), 1)
    qc = row // CHUNK
    kc = col // CHUNK - QB // CHUNK
    allowed = (kc <= qc) & (kc >= qc - WINDOW_CHUNKS) & ((col >= QB) | (n > 0))
    addmask = jnp.where(allowed, 0.0, NEG_BIG)
    k_all = jnp.concatenate([kp_ref[0], ko_ref[0]], axis=0)
    v_all = jnp.concatenate([vp_ref[0], vo_ref[0]], axis=0)
    q_all = q_ref[0] * (HEAD_DIM ** -0.5)
    low = lax.broadcasted_iota(I32, (QB, LANES), 1) < HEAD_DIM
    zero = jnp.zeros((QB, LANES), q_all.dtype)
    grp = A_GROUP * QB

    for g in range(A_KV_HEADS):
        kg = k_all[:, g * HEAD_DIM:(g + 1) * HEAD_DIM]
        k2 = jnp.concatenate([kg, kg], axis=1)
        parts = []
        for c in range(A_GROUP // 2):
            col0 = (g * A_GROUP + 2 * c) * HEAD_DIM
            q2 = q_all[:, col0:col0 + LANES]
            parts += [jnp.where(low, q2, zero), jnp.where(low, zero, q2)]
        s_scr[g * grp:(g + 1) * grp, :] = lax.dot_general(
            jnp.concatenate(parts, axis=0), k2, (((1,), (1,)), ((), ())), preferred_element_type=F32)

    s = s_scr[...].reshape(A_HEADS, QB, 2 * QB) + bias_ref[...] + addmask[None]
    sink = sink_ref[...][:, :, 0:1]
    m = jnp.maximum(jnp.max(s, axis=-1, keepdims=True), sink)
    p = jnp.exp(s - m).astype(BF16).reshape(A_HEADS * QB, 2 * QB)
    p_scr[...] = p
    total = jnp.dot(p, jnp.ones((2 * QB, LANES), BF16), preferred_element_type=F32)
    sink_term = jnp.broadcast_to(jnp.exp(sink - m), (A_HEADS, QB, LANES))
    den_scr[...] = total + sink_term.reshape(A_HEADS * QB, LANES)

    outs = []
    for g in range(A_KV_HEADS):
        vg = v_all[:, g * HEAD_DIM:(g + 1) * HEAD_DIM]
        v2 = jnp.concatenate([vg, vg], axis=1)
        rows = slice(g * grp, (g + 1) * grp)
        o = jnp.dot(p_scr[rows, :], v2, preferred_element_type=F32) / den_scr[rows, :]
        for c in range(A_GROUP // 2):
            even = o[(2 * c) * QB:(2 * c + 1) * QB]
            odd = o[(2 * c + 1) * QB:(2 * c + 2) * QB]
            outs.append(jnp.where(low, even, odd))
    o_ref[0] = jnp.concatenate(outs, axis=-1).astype(o_ref.dtype)


def _swa(proj3, bias_a, sinks):
    bn, seq, _ = proj3.shape
    nblk = seq // QB
    kvw = A_KV_HEADS * HEAD_DIM
    qw = A_HEADS * HEAD_DIM
    sink_b = jnp.broadcast_to(sinks.astype(F32)[:, None, None], (A_HEADS, QB, LANES))
    prev = lambda c: (lambda b, n: (b, jnp.maximum(n - 1, 0), c))
    own = lambda c: (lambda b, n: (b, n, c))
    return pl.pallas_call(
        _swa_kernel,
        grid=(bn, nblk),
        in_specs=[
            pl.BlockSpec((1, QB, qw), own(EV_QA // qw)),
            pl.BlockSpec((1, QB, kvw), prev(EV_KA // kvw)),
            pl.BlockSpec((1, QB, kvw), own(EV_KA // kvw)),
            pl.BlockSpec((1, QB, kvw), prev(EV_VA // kvw)),
            pl.BlockSpec((1, QB, kvw), own(EV_VA // kvw)),
            pl.BlockSpec((A_HEADS, QB, 2 * QB), lambda b, n: (0, 0, 0)),
            pl.BlockSpec((A_HEADS, QB, LANES), lambda b, n: (0, 0, 0)),
        ],
        out_specs=pl.BlockSpec((1, QB, qw), lambda b, n: (b, n, 0)),
        out_shape=jax.ShapeDtypeStruct((bn, seq, qw), BF16),
        scratch_shapes=[
            pltpu.VMEM((A_HEADS * QB, 2 * QB), F32),
            pltpu.VMEM((A_HEADS * QB, 2 * QB), BF16),
            pltpu.VMEM((A_HEADS * QB, LANES), F32),
        ],
        compiler_params=_cparams(("arbitrary", "arbitrary")),
        name="swa",
    )(proj3, proj3, proj3, proj3, proj3, bias_a, sink_b)


KCH = MXU_DIM
DSA_MIN_DENOM = 2.0 ** -60
BF16_SLACK = 1.0 + 2.0 ** -7


def _dsa_kernel(qb_ref, qi_ref, kq_ref, ckv_ref, kk_ref, wuk_ref, wuv_ref, bias_ref, cb_ref,
                mb_ref, o_ref,
                qst_scr, wrow_scr, key_scr, mask_scr, qlat_scr, m_scr, l_scr, acc_scr, cut_scr,
                ckv_scr, kk_scr, *, topk):
    n = pl.program_id(1)
    nch = n // 2 + 1
    spad = key_scr.shape[0]

    @pl.when(n == 0)
    def _():
        ckv_scr[0:QB, :] = jnp.zeros((QB, ckv_scr.shape[1]), ckv_scr.dtype)
        kk_scr[0:QB, :] = jnp.zeros((QB, kk_scr.shape[1]), kk_scr.dtype)
        ckv_scr[QB:, :] = ckv_ref[0]
        kk_scr[QB:, :] = kk_ref[0]

    def chunk_start(e):
        return pl.multiple_of((n - 2 * e) * QB, QB)

    idx_scale = IDX_DIM ** -0.5 * IDX_HEADS ** -0.5
    w_t = jnp.transpose(kq_ref[0].astype(F32)) * idx_scale
    for h in range(IDX_HEADS):
        qst_scr[h * QB:(h + 1) * QB, :] = qi_ref[0, :, h * LANES:(h + 1) * LANES]
        wrow_scr[h] = jnp.broadcast_to(w_t[IDX_DIM + h:IDX_DIM + h + 1, :], (SUBLANES, LANES))
    q_all = qb_ref[0]
    for h in range(B_HEADS):
        ql = jnp.dot(q_all[:, h * HEAD_DIM:(h + 1) * HEAD_DIM], wuk_ref[h],
                     preferred_element_type=F32)
        ql = (ql * (HEAD_DIM ** -0.5 * LOG2E)).astype(BF16)
        qlat_scr[h * QB:(h + 1) * QB, :] = ql
        qf = ql.astype(F32)
        qn = jnp.sqrt(jnp.sum(qf * qf, axis=-1, keepdims=True))
        m_scr[h] = jnp.broadcast_to(qn, (QB, LANES)) * cb_ref[...] + mb_ref[h]

    key_row = lax.broadcasted_iota(I32, (KCH, LANES), 0)
    limit = (n * QB + CHUNK) + (lax.broadcasted_iota(I32, (1, LANES), 1) // CHUNK) * CHUNK

    def idx_chunk(e):
        ks = chunk_start(e)
        kj = kk_scr[pl.ds(ks, KCH), :]
        r = lax.dot_general(kj, qst_scr[...], (((1,), (1,)), ((), ())), preferred_element_type=F32)
        sc = jnp.maximum(r[:, 0:LANES], 0.0) * wrow_scr[0][0:1, :]
        for h in range(1, IDX_HEADS):
            sc = sc + jnp.maximum(r[:, h * LANES:(h + 1) * LANES], 0.0) * wrow_scr[h][0:1, :]
        bits = lax.bitcast_convert_type(sc, I32)
        key = bits ^ ((bits >> 31) & 0x7FFFFFFF)
        key = jnp.where(bits == INT_MIN, 0, key)
        kpos = ks - QB + key_row
        key = jnp.where((kpos >= 0) & (kpos < limit), key, INT_MIN)
        key_scr[pl.ds(ks, KCH), :] = key

    def idx_body(i, _):
        idx_chunk(2 * i)
        idx_chunk(jnp.minimum(2 * i + 1, nch - 1))
        return 0

    lax.fori_loop(0, (nch + 1) // 2, idx_body, 0)

    def count(pred):
        def hits(e):
            ks = chunk_start(e)
            hit = jnp.where(pred(key_scr[pl.ds(ks, KCH), :], ks + key_row), 1, 0)
            return jnp.sum(hit.reshape(KCH // SUBLANES, SUBLANES, LANES), axis=0)

        def pair(i, acc):
            return acc + (hits(2 * i) + hits(2 * i + 1))

        c = lax.fori_loop(0, nch // 2, pair, jnp.zeros((SUBLANES, LANES), I32))
        c = c + jnp.where(nch % 2 == 1, hits(nch - 1), 0)
        return jnp.sum(c, axis=0, keepdims=True)

    def bit_body(b, t):
        cand = t ^ jnp.left_shift(jnp.int32(1), 31 - b)
        tot = count(lambda k, pos: k >= cand)
        return jnp.where(tot >= topk, cand, t)

    thr = lax.fori_loop(0, 32, bit_body, jnp.full((1, LANES), INT_MIN, I32))
    thr = jnp.maximum(thr, INT_MIN + 1)

    n_gt = count(lambda k, pos: k > thr)
    n_eq = count(lambda k, pos: k == thr)
    need = topk - n_gt
    cut_scr[...] = jnp.full(cut_scr.shape, 2 * spad, I32)
    excess = jnp.max(jnp.where(n_eq > need, 1, 0))

    @pl.when(excess > 0)
    def _():
        nbits = int(spad).bit_length()

        def cut_body(b, c):
            cand = c | jnp.left_shift(jnp.int32(1), nbits - 1 - b)
            tot = count(lambda k, pos: (k == thr) & (pos < cand))
            return jnp.where(tot <= need, cand, c)

        c = lax.fori_loop(0, nbits, cut_body, jnp.zeros((1, LANES), I32))
        cut_scr[...] = jnp.broadcast_to(jnp.where(n_eq > need, c, 2 * spad), cut_scr.shape)

    cut = cut_scr[0:1, :]

    def mask_body(e, _):
        ks = chunk_start(e)
        k = key_scr[pl.ds(ks, KCH), :]
        sel = (k > thr) | ((k == thr) & (ks + key_row < cut))
        mask_scr[:, pl.ds(ks, KCH)] = jnp.transpose(jnp.where(sel, 0.0, NEG_BIG))
        return 0

    lax.fori_loop(0, nch, mask_body, 0)

    def masked_scores(e, near):
        ks = chunk_start(e)
        kv = ckv_scr[pl.ds(ks, KCH), :]
        addmask = mask_scr[:, pl.ds(ks, KCH)]
        s = lax.dot_general(qlat_scr[...], kv, (((1,), (1,)), ((), ())), preferred_element_type=F32)
        s = s.reshape(B_HEADS, QB, KCH) + addmask[None]
        if near:
            s = s + bias_ref[...]
        return s, kv

    def max_pass(e, near):
        s, _ = masked_scores(e, near)
        m_scr[...] = jnp.maximum(m_scr[...], jnp.maximum(s[:, :, :LANES], s[:, :, LANES:]))

    def sum_pass(e, near):
        s, kv = masked_scores(e, near)
        m = m_scr[...]
        p_lo = jnp.exp2(s[:, :, :LANES] - m)
        p_hi = jnp.exp2(s[:, :, LANES:] - m)
        l_scr[...] += p_lo + p_hi
        p = jnp.concatenate([p_lo, p_hi], axis=-1).astype(BF16).reshape(B_HEADS * QB, KCH)
        acc_scr[...] += jnp.dot(p, kv, preferred_element_type=F32)

    def sweep(fn):
        def far_pair(i, _):
            fn(2 * i + 1, False)
            fn(2 * i + 2, False)
            return 0
        lax.fori_loop(0, (nch - 1) // 2, far_pair, 0)

        @pl.when((nch - 1) % 2 == 1)
        def _():
            fn(nch - 1, False)

        fn(0, True)

    def accumulate():
        l_scr[...] = jnp.zeros(l_scr.shape, F32)
        acc_scr[...] = jnp.zeros(acc_scr.shape, F32)
        sweep(sum_pass)

    accumulate()
    smallest = jnp.min(jnp.sum(l_scr[...], axis=-1, keepdims=True))

    @pl.when(jnp.logical_not(smallest >= DSA_MIN_DENOM))
    def _():
        m_scr[...] = jnp.full(m_scr.shape, NEG_BIG, F32)
        sweep(max_pass)
        m_scr[...] = jnp.broadcast_to(jnp.max(m_scr[...], axis=-1, keepdims=True), m_scr.shape)
        accumulate()

    outs = []
    for h in range(B_HEADS):
        l = jnp.sum(l_scr[h], axis=-1, keepdims=True)
        o_lat = (acc_scr[h * QB:(h + 1) * QB, :] / l).astype(BF16)
        outs.append(jnp.dot(o_lat, wuv_ref[h], preferred_element_type=F32))
    o_ref[0] = jnp.concatenate(outs, axis=-1).astype(o_ref.dtype)


def _dsa(proj3, w_uk, w_uv, bias_b, kv_norm_g):
    bn, seq, _ = proj3.shape
    nblk = seq // QB
    spad = seq + QB
    qw = B_HEADS * HEAD_DIM
    iw = IDX_HEADS * LANES
    key_norm = jnp.max(jnp.abs(kv_norm_g.astype(F32))) * (KV_RANK ** 0.5 * BF16_SLACK)
    cb = jnp.broadcast_to(key_norm, (1, LANES))
    mb = jnp.broadcast_to(jnp.maximum(jnp.max(bias_b, axis=(1, 2)), 0.0)[:, None, None],
                          (B_HEADS, 1, LANES))
    return pl.pallas_call(
        functools.partial(_dsa_kernel, topk=min(TOPK_MAX, seq // 4)),
        grid=(bn, nblk),
        in_specs=[
            pl.BlockSpec((1, QB, qw), lambda b, n: (b, n, EV_QB // qw)),
            pl.BlockSpec((1, QB, iw), lambda b, n: (b, n, EV_QI // iw)),
            pl.BlockSpec((1, QB, LANES), lambda b, n: (b, n, EV_KW // LANES)),
            pl.BlockSpec((1, seq, KV_RANK), lambda b, n: (b, 0, EV_CL // KV_RANK)),
            pl.BlockSpec((1, seq, LANES), lambda b, n: (b, 0, EV_KW // LANES)),
            pl.BlockSpec(w_uk.shape, lambda b, n: (0, 0, 0)),
            pl.BlockSpec(w_uv.shape, lambda b, n: (0, 0, 0)),
            pl.BlockSpec((B_HEADS, QB, KCH), lambda b, n: (0, 0, 0)),
            pl.BlockSpec((1, LANES), lambda b, n: (0, 0)),
            pl.BlockSpec((B_HEADS, 1, LANES), lambda b, n: (0, 0, 0)),
        ],
        out_specs=pl.BlockSpec((1, QB, qw), lambda b, n: (b, n, 0)),
        out_shape=jax.ShapeDtypeStruct((bn, seq, qw), BF16),
        scratch_shapes=[
            pltpu.VMEM((IDX_HEADS * QB, LANES), BF16),
            pltpu.VMEM((IDX_HEADS, SUBLANES, LANES), F32),
            pltpu.VMEM((spad, LANES), I32),
            pltpu.VMEM((QB, spad), F32),
            pltpu.VMEM((B_HEADS * QB, KV_RANK), BF16),
            pltpu.VMEM((B_HEADS, QB, LANES), F32),
            pltpu.VMEM((B_HEADS, QB, LANES), F32),
            pltpu.VMEM((B_HEADS * QB, KV_RANK), F32),
            pltpu.VMEM((SUBLANES, LANES), I32),
            pltpu.VMEM((spad, KV_RANK), BF16),
            pltpu.VMEM((spad, LANES), BF16),
        ],
        compiler_params=_cparams(("arbitrary", "arbitrary")),
        name="dsa",
    )(proj3, proj3, proj3, proj3, proj3, w_uk, w_uv, bias_b, cb, mb)


SB_HG = 8


def _sb_kernel(q_ref, k_ref, v_ref, o_ref, acc_scr, carry_scr):
    n = pl.program_id(2)
    scale = C_HEAD_DIM ** -0.5
    rows = SB_HG * QB

    def sums_operand(width):
        r = lax.broadcasted_iota(I32, (width, width), 0)
        c = lax.broadcasted_iota(I32, (width, width), 1)
        suffix = jnp.where(r > c, 1.0, 0.0).astype(BF16)
        sums = jnp.concatenate([suffix, jnp.ones((width, QB), BF16)], axis=1)
        return jnp.concatenate([sums, sums], axis=0)

    def head_cols(h):
        return slice(h * C_HEAD_DIM, (h + 1) * C_HEAD_DIM)

    def chunk(ks, width, diagonal):
        z = jnp.concatenate(
            [lax.dot_general(q_ref[0, :, head_cols(h)], k_ref[0, pl.ds(ks, width), head_cols(h)],
                             (((1,), (1,)), ((), ())), preferred_element_type=F32)
             for h in range(SB_HG)], axis=0) * scale
        sp = jnp.maximum(z, 0.0) + jnp.log(1.0 + jnp.exp(-jnp.abs(z)))
        if diagonal:
            key = lax.broadcasted_iota(I32, (rows, width), 1) - (width - QB)
            earlier = key < (lax.broadcasted_iota(I32, (rows, width), 0) % QB)
            lk = jnp.where(earlier, -sp, 0.0)
        else:
            lk = -sp
        hi = lk.astype(BF16)
        lo = (lk - hi.astype(F32)).astype(BF16)
        inner = jnp.dot(jnp.concatenate([hi, lo], axis=1), sums_operand(width),
                        preferred_element_type=F32)
        carry = carry_scr[...]
        a = jnp.exp((z - sp) + (jnp.concatenate([carry] * (width // QB), axis=1) + inner[:, :width]))
        if diagonal:
            a = jnp.where(earlier, a, 0.0)
        a = a.astype(BF16)
        for h in range(SB_HG):
            r = slice(h * QB, (h + 1) * QB)
            acc_scr[r, :] += jnp.dot(a[r], v_ref[0, pl.ds(ks, width), head_cols(h)],
                                     preferred_element_type=F32)
        carry = carry + inner[:, width:]
        carry_scr[...] = carry
        return (jnp.max(carry) < SB_DONE).astype(I32)

    acc_scr[...] = jnp.zeros(acc_scr.shape, F32)
    carry_scr[...] = jnp.zeros(carry_scr.shape, F32)

    @pl.when(n % 2 == 0)
    def _():
        chunk(pl.multiple_of(n * QB, QB), QB, True)

    @pl.when(n % 2 == 1)
    def _():
        chunk(pl.multiple_of((n - 1) * QB, QB), 2 * QB, True)

    def cond(state):
        j, done = state
        return (j >= 1) & (done == 0)

    def body(state):
        j, _ = state
        return j - 2, chunk(pl.multiple_of((j - 1) * QB, QB), 2 * QB, False)

    lax.while_loop(cond, body, (n - 1 - n % 2, jnp.int32(0)))
    o_ref[0] = jnp.concatenate([acc_scr[h * QB:(h + 1) * QB, :] for h in range(SB_HG)],
                               axis=-1).astype(o_ref.dtype)


def _sb(qkv3):
    bn, seq, _ = qkv3.shape
    nblk = seq // QB
    gw = SB_HG * C_HEAD_DIM
    ngrp = C_HEADS // SB_HG
    return pl.pallas_call(
        _sb_kernel,
        grid=(bn, ngrp, nblk),
        in_specs=[
            pl.BlockSpec((1, QB, gw), lambda b, g, n: (b, n, g)),
            pl.BlockSpec((1, seq, gw), lambda b, g, n: (b, 0, ngrp + g)),
            pl.BlockSpec((1, seq, gw), lambda b, g, n: (b, 0, 2 * ngrp + g)),
        ],
        out_specs=pl.BlockSpec((1, QB, gw), lambda b, g, n: (b, n, g)),
        out_shape=jax.ShapeDtypeStruct((bn, seq, C_HEADS * C_HEAD_DIM), BF16),
        scratch_shapes=[pltpu.VMEM((SB_HG * QB, C_HEAD_DIM), F32),
                        pltpu.VMEM((SB_HG * QB, QB), F32)],
        compiler_params=_cparams(("arbitrary", "arbitrary", "arbitrary")),
        name="stick_breaking",
    )(qkv3, qkv3, qkv3)


def _t5_bucket(rel):
    half = NUM_BUCKETS // 2
    max_exact = half // 2
    n = jnp.abs(rel)
    nf = jnp.maximum(n, 1).astype(F32)
    large = max_exact + (jnp.log(nf / max_exact) / math.log(MAX_DISTANCE / max_exact)
                         * (half - max_exact)).astype(I32)
    large = jnp.minimum(large, half - 1)
    return jnp.where(rel > 0, half, 0) + jnp.where(n < max_exact, n, large)


def _band_bias(rel_bias):
    rel = (jnp.arange(2 * QB) - QB)[None, :] - jnp.arange(QB)[:, None]
    table = rel_bias.astype(F32)
    bucket = _t5_bucket(rel)[None]
    band = sum(jnp.where(bucket == k, table[k][:, None, None], 0.0) for k in range(NUM_BUCKETS))
    far = table[_t5_bucket(jnp.int32(-(QB + 1)))]
    return band, far


def _even_w_in(w_in):
    d = w_in.shape[0]
    sizes = (A_HEADS * HEAD_DIM, A_KV_HEADS * HEAD_DIM, A_KV_HEADS * HEAD_DIM,
             B_HEADS * HEAD_DIM, KV_RANK, IDX_HEADS * IDX_DIM, IDX_DIM, IDX_HEADS)
    offs = [int(o) for o in np.cumsum((0,) + sizes)]
    w = w_in.astype(BF16)
    moves = [(offs[0], offs[1], EV_QA), (offs[1], offs[2], EV_KA), (offs[2], offs[3], EV_VA),
             (offs[3], offs[4], EV_QB), (offs[4], offs[5], EV_CL), (offs[6], offs[8], EV_KW)]
    moves += [(offs[5] + h * IDX_DIM, offs[5] + (h + 1) * IDX_DIM, EV_QI + h * LANES)
              for h in range(IDX_HEADS)]
    out = jnp.zeros((d, EV_WIDTH), BF16)
    for lo, hi, dst in moves:
        out = lax.dynamic_update_slice(out, w[:, lo:hi], (0, dst))
    return out


def kernel(x, c, rel_bias, ada_w, ada_b, norm_mix_g, norm_ffn_g, ev_w_in, ev_kv_norm_g, ev_w_uk,
           ev_w_uv, ev_sinks, ev_w_out, od_w_in, od_w_out, ffn_w_up, ffn_conv_w, ffn_conv_b,
           ffn_w_down, final_g):
    bn, seq, d = x.shape
    depth = ada_w.shape[0]
    x2 = x.reshape(bn * seq, d)

    mod = _ada_mod(c, ada_w, ada_b)
    band, far = _band_bias(rel_bias)
    bias_a = band[:A_HEADS]
    bias_b = (band[A_HEADS:] - far[A_HEADS:, None, None]) * LOG2E
    w_up_all = ffn_w_up.astype(BF16)
    w_down_all = ffn_w_down.astype(BF16)

    for i in range(depth):
        sh1, sc1, g1, sh2, sc2, g2 = [m.reshape(bn, 1, d) for m in jnp.split(mod[i], 6, axis=-1)]
        g_mix = norm_mix_g[i].reshape(1, d)
        j = i // 2
        if i % 2 == 0:
            proj = _in_proj(x2, g_mix, sc1, sh1, _even_w_in(ev_w_in[j]),
                            ev_kv_norm_g[j].reshape(1, KV_RANK), seq, tm=IN_PROJ_TM, tn=IN_PROJ_TN,
                            kv_col=EV_CL)
            proj3 = proj.reshape(bn, seq, EV_WIDTH)
            o_a = _swa(proj3, bias_a, ev_sinks[j])
            o_b = _dsa(proj3, ev_w_uk[j].astype(BF16), ev_w_uv[j].astype(BF16), bias_b, ev_kv_norm_g[j])
            w_out = ev_w_out[j].astype(BF16)
            wa = w_out[:A_HEADS * HEAD_DIM]
            wb = w_out[A_HEADS * HEAD_DIM:]
            x2 = _out_proj([o_a.reshape(bn * seq, -1), o_b.reshape(bn * seq, -1)], [wa, wb],
                           x2, g1, seq, tm=OUT_PROJ_TM)
        else:
            qkv = _in_proj(x2, g_mix, sc1, sh1, od_w_in[j].astype(BF16),
                           jnp.ones((1, KV_RANK), F32), seq, tm=IN_PROJ_TM, tn=IN_PROJ_TN)
            o_c = _sb(qkv.reshape(bn, seq, -1))
            x2 = _out_proj([o_c.reshape(bn * seq, -1)], [od_w_out[j].astype(BF16)], x2, g1, seq,
                           tm=OUT_PROJ_TM)
        conv4 = jnp.concatenate([ffn_conv_w[i], ffn_conv_b[i][None]], axis=0)
        x2 = _conv_ffn(x2, norm_ffn_g[i].reshape(1, d), sc2, sh2, g2, w_up_all, conv4, w_down_all,
                       final_g.reshape(1, d), seq, layer=i, tm=FFN_TM, tf=FFN_TF,
                       final=(i == depth - 1))
    return x2.reshape(bn, seq, d)
```
